```python
import jax, jax.numpy as jnp
from jax import lax
import numpy as np

D_MODEL = 1024
BATCH = 16
SEQ = 2048
DEPTH = 4

A_HEADS = 4
A_DK = 128
A_DV = 128
A_WIDTH = A_HEADS * A_DV
B_HEADS = 4
B_DK = 64
B_DV = 128
B_QK = B_HEADS * B_DK
B_WIDTH = B_HEADS * B_DV
C_HEADS = 4
C_DK = 64
C_DV = 128
C_QK = C_HEADS * C_DK
C_WIDTH = C_HEADS * C_DV
GLA_RANK = 16
GLA_TAU = 16.0
N_BRANCH = 3
SPLIT_SIZES = (
    A_WIDTH, A_HEADS * A_DK, A_HEADS * A_DK, A_WIDTH, A_WIDTH,
    B_QK, B_QK, B_WIDTH, B_WIDTH,
    C_QK, C_QK, C_WIDTH, C_WIDTH,
    2 * GLA_RANK,
    D_MODEL, D_MODEL, D_MODEL,
)
N_IN = 5 * A_WIDTH + 2 * B_QK + 2 * B_WIDTH + 2 * C_QK + 2 * C_WIDTH + 2 * GLA_RANK + N_BRANCH * D_MODEL
D_FF = ((8 * D_MODEL + 3 * 256 - 1) // (3 * 256)) * 256
CHUNK = 64
ROPE_BASE = 10000.0
EPS = 1e-6
TINY = 1e-30

kernel_name = "hybrid_hgrn2_retnet_gla_adaln_encoder"


def rms_norm(x, gain):
    x32 = x.astype(jnp.float32)
    y = x32 * lax.rsqrt(jnp.mean(x32 * x32, axis=-1, keepdims=True) + EPS)
    return (y * gain.astype(jnp.float32)).astype(x.dtype)


def head_rms(o):
    return o * lax.rsqrt(jnp.mean(o * o, axis=-1, keepdims=True) + EPS)


def head_group_norm(o):
    mu = jnp.mean(o, axis=-1, keepdims=True)
    var = jnp.mean(jnp.square(o - mu), axis=-1, keepdims=True)
    return (o - mu) * lax.rsqrt(var + EPS)


def rotary(x):
    seqlen, d = x.shape[1], x.shape[-1]
    pos = jnp.arange(seqlen, dtype=jnp.float32)
    inv_freq = ROPE_BASE ** (-jnp.arange(0, d, 2, dtype=jnp.float32) / d)
    ang = pos[:, None] * inv_freq[None, :]
    cos = jnp.cos(ang)[None, :, None, :]
    sin = jnp.sin(ang)[None, :, None, :]
    x32 = x.astype(jnp.float32)
    x1, x2 = x32[..., : d // 2], x32[..., d // 2:]
    return jnp.concatenate([x1 * cos - x2 * sin, x1 * sin + x2 * cos], axis=-1).astype(x.dtype)


def chunked_gated_linear_attention(q, k, v, log_g, inclusive):
    bsz, seqlen, heads, dk = q.shape
    dv = v.shape[-1]
    n = seqlen // CHUNK

    def blocks(a):
        a = a.astype(jnp.float32).reshape(bsz, n, CHUNK, heads, a.shape[-1])
        return a.transpose(1, 0, 3, 2, 4)

    mask = jnp.tril(jnp.ones((CHUNK, CHUNK), dtype=bool), k=0 if inclusive else -1)[:, :, None]

    def step(state, inp):
        qb, kb, vb, gb = inp
        cum = jnp.cumsum(gb, axis=2)
        last = cum[:, :, -1:, :]
        diff = jnp.where(mask, cum[:, :, :, None, :] - cum[:, :, None, :, :], 0.0)
        rel = jnp.where(mask, jnp.exp(diff), 0.0)
        scores = jnp.einsum('bhid,bhjd,bhijd->bhij', qb, kb, rel)
        out = (jnp.einsum('bhij,bhjv->bhiv', scores, vb)
               + jnp.einsum('bhid,bhdv->bhiv', qb * jnp.exp(cum), state))
        state = (state * jnp.exp(last[:, :, 0, :, None])
                 + jnp.einsum('bhjd,bhjv->bhdv', kb * jnp.exp(last - cum), vb))
        return state, out

    init = jnp.zeros((bsz, heads, dk, dv), jnp.float32)
    _, out = lax.scan(step, init, (blocks(q), blocks(k), blocks(v), blocks(log_g)))
    return out.transpose(1, 0, 3, 2, 4).reshape(bsz, seqlen, heads, dv)


def bidirectional_scan(q, k_fwd, k_bwd, v, log_g_fwd, log_g_bwd):
    fwd = chunked_gated_linear_attention(q, k_fwd, v, log_g_fwd, True)
    flip = lambda a: jnp.flip(a, axis=1)
    bwd = flip(chunked_gated_linear_attention(flip(q), flip(k_bwd), flip(v), flip(log_g_bwd), False))
    return fwd + bwd


def retention_log_decay(reverse):
    h = jnp.arange(B_HEADS, dtype=jnp.float32)
    if reverse:
        h = h[::-1]
    return jnp.log1p(-jnp.exp2(-5.0 - h))


def hgrn2_lower_bounds(lb_logits):
    p = jax.nn.softmax(lb_logits.astype(jnp.float32), axis=0)
    return jnp.maximum(jnp.cumsum(p, axis=0) - p[0:1], 0.0)


def mixer_sublayer(h, lb, norm_a_g, norm_b_g, norm_c_g, w_in, w_alpha, b_alpha, w_pa, w_pb, w_pc, w_out):
    bsz, seqlen, _ = h.shape
    dt = h.dtype
    z = h @ w_in
    idx = np.cumsum(SPLIT_SIZES)[:-1]
    (a_q, a_ff, a_fb, a_i, a_g, b_q, b_k, b_v, b_g, c_q, c_k, c_v, c_g, c_lr,
     gate_a, gate_b, gate_c) = jnp.split(z, idx, axis=-1)
    heads = lambda t, n: t.reshape(bsz, seqlen, n, -1)

    lb32 = lb.astype(jnp.float32).reshape(A_HEADS, A_DK)
    def forget(zf):
        zf = heads(zf, A_HEADS).astype(jnp.float32)
        f = lb32 + (1.0 - lb32) * jax.nn.sigmoid(zf)
        log_f = jnp.log(jnp.maximum(f, TINY))
        key = (1.0 - lb32) * jax.nn.sigmoid(-zf)
        return log_f, key
    log_f_fwd, key_fwd = forget(a_ff)
    log_f_bwd, key_bwd = forget(a_fb)
    qa = heads(jax.nn.silu(a_q), A_HEADS)
    o_a = bidirectional_scan(qa, key_fwd, key_bwd, heads(a_i, A_HEADS), log_f_fwd, log_f_bwd)
    y_a = (head_rms(o_a).reshape(bsz, seqlen, A_WIDTH) * norm_a_g).astype(dt) * jax.nn.sigmoid(a_g)

    qb = rotary(heads(b_q, B_HEADS))
    kb = rotary(heads(b_k, B_HEADS)) * (B_DK ** -0.5)
    shape_b = (bsz, seqlen, B_HEADS, B_DK)
    lg_fwd = jnp.broadcast_to(retention_log_decay(False)[None, None, :, None], shape_b)
    lg_bwd = jnp.broadcast_to(retention_log_decay(True)[None, None, :, None], shape_b)
    o_b = bidirectional_scan(qb, kb, kb, heads(b_v, B_HEADS), lg_fwd, lg_bwd)
    y_b = (head_group_norm(o_b).reshape(bsz, seqlen, B_WIDTH) * norm_b_g).astype(dt) * jax.nn.silu(b_g)

    lr = c_lr.reshape(bsz, seqlen, 2, GLA_RANK)
    alpha_logits = jnp.einsum('btnr,nrk->btnk', lr, w_alpha) + b_alpha
    log_alpha = jax.nn.log_sigmoid(alpha_logits.astype(jnp.float32)) / GLA_TAU
    la_fwd = heads(log_alpha[:, :, 0], C_HEADS)
    la_bwd = heads(log_alpha[:, :, 1], C_HEADS)
    qc = heads(c_q, C_HEADS) * (C_DK ** -0.5)
    kc = heads(c_k, C_HEADS)
    o_c = bidirectional_scan(qc, kc, kc, heads(c_v, C_HEADS), la_fwd, la_bwd)
    y_c = (head_rms(o_c).reshape(bsz, seqlen, C_WIDTH) * norm_c_g).astype(dt) * jax.nn.silu(c_g)

    merged = (jax.nn.sigmoid(gate_a) * (y_a @ w_pa)
              + jax.nn.sigmoid(gate_b) * (y_b @ w_pb)
              + jax.nn.sigmoid(gate_c) * (y_c @ w_pc))
    return merged @ w_out


def setup_inputs(seed: int = 0) -> dict:
    key = jax.random.key(seed)
    ks = jax.random.split(key, 24)
    f32 = jnp.float32
    nrm = lambda k, shape, scale: jax.random.normal(k, shape, f32) * scale
    gain = lambda k, shape: 1.0 + 0.01 * jax.random.normal(k, shape, f32)
    return {
        "x": jax.random.normal(ks[0], (BATCH, SEQ, D_MODEL), f32),
        "c": jax.random.normal(ks[1], (BATCH, D_MODEL), f32),
        "norm1_g": gain(ks[2], (DEPTH, D_MODEL)),
        "w_ada": nrm(ks[3], (DEPTH, D_MODEL, 6 * D_MODEL), D_MODEL ** -0.5),
        "b_ada": nrm(ks[4], (DEPTH, 6 * D_MODEL), 0.01),
        "w_in": nrm(ks[5], (DEPTH, D_MODEL, N_IN), D_MODEL ** -0.5),
        "lb_logits": nrm(ks[6], (DEPTH, A_HEADS * A_DK), 0.5),
        "norm_a_g": gain(ks[7], (DEPTH, A_WIDTH)),
        "norm_b_g": gain(ks[8], (DEPTH, B_WIDTH)),
        "norm_c_g": gain(ks[9], (DEPTH, C_WIDTH)),
        "w_alpha": nrm(ks[10], (DEPTH, 2, GLA_RANK, C_QK), GLA_RANK ** -0.5),
        "b_alpha": nrm(ks[11], (DEPTH, 2, C_QK), 0.1),
        "w_pa": nrm(ks[12], (DEPTH, A_WIDTH, D_MODEL), A_WIDTH ** -0.5),
        "w_pb": nrm(ks[13], (DEPTH, B_WIDTH, D_MODEL), B_WIDTH ** -0.5),
        "w_pc": nrm(ks[14], (DEPTH, C_WIDTH, D_MODEL), C_WIDTH ** -0.5),
        "w_out": nrm(ks[15], (DEPTH, D_MODEL, D_MODEL), D_MODEL ** -0.5),
        "norm2_g": gain(ks[16], (DEPTH, D_MODEL)),
        "w_ffn_in": nrm(ks[17], (DEPTH, D_MODEL, 2 * D_FF), D_MODEL ** -0.5),
        "w_ffn_out": nrm(ks[18], (DEPTH, D_FF, D_MODEL), D_FF ** -0.5),
        "norm_f_g": gain(ks[19], (D_MODEL,)),
    }


def reference(x, c, norm1_g, w_ada, b_ada, w_in, lb_logits, norm_a_g, norm_b_g, norm_c_g,
              w_alpha, b_alpha, w_pa, w_pb, w_pc, w_out, norm2_g, w_ffn_in, w_ffn_out, norm_f_g):
    lower_bounds = hgrn2_lower_bounds(lb_logits)
    c_act = jax.nn.silu(c)
    for l in range(DEPTH):
        mod = c_act @ w_ada[l] + b_ada[l]
        sh1, sc1, g1, sh2, sc2, g2 = [m[:, None, :] for m in jnp.split(mod, 6, axis=-1)]
        h = rms_norm(x, norm1_g[l]) * (1.0 + sc1) + sh1
        x = x + g1 * mixer_sublayer(h, lower_bounds[l], norm_a_g[l], norm_b_g[l], norm_c_g[l], w_in[l],
                                    w_alpha[l], b_alpha[l], w_pa[l], w_pb[l], w_pc[l], w_out[l])
        h2 = rms_norm(x, norm2_g[l]) * (1.0 + sc2) + sh2
        gate, up = jnp.split(h2 @ w_ffn_in[l], 2, axis=-1)
        x = x + g2 * ((jax.nn.silu(gate) * up) @ w_ffn_out[l])
    return rms_norm(x, norm_f_g)
```

```python
import functools

import jax
import jax.numpy as jnp
from jax import lax
from jax.experimental import pallas as pl
from jax.experimental.pallas import tpu as pltpu

F32 = jnp.float32
BF16 = jnp.bfloat16

D_MODEL = 1024
N_HEADS = 4
A_DK = 128
BC_DK = 64
DV = 128
GLA_RANK = 16
GLA_TAU = 16.0
D_FF = 2816
EPS = 1e-6
TINY = 1e-30

LANES = 128
CHUNK = 64
VMEM_LIMIT = 48 * 1024 * 1024

ZB_GATE = 0
ZB_AQ, ZB_AI, ZB_AG = 3072, 3584, 4096
ZB_BQ, ZB_BK, ZB_BV, ZB_BG = 4608, 4864, 5120, 5632
ZB_CQ, ZB_CK, ZB_CV, ZB_CG = 6144, 6400, 6656, 7168
ZB_COLS = 7680
ZF_FF, ZF_FB, ZF_LR = 0, 512, 1024
ZF_COLS = 1152


def _dot(a, b):
    return jnp.dot(a, b, preferred_element_type=F32)


def _dot_nt(a, b):
    return lax.dot_general(a, b, (((1,), (1,)), ((), ())), preferred_element_type=F32)


def _dot_tn(a, b):
    return lax.dot_general(a, b, (((0,), (0,)), ((), ())), preferred_element_type=F32)


def _sigmoid(x):
    return jax.nn.sigmoid(x)


def _mod_kernel(c_ref, w_ref, b_ref, o_ref):
    c = c_ref[...]
    a = c * _sigmoid(c)
    w = w_ref[...]
    a_hi = a.astype(BF16)
    a_lo = (a - a_hi.astype(F32)).astype(BF16)
    w_hi = w.astype(BF16)
    w_lo = (w - w_hi.astype(F32)).astype(BF16)
    o_ref[...] = _dot(a_hi, w_hi) + _dot(a_hi, w_lo) + _dot(a_lo, w_hi) + b_ref[...]


def _modulation(c, w_ada, b_ada):
    depth, d, n6 = w_ada.shape
    bsz = c.shape[0]
    return pl.pallas_call(
        _mod_kernel,
        grid=(depth, n6 // d),
        in_specs=[
            pl.BlockSpec((bsz, d), lambda l, j: (0, 0)),
            pl.BlockSpec((None, d, d), lambda l, j: (l, 0, j)),
            pl.BlockSpec((None, 1, d), lambda l, j: (l, 0, j)),
        ],
        out_specs=pl.BlockSpec((None, bsz, d), lambda l, j: (l, 0, j)),
        out_shape=jax.ShapeDtypeStruct((depth, bsz, n6), F32),
        name="adaln_modulation",
    )(c, w_ada, b_ada.reshape(depth, 1, n6))


def _lb_kernel(x_ref, o_ref):
    depth = x_ref.shape[0]
    rows = [x_ref[i:i + 1, :] for i in range(depth)]
    m = rows[0]
    for r in rows[1:]:
        m = jnp.maximum(m, r)
    e = [jnp.exp(r - m) for r in rows]
    s = e[0]
    for t in e[1:]:
        s = s + t
    p = [t / s for t in e]
    acc = p[0]
    o_ref[0:1, :] = jnp.maximum(acc - p[0], 0.0)
    for i in range(1, depth):
        acc = acc + p[i]
        o_ref[i:i + 1, :] = jnp.maximum(acc - p[0], 0.0)


def _lower_bounds(lb_logits):
    return pl.pallas_call(
        _lb_kernel,
        out_shape=jax.ShapeDtypeStruct(lb_logits.shape, F32),
        name="hgrn2_lower_bounds",
    )(lb_logits)


def _norm_mod(x, gain, sc, sh):
    y = x * lax.rsqrt(jnp.mean(x * x, axis=-1, keepdims=True) + EPS)
    return (y * gain) * (1.0 + sc) + sh


def _inproj_kernel(x_ref, sh_ref, sc_ref, g_ref, w_ref, o_ref, h_ref):
    @pl.when(pl.program_id(1) == 0)
    def _():
        h_ref[...] = _norm_mod(x_ref[...], g_ref[...], sc_ref[0], sh_ref[0]).astype(BF16)

    o_ref[...] = _dot(h_ref[...], w_ref[...]).astype(o_ref.dtype)


def _inproj(x2, sh, sc, gain, w, out_dtype, seqlen, tm, tn, name):
    m, d = x2.shape
    n = w.shape[1]
    per = seqlen // tm
    return pl.pallas_call(
        _inproj_kernel,
        grid=(m // tm, n // tn),
        in_specs=[
            pl.BlockSpec((tm, d), lambda i, j: (i, 0)),
            pl.BlockSpec((1, 1, d), lambda i, j: (i // per, 0, 0)),
            pl.BlockSpec((1, 1, d), lambda i, j: (i // per, 0, 0)),
            pl.BlockSpec((1, d), lambda i, j: (0, 0)),
            pl.BlockSpec((d, tn), lambda i, j: (0, j)),
        ],
        out_specs=pl.BlockSpec((tm, tn), lambda i, j: (i, j)),
        out_shape=jax.ShapeDtypeStruct((m, n), out_dtype),
        scratch_shapes=[pltpu.VMEM((tm, d), BF16)],
        compiler_params=pltpu.CompilerParams(
            dimension_semantics=("parallel", "arbitrary"), vmem_limit_bytes=VMEM_LIMIT),
        name=name,
    )(x2, sh, sc, gain, w)


def _merge_kernel(x_ref, ya_ref, yb_ref, yc_ref, ga_ref, gb_ref, gc_ref,
                  wpa_ref, wpb_ref, wpc_ref, wout_ref, g1_ref, o_ref):
    merged = (_sigmoid(ga_ref[...].astype(F32)) * _dot(ya_ref[...], wpa_ref[...])
              + _sigmoid(gb_ref[...].astype(F32)) * _dot(yb_ref[...], wpb_ref[...])
              + _sigmoid(gc_ref[...].astype(F32)) * _dot(yc_ref[...], wpc_ref[...]))
    out = _dot(merged.astype(BF16), wout_ref[...])
    o_ref[...] = x_ref[...] + g1_ref[0] * out


def _merge(x2, ya, yb, yc, zb, wpa, wpb, wpc, wout, g1, seqlen, tm):
    m, d = x2.shape
    w = ya.shape[1]
    per = seqlen // tm
    row = lambda i: (i, 0)
    const = lambda i: (0, 0)
    return pl.pallas_call(
        _merge_kernel,
        grid=(m // tm,),
        in_specs=[
            pl.BlockSpec((tm, d), row),
            pl.BlockSpec((tm, w), row), pl.BlockSpec((tm, w), row), pl.BlockSpec((tm, w), row),
            pl.BlockSpec((tm, d), lambda i: (i, ZB_GATE // d)),
            pl.BlockSpec((tm, d), lambda i: (i, ZB_GATE // d + 1)),
            pl.BlockSpec((tm, d), lambda i: (i, ZB_GATE // d + 2)),
            pl.BlockSpec((w, d), const), pl.BlockSpec((w, d), const), pl.BlockSpec((w, d), const),
            pl.BlockSpec((d, d), const),
            pl.BlockSpec((1, 1, d), lambda i: (i // per, 0, 0)),
        ],
        out_specs=pl.BlockSpec((tm, d), row),
        out_shape=jax.ShapeDtypeStruct((m, d), F32),
        compiler_params=pltpu.CompilerParams(
            dimension_semantics=("parallel",), vmem_limit_bytes=VMEM_LIMIT),
        name="merge_outproj",
    )(x2, ya, yb, yc, zb, zb, zb, wpa, wpb, wpc, wout, g1)


def _ffn_kernel(x_ref, sh_ref, sc_ref, g2_ref, ng_ref, wg_ref, wu_ref, wo_ref, nf_ref,
                o_ref, h_ref, acc_ref, *, final_norm):
    k = pl.program_id(1)

    @pl.when(k == 0)
    def _():
        h_ref[...] = _norm_mod(x_ref[...], ng_ref[...], sc_ref[0], sh_ref[0]).astype(BF16)
        acc_ref[...] = jnp.zeros_like(acc_ref)

    h = h_ref[...]
    gate = _dot(h, wg_ref[...])
    up = _dot(h, wu_ref[...])
    act = (gate * _sigmoid(gate) * up).astype(BF16)
    acc_ref[...] += _dot(act, wo_ref[...])

    @pl.when(k == pl.num_programs(1) - 1)
    def _():
        xn = x_ref[...] + g2_ref[0] * acc_ref[...]
        if final_norm:
            xn = xn * lax.rsqrt(jnp.mean(xn * xn, axis=-1, keepdims=True) + EPS) * nf_ref[...]
        o_ref[...] = xn


def _ffn(x2, sh, sc, g2, ng, w_in, w_out, nf, seqlen, tm, tf, final_norm):
    m, d = x2.shape
    dff = w_out.shape[0]
    nk = dff // tf
    per = seqlen // tm
    bvec = lambda i, k: (i // per, 0, 0)
    return pl.pallas_call(
        functools.partial(_ffn_kernel, final_norm=final_norm),
        grid=(m // tm, nk),
        in_specs=[
            pl.BlockSpec((tm, d), lambda i, k: (i, 0)),
            pl.BlockSpec((1, 1, d), bvec), pl.BlockSpec((1, 1, d), bvec), pl.BlockSpec((1, 1, d), bvec),
            pl.BlockSpec((1, d), lambda i, k: (0, 0)),
            pl.BlockSpec((d, tf), lambda i, k: (0, k)),
            pl.BlockSpec((d, tf), lambda i, k: (0, nk + k)),
            pl.BlockSpec((tf, d), lambda i, k: (k, 0)),
            pl.BlockSpec((1, d), lambda i, k: (0, 0)),
        ],
        out_specs=pl.BlockSpec((tm, d), lambda i, k: (i, 0)),
        out_shape=jax.ShapeDtypeStruct((m, d), F32),
        scratch_shapes=[pltpu.VMEM((tm, d), BF16), pltpu.VMEM((tm, d), F32)],
        compiler_params=pltpu.CompilerParams(
            dimension_semantics=("parallel", "arbitrary"), vmem_limit_bytes=VMEM_LIMIT),
        name="swiglu_ffn",
    )(x2, sh, sc, g2, ng, w_in, w_in, w_out, nf)


N_LEVELS = CHUNK.bit_length() - 1


def _block_sums(g):
    row = lax.broadcasted_iota(jnp.int32, g.shape, 0)
    lb, tb = g, g
    out = [(lb, tb)]
    b = 1
    while b < CHUNK:
        right = (row & b) != 0
        from_left = pltpu.roll(tb, b, axis=0)
        from_right = pltpu.roll(tb, CHUNK - b, axis=0)
        lb = lb + jnp.where(right, from_left, 0.0)
        tb = tb + jnp.where(right, from_left, from_right)
        out.append((lb, tb))
        b *= 2
    return out


def _gated_factors(g, forward):
    sums = _block_sums(g)
    if forward:
        facs = [(jnp.exp(lb), jnp.exp(tb - lb)) for lb, tb in sums]
    else:
        facs = [(jnp.exp((tb - lb) + g), jnp.exp(lb - g)) for lb, tb in sums]
    return facs, jnp.exp(sums[-1][1][0:1, :])


def _fixed_factors(log_decay, forward):
    pos = lax.broadcasted_iota(jnp.int32, (CHUNK, LANES), 0).astype(F32)
    mid = CHUNK // 2
    if forward:
        facs = [(jnp.exp((pos - (mid - 1.0)) * log_decay), jnp.exp(((mid - 1.0) - pos) * log_decay)),
                (jnp.exp((pos + 1.0) * log_decay), jnp.exp((CHUNK - 1.0 - pos) * log_decay))]
    else:
        facs = [(jnp.exp((mid - pos) * log_decay), jnp.exp((pos - mid) * log_decay)),
                (jnp.exp((CHUNK - pos) * log_decay), jnp.exp(pos * log_decay))]
    return facs, jnp.exp(float(CHUNK) * log_decay)


def _tri_masks(forward):
    row = lax.broadcasted_iota(jnp.int32, (CHUNK, CHUNK), 0)
    col = lax.broadcasted_iota(jnp.int32, (CHUNK, CHUNK), 1)
    return [(row > col) if forward else (col > row)], row == col


def _level_masks(forward):
    row = lax.broadcasted_iota(jnp.int32, (CHUNK, CHUNK), 0)
    col = lax.broadcasted_iota(jnp.int32, (CHUNK, CHUNK), 1)
    differ = row ^ col
    side = (row > col) if forward else (col > row)
    return [((differ >> level) == 1) & side for level in range(N_LEVELS)], row == col


def _lane_masks(n_sub):
    if n_sub == 1:
        return [None]
    lane = lax.broadcasted_iota(jnp.int32, (1, LANES), 1)
    width = LANES // n_sub
    return [jnp.where((lane >= i * width) & (lane < (i + 1) * width), 1.0, 0.0) for i in range(n_sub)]


def _scan_step(q, k, factors, forward, masks, lane_masks, v_list, st_ref, slot0):
    facs, chunk_decay = factors
    level_masks, eye = masks
    kts = [(k * fk).astype(BF16) for _, fk in facs]
    outs = []
    for i, v in enumerate(v_list):
        qm = q if lane_masks[i] is None else q * lane_masks[i]
        p = jnp.sum(qm * k, axis=-1, keepdims=True) if forward else 0.0
        p = jnp.where(eye, p, 0.0)
        for level, mask in enumerate(level_masks):
            s = _dot_nt((qm * facs[level][0]).astype(BF16), kts[level])
            p = jnp.where(mask, s, p)
        st = st_ref[slot0 + i]
        inter = _dot_nt((qm * facs[-1][0]).astype(BF16), st.astype(BF16))
        outs.append(_dot(p.astype(BF16), v) + inter)
        st_ref[slot0 + i] = st * chunk_decay + _dot_tn(v, kts[-1])
    return outs


def _chunk_rows(n):
    return pl.ds(pl.multiple_of(n * CHUNK, CHUNK), CHUNK)


def _scan_both(n_chunks, n_sub, make_masks, prep_f, prep_b, load_v, of_ref, ob_ref, st_ref):
    st_ref[...] = jnp.zeros_like(st_ref)
    lane_masks = _lane_masks(n_sub)
    masks_f = make_masks(True)
    masks_b = make_masks(False)

    def fwd(n, carry):
        rows = _chunk_rows(n)
        q, k, factors = prep_f(rows)
        outs = _scan_step(q, k, factors, True, masks_f, lane_masks, load_v(rows), st_ref, 0)
        for i, o in enumerate(outs):
            of_ref[rows, i * DV:(i + 1) * DV] = o
        return carry

    def bwd(t, carry):
        rows = _chunk_rows(n_chunks - 1 - t)
        q, k, factors = prep_b(rows)
        outs = _scan_step(q, k, factors, False, masks_b, lane_masks, load_v(rows), st_ref, n_sub)
        for i, o in enumerate(outs):
            ob_ref[rows, i * DV:(i + 1) * DV] = o
        return carry

    lax.fori_loop(0, n_chunks, fwd, 0)
    lax.fori_loop(0, n_chunks, bwd, 0)


def _finish(n_sub, of_ref, ob_ref, gain_ref, gate_ref, y_ref, center, swish):
    rows_per = 256
    n_blocks = of_ref.shape[0] // rows_per

    def body(b, carry):
        rows = pl.ds(pl.multiple_of(b * rows_per, rows_per), rows_per)
        for i in range(n_sub):
            cols = slice(i * DV, (i + 1) * DV)
            o = of_ref[rows, cols] + ob_ref[rows, cols]
            if center:
                o = o - jnp.mean(o, axis=-1, keepdims=True)
            o = o * lax.rsqrt(jnp.mean(o * o, axis=-1, keepdims=True) + EPS)
            gz = gate_ref[rows, cols].astype(F32)
            act = gz * _sigmoid(gz) if swish else _sigmoid(gz)
            y_ref[rows, cols] = ((o * gain_ref[:, cols]) * act).astype(y_ref.dtype)
        return carry

    lax.fori_loop(0, n_blocks, body, 0)


def _mixer_a_kernel(q_ref, v_ref, og_ref, ff_ref, fb_ref, lb_ref, gain_ref, y_ref,
                    of_ref, ob_ref, st_ref):
    lb = lb_ref[...]
    one_m_lb = 1.0 - lb

    def prep(z_ref, forward, rows):
        z = z_ref[rows, :]
        f = lb + one_m_lb * _sigmoid(z)
        g = jnp.log(jnp.maximum(f, TINY))
        key = one_m_lb * _sigmoid(-z)
        aq = q_ref[rows, :].astype(F32)
        return aq * _sigmoid(aq), key, _gated_factors(g, forward)

    _scan_both(q_ref.shape[0] // CHUNK, 1, _level_masks,
               functools.partial(prep, ff_ref, True),
               functools.partial(prep, fb_ref, False),
               lambda rows: [v_ref[rows, :]],
               of_ref, ob_ref, st_ref)
    _finish(1, of_ref, ob_ref, gain_ref, og_ref, y_ref, center=False, swish=False)


def _rotate_half(x):
    lane = lax.broadcasted_iota(jnp.int32, x.shape, 1)
    half = BC_DK // 2
    first = (lane % BC_DK) < half
    return jnp.where(first, pltpu.roll(x, LANES - half, axis=1), pltpu.roll(x, half, axis=1))


def _mixer_b_kernel(q_ref, k_ref, v_ref, sg_ref, cos_ref, sin_ref, lgf_ref, lgb_ref, gain_ref, y_ref,
                    of_ref, ob_ref, st_ref):
    pack_f = _fixed_factors(lgf_ref[...], True)
    pack_b = _fixed_factors(lgb_ref[...], False)

    def prep(pack, rows):
        cos = cos_ref[rows, :]
        sin = sin_ref[rows, :]
        q = q_ref[rows, :].astype(F32)
        k = k_ref[rows, :].astype(F32)
        q = q * cos + _rotate_half(q) * sin
        k = (k * cos + _rotate_half(k) * sin) * (BC_DK ** -0.5)
        return q, k, pack

    _scan_both(q_ref.shape[0] // CHUNK, 2, _tri_masks,
               functools.partial(prep, pack_f), functools.partial(prep, pack_b),
               lambda rows: [v_ref[rows, 0:DV], v_ref[rows, DV:2 * DV]],
               of_ref, ob_ref, st_ref)
    _finish(2, of_ref, ob_ref, gain_ref, sg_ref, y_ref, center=True, swish=True)


def _mixer_c_kernel(q_ref, k_ref, v_ref, sg_ref, lr_ref, waf_ref, wab_ref, baf_ref, bab_ref, gain_ref, y_ref,
                    of_ref, ob_ref, st_ref):
    def prep(w_ref, b_ref, forward, rows):
        x = _dot(lr_ref[rows, :].astype(BF16), w_ref[...]) + b_ref[...]
        g = (jnp.minimum(x, 0.0) - jnp.log(1.0 + jnp.exp(-jnp.abs(x)))) * (1.0 / GLA_TAU)
        q = q_ref[rows, :].astype(F32) * (BC_DK ** -0.5)
        return q, k_ref[rows, :].astype(F32), _gated_factors(g, forward)

    _scan_both(q_ref.shape[0] // CHUNK, 2, _level_masks,
               functools.partial(prep, waf_ref, baf_ref, True),
               functools.partial(prep, wab_ref, bab_ref, False),
               lambda rows: [v_ref[rows, 0:DV], v_ref[rows, DV:2 * DV]],
               of_ref, ob_ref, st_ref)
    _finish(2, of_ref, ob_ref, gain_ref, sg_ref, y_ref, center=False, swish=True)


def _mixer_call(kernel, name, bsz, seqlen, n_groups, width, in_arrays, in_specs):
    n_sub = width // DV
    return pl.pallas_call(
        kernel,
        grid=(bsz, n_groups),
        in_specs=in_specs,
        out_specs=pl.BlockSpec((seqlen, width), lambda b, g: (b, g)),
        out_shape=jax.ShapeDtypeStruct((bsz * seqlen, n_groups * width), BF16),
        scratch_shapes=[
            pltpu.VMEM((seqlen, width), F32),
            pltpu.VMEM((seqlen, width), F32),
            pltpu.VMEM((2 * n_sub, DV, LANES), F32),
        ],
        compiler_params=pltpu.CompilerParams(
            dimension_semantics=("parallel", "parallel"), vmem_limit_bytes=VMEM_LIMIT),
        name=name,
    )(*in_arrays)


def _zspec(seqlen, width, col0):
    blk = col0 // width
    return pl.BlockSpec((seqlen, width), lambda b, g: (b, blk + g))


def _vspec(width, blk0=0):
    return pl.BlockSpec((1, width), lambda b, g: (0, blk0 + g))


def _mixer_a(zb, zf, lb, gain, bsz, seqlen):
    t = seqlen
    return _mixer_call(
        _mixer_a_kernel, "mixer_hgrn2", bsz, t, N_HEADS, DV,
        (zb, zb, zb, zf, zf, lb, gain),
        [_zspec(t, LANES, ZB_AQ), _zspec(t, LANES, ZB_AI), _zspec(t, LANES, ZB_AG),
         _zspec(t, LANES, ZF_FF), _zspec(t, LANES, ZF_FB), _vspec(LANES), _vspec(LANES)])


def _mixer_b(zb, cos, sin, lgf, lgb, gain, bsz, seqlen):
    t = seqlen
    table = pl.BlockSpec((t, LANES), lambda b, g: (0, 0))
    return _mixer_call(
        _mixer_b_kernel, "mixer_retention", bsz, t, N_HEADS // 2, 2 * DV,
        (zb, zb, zb, zb, cos, sin, lgf, lgb, gain),
        [_zspec(t, LANES, ZB_BQ), _zspec(t, LANES, ZB_BK), _zspec(t, 2 * DV, ZB_BV), _zspec(t, 2 * DV, ZB_BG),
         table, table, _vspec(LANES), _vspec(LANES), _vspec(2 * DV)])


def _mixer_c(zb, zf, wal, bal, gain, bsz, seqlen):
    t = seqlen
    half = N_HEADS * BC_DK // LANES
    return _mixer_call(
        _mixer_c_kernel, "mixer_gla", bsz, t, N_HEADS // 2, 2 * DV,
        (zb, zb, zb, zb, zf, wal, wal, bal, bal, gain),
        [_zspec(t, LANES, ZB_CQ), _zspec(t, LANES, ZB_CK), _zspec(t, 2 * DV, ZB_CV), _zspec(t, 2 * DV, ZB_CG),
         pl.BlockSpec((t, LANES), lambda b, g: (b, ZF_LR // LANES)),
         pl.BlockSpec((LANES, LANES), lambda b, g: (0, g)),
         pl.BlockSpec((LANES, LANES), lambda b, g: (0, half + g)),
         _vspec(LANES), _vspec(LANES, half), _vspec(2 * DV)])


def _split_w_in(w):
    o = {}
    off = 0
    for name, size in (("a_q", 512), ("a_ff", 512), ("a_fb", 512), ("a_i", 512), ("a_g", 512),
                       ("b_q", 256), ("b_k", 256), ("b_v", 512), ("b_g", 512),
                       ("c_q", 256), ("c_k", 256), ("c_v", 512), ("c_g", 512), ("c_lr", 32),
                       ("gate_a", 1024), ("gate_b", 1024), ("gate_c", 1024)):
        o[name] = w[:, off:off + size]
        off += size
    wb = jnp.concatenate([o[n] for n in ("gate_a", "gate_b", "gate_c", "a_q", "a_i", "a_g",
                                         "b_q", "b_k", "b_v", "b_g", "c_q", "c_k", "c_v", "c_g")], axis=1)
    pad = jnp.zeros((w.shape[0], ZF_COLS - ZF_LR - 2 * GLA_RANK), w.dtype)
    wf = jnp.concatenate([o["a_ff"], o["a_fb"], o["c_lr"], pad], axis=1)
    return wb.astype(BF16), wf.astype(BF16)


def _rotary_tables(seqlen):
    pos = jnp.arange(seqlen, dtype=F32)
    inv_freq = 10000.0 ** (-jnp.arange(0, BC_DK, 2, dtype=F32) / BC_DK)
    ang = pos[:, None] * inv_freq[None, :]
    cos, sin = jnp.cos(ang), jnp.sin(ang)
    reps = LANES // BC_DK
    return (jnp.tile(jnp.concatenate([cos, cos], axis=1), (1, reps)),
            jnp.tile(jnp.concatenate([-sin, sin], axis=1), (1, reps)))


def _retention_log_decays():
    h = jnp.arange(N_HEADS, dtype=F32)
    fwd = jnp.log1p(-jnp.exp2(-5.0 - h))
    bwd = jnp.log1p(-jnp.exp2(-5.0 - h[::-1]))
    spread = lambda v: jnp.repeat(v, BC_DK)[None, :]
    return spread(fwd), spread(bwd)


def kernel(x, c, norm1_g, w_ada, b_ada, w_in, lb_logits, norm_a_g, norm_b_g, norm_c_g, w_alpha, b_alpha,
           w_pa, w_pb, w_pc, w_out, norm2_g, w_ffn_in, w_ffn_out, norm_f_g):
    bsz, seqlen, d = x.shape
    depth = w_in.shape[0]
    assert d == D_MODEL and seqlen % 256 == 0
    tm = min(512, seqlen)

    mod = _modulation(c, w_ada, b_ada)
    lbs = _lower_bounds(lb_logits)
    cos, sin = _rotary_tables(seqlen)
    lgf, lgb = _retention_log_decays()

    x2 = x.reshape(bsz * seqlen, d)
    for l in range(depth):
        sh1, sc1, g1, sh2, sc2, g2 = [mod[l, :, i * d:(i + 1) * d].reshape(bsz, 1, d) for i in range(6)]
        wb, wf = _split_w_in(w_in[l])
        gain1 = norm1_g[l].reshape(1, d)
        zb = _inproj(x2, sh1, sc1, gain1, wb, BF16, seqlen, tm, 1280, "inproj_main")
        zf = _inproj(x2, sh1, sc1, gain1, wf, F32, seqlen, tm, ZF_COLS, "inproj_gates")

        wal = jnp.zeros((LANES, 2 * N_HEADS * BC_DK), F32)
        wal = wal.at[0:GLA_RANK, 0:N_HEADS * BC_DK].set(w_alpha[l, 0])
        wal = wal.at[GLA_RANK:2 * GLA_RANK, N_HEADS * BC_DK:].set(w_alpha[l, 1])
        bal = b_alpha[l].reshape(1, 2 * N_HEADS * BC_DK)

        ya = _mixer_a(zb, zf, lbs[l].reshape(1, -1), norm_a_g[l].reshape(1, -1), bsz, seqlen)
        yb = _mixer_b(zb, cos, sin, lgf, lgb, norm_b_g[l].reshape(1, -1), bsz, seqlen)
        yc = _mixer_c(zb, zf, wal.astype(BF16), bal, norm_c_g[l].reshape(1, -1), bsz, seqlen)

        x2 = _merge(x2, ya, yb, yc, zb, w_pa[l].astype(BF16), w_pb[l].astype(BF16), w_pc[l].astype(BF16),
                    w_out[l].astype(BF16), g1, seqlen, tm)
        x2 = _ffn(x2, sh2, sc2, g2, norm2_g[l].reshape(1, d), w_ffn_in[l].astype(BF16),
                  w_ffn_out[l].astype(BF16), norm_f_g.reshape(1, d), seqlen, tm, D_FF // 2,
                  final_norm=(l == depth - 1))
    return x2.reshape(bsz, seqlen, d)
```

```python
import functools

import jax
import jax.numpy as jnp
from jax import lax
from jax.experimental import pallas as pl
from jax.experimental.pallas import tpu as pltpu

F32 = jnp.float32
BF16 = jnp.bfloat16

D_MODEL = 1024
N_HEADS = 4
A_DK = 128
BC_DK = 64
DV = 128
GLA_RANK = 16
GLA_TAU = 16.0
D_FF = 2816
EPS = 1e-6
TINY = 1e-30
LOG2_E = 1.4426950408889634

LANES = 128
CHUNK = 64
VMEM_LIMIT = 48 * 1024 * 1024

ZB_GATE = 0
ZB_AQ, ZB_AI, ZB_AG = 3072, 3584, 4096
ZB_BQ, ZB_BK, ZB_BV, ZB_BG = 4608, 4864, 5120, 5632
ZB_CQ, ZB_CK, ZB_CV, ZB_CG = 6144, 6400, 6656, 7168
ZB_COLS = 7680
ZF_FF, ZF_FB, ZF_LR = 0, 512, 1024
ZF_COLS = 1152


def _dot(a, b):
    return jnp.dot(a, b, preferred_element_type=F32)


def _dot_nt(a, b):
    return lax.dot_general(a, b, (((1,), (1,)), ((), ())), preferred_element_type=F32)


def _dot_tn(a, b):
    return lax.dot_general(a, b, (((0,), (0,)), ((), ())), preferred_element_type=F32)


def _sigmoid(x):
    return jax.nn.sigmoid(x)


def _mod_kernel(c_ref, w_ref, b_ref, o_ref):
    c = c_ref[...]
    a = c * _sigmoid(c)
    w = w_ref[...]
    a_hi = a.astype(BF16)
    a_lo = (a - a_hi.astype(F32)).astype(BF16)
    w_hi = w.astype(BF16)
    w_lo = (w - w_hi.astype(F32)).astype(BF16)
    o_ref[...] = _dot(a_hi, w_hi) + _dot(a_hi, w_lo) + _dot(a_lo, w_hi) + b_ref[...]


def _modulation(c, w_ada, b_ada):
    depth, d, n6 = w_ada.shape
    bsz = c.shape[0]
    return pl.pallas_call(
        _mod_kernel,
        grid=(depth, n6 // d),
        in_specs=[
            pl.BlockSpec((bsz, d), lambda l, j: (0, 0)),
            pl.BlockSpec((None, d, d), lambda l, j: (l, 0, j)),
            pl.BlockSpec((None, 1, d), lambda l, j: (l, 0, j)),
        ],
        out_specs=pl.BlockSpec((None, bsz, d), lambda l, j: (l, 0, j)),
        out_shape=jax.ShapeDtypeStruct((depth, bsz, n6), F32),
        name="adaln_modulation",
    )(c, w_ada, b_ada.reshape(depth, 1, n6))


def _lb_kernel(x_ref, o_ref):
    depth = x_ref.shape[0]
    rows = [x_ref[i:i + 1, :] for i in range(depth)]
    m = rows[0]
    for r in rows[1:]:
        m = jnp.maximum(m, r)
    e = [jnp.exp(r - m) for r in rows]
    s = e[0]
    for t in e[1:]:
        s = s + t
    p = [t / s for t in e]
    acc = p[0]
    o_ref[0:1, :] = jnp.maximum(acc - p[0], 0.0)
    for i in range(1, depth):
        acc = acc + p[i]
        o_ref[i:i + 1, :] = jnp.maximum(acc - p[0], 0.0)


def _lower_bounds(lb_logits):
    return pl.pallas_call(
        _lb_kernel,
        out_shape=jax.ShapeDtypeStruct(lb_logits.shape, F32),
        name="hgrn2_lower_bounds",
    )(lb_logits)


def _norm_mod(x, gain, sc, sh):
    y = x * lax.rsqrt(jnp.mean(x * x, axis=-1, keepdims=True) + EPS)
    return (y * gain) * (1.0 + sc) + sh


def _inproj_kernel(x_ref, sh_ref, sc_ref, g_ref, w_ref, o_ref, h_ref):
    @pl.when(pl.program_id(1) == 0)
    def _():
        h_ref[...] = _norm_mod(x_ref[...], g_ref[...], sc_ref[0], sh_ref[0]).astype(BF16)

    o_ref[...] = _dot(h_ref[...], w_ref[...]).astype(o_ref.dtype)


def _inproj(x2, sh, sc, gain, w, out_dtype, seqlen, tm, tn, name):
    m, d = x2.shape
    n = w.shape[1]
    per = seqlen // tm
    return pl.pallas_call(
        _inproj_kernel,
        grid=(m // tm, n // tn),
        in_specs=[
            pl.BlockSpec((tm, d), lambda i, j: (i, 0)),
            pl.BlockSpec((1, 1, d), lambda i, j: (i // per, 0, 0)),
            pl.BlockSpec((1, 1, d), lambda i, j: (i // per, 0, 0)),
            pl.BlockSpec((1, d), lambda i, j: (0, 0)),
            pl.BlockSpec((d, tn), lambda i, j: (0, j)),
        ],
        out_specs=pl.BlockSpec((tm, tn), lambda i, j: (i, j)),
        out_shape=jax.ShapeDtypeStruct((m, n), out_dtype),
        scratch_shapes=[pltpu.VMEM((tm, d), BF16)],
        compiler_params=pltpu.CompilerParams(
            dimension_semantics=("parallel", "arbitrary"), vmem_limit_bytes=VMEM_LIMIT),
        name=name,
    )(x2, sh, sc, gain, w)


def _merge_kernel(x_ref, ya_ref, yb_ref, yc_ref, ga_ref, gb_ref, gc_ref,
                  wpa_ref, wpb_ref, wpc_ref, wout_ref, g1_ref, o_ref):
    merged = (_sigmoid(ga_ref[...].astype(F32)) * _dot(ya_ref[...], wpa_ref[...])
              + _sigmoid(gb_ref[...].astype(F32)) * _dot(yb_ref[...], wpb_ref[...])
              + _sigmoid(gc_ref[...].astype(F32)) * _dot(yc_ref[...], wpc_ref[...]))
    out = _dot(merged.astype(BF16), wout_ref[...])
    o_ref[...] = x_ref[...] + g1_ref[0] * out


def _merge(x2, ya, yb, yc, zb, wpa, wpb, wpc, wout, g1, seqlen, tm):
    m, d = x2.shape
    w = ya.shape[1]
    per = seqlen // tm
    row = lambda i: (i, 0)
    const = lambda i: (0, 0)
    return pl.pallas_call(
        _merge_kernel,
        grid=(m // tm,),
        in_specs=[
            pl.BlockSpec((tm, d), row),
            pl.BlockSpec((tm, w), row), pl.BlockSpec((tm, w), row), pl.BlockSpec((tm, w), row),
            pl.BlockSpec((tm, d), lambda i: (i, ZB_GATE // d)),
            pl.BlockSpec((tm, d), lambda i: (i, ZB_GATE // d + 1)),
            pl.BlockSpec((tm, d), lambda i: (i, ZB_GATE // d + 2)),
            pl.BlockSpec((w, d), const), pl.BlockSpec((w, d), const), pl.BlockSpec((w, d), const),
            pl.BlockSpec((d, d), const),
            pl.BlockSpec((1, 1, d), lambda i: (i // per, 0, 0)),
        ],
        out_specs=pl.BlockSpec((tm, d), row),
        out_shape=jax.ShapeDtypeStruct((m, d), F32),
        compiler_params=pltpu.CompilerParams(
            dimension_semantics=("parallel",), vmem_limit_bytes=VMEM_LIMIT),
        name="merge_outproj",
    )(x2, ya, yb, yc, zb, zb, zb, wpa, wpb, wpc, wout, g1)


def _ffn_kernel(x_ref, sh_ref, sc_ref, g2_ref, ng_ref, wg_ref, wu_ref, wo_ref, nf_ref,
                o_ref, h_ref, acc_ref, *, final_norm):
    k = pl.program_id(1)

    @pl.when(k == 0)
    def _():
        h_ref[...] = _norm_mod(x_ref[...], ng_ref[...], sc_ref[0], sh_ref[0]).astype(BF16)
        acc_ref[...] = jnp.zeros_like(acc_ref)

    h = h_ref[...]
    gate = _dot(h, wg_ref[...])
    up = _dot(h, wu_ref[...])
    act = (gate * _sigmoid(gate) * up).astype(BF16)
    acc_ref[...] += _dot(act, wo_ref[...])

    @pl.when(k == pl.num_programs(1) - 1)
    def _():
        xn = x_ref[...] + g2_ref[0] * acc_ref[...]
        if final_norm:
            xn = xn * lax.rsqrt(jnp.mean(xn * xn, axis=-1, keepdims=True) + EPS) * nf_ref[...]
        o_ref[...] = xn


def _ffn(x2, sh, sc, g2, ng, w_in, w_out, nf, seqlen, tm, tf, final_norm):
    m, d = x2.shape
    dff = w_out.shape[0]
    nk = dff // tf
    per = seqlen // tm
    bvec = lambda i, k: (i // per, 0, 0)
    return pl.pallas_call(
        functools.partial(_ffn_kernel, final_norm=final_norm),
        grid=(m // tm, nk),
        in_specs=[
            pl.BlockSpec((tm, d), lambda i, k: (i, 0)),
            pl.BlockSpec((1, 1, d), bvec), pl.BlockSpec((1, 1, d), bvec), pl.BlockSpec((1, 1, d), bvec),
            pl.BlockSpec((1, d), lambda i, k: (0, 0)),
            pl.BlockSpec((d, tf), lambda i, k: (0, k)),
            pl.BlockSpec((d, tf), lambda i, k: (0, nk + k)),
            pl.BlockSpec((tf, d), lambda i, k: (k, 0)),
            pl.BlockSpec((1, d), lambda i, k: (0, 0)),
        ],
        out_specs=pl.BlockSpec((tm, d), lambda i, k: (i, 0)),
        out_shape=jax.ShapeDtypeStruct((m, d), F32),
        scratch_shapes=[pltpu.VMEM((tm, d), BF16), pltpu.VMEM((tm, d), F32)],
        compiler_params=pltpu.CompilerParams(
            dimension_semantics=("parallel", "arbitrary"), vmem_limit_bytes=VMEM_LIMIT),
        name="swiglu_ffn",
    )(x2, sh, sc, g2, ng, w_in, w_in, w_out, nf)


N_LEVELS = CHUNK.bit_length() - 1


def _block_sums(g):
    row = lax.broadcasted_iota(jnp.int32, g.shape, 0)
    lb, tb = g, g
    out = [(lb, tb)]
    b = 1
    while b < CHUNK:
        right = (row & b) != 0
        from_left = pltpu.roll(tb, b, axis=0)
        from_right = pltpu.roll(tb, CHUNK - b, axis=0)
        lb = lb + jnp.where(right, from_left, 0.0)
        tb = tb + jnp.where(right, from_left, from_right)
        out.append((lb, tb))
        b *= 2
    return out


def _gated_factors(g2, forward):
    sums = _block_sums(g2)
    facs = [(jnp.exp2(g2), None)]
    for lb, tb in sums[1:]:
        if forward:
            facs.append((jnp.exp2(lb), jnp.exp2(tb - lb)))
        else:
            facs.append((jnp.exp2((tb - lb) + g2), jnp.exp2(lb - g2)))
    return facs, jnp.exp2(sums[-1][1][0:1, :])


def _fixed_factors(log_decay, forward):
    pos = lax.broadcasted_iota(jnp.int32, (CHUNK, LANES), 0).astype(F32)
    mid = CHUNK // 2
    if forward:
        facs = [(jnp.exp((pos - (mid - 1.0)) * log_decay), jnp.exp(((mid - 1.0) - pos) * log_decay)),
                (jnp.exp((pos + 1.0) * log_decay), jnp.exp((CHUNK - 1.0 - pos) * log_decay))]
    else:
        facs = [(jnp.exp((mid - pos) * log_decay), jnp.exp((pos - mid) * log_decay)),
                (jnp.exp((CHUNK - pos) * log_decay), jnp.exp(pos * log_decay))]
    return facs, jnp.exp(float(CHUNK) * log_decay)


def _store_masks(mask_ref, hierarchical):
    row = lax.broadcasted_iota(jnp.int32, (CHUNK, CHUNK), 0)
    col = lax.broadcasted_iota(jnp.int32, (CHUNK, CHUNK), 1)
    one = lambda m: jnp.where(m, 1.0, 0.0)
    mask_ref[0] = one(row == col)
    n = N_LEVELS if hierarchical else 1
    for level in range(n):
        same_level = ((row ^ col) >> level) == 1 if hierarchical else (row != col)
        mask_ref[1 + level] = one(same_level & (row > col))
        mask_ref[1 + n + level] = one(same_level & (col > row))
    return n


def _lane_masks(n_sub):
    if n_sub == 1:
        return [None]
    lane = lax.broadcasted_iota(jnp.int32, (1, LANES), 1)
    width = LANES // n_sub
    return [jnp.where((lane >= i * width) & (lane < (i + 1) * width), 1.0, 0.0) for i in range(n_sub)]


def _scan_step(q, k, factors, forward, mask_ref, n_levels, lane_masks, v_list, st_ref, slot0):
    facs, chunk_decay = factors
    kts = [(k if fk is None else k * fk).astype(BF16) for _, fk in facs]
    mask0 = 1 if forward else 1 + n_levels
    outs = []
    for i, v in enumerate(v_list):
        qm = q if lane_masks[i] is None else q * lane_masks[i]
        p = None
        if forward:
            p = mask_ref[0] * jnp.sum(qm * k, axis=-1, keepdims=True)
        for level in range(n_levels):
            s = _dot_nt((qm * facs[level][0]).astype(BF16), kts[level]) * mask_ref[mask0 + level]
            p = s if p is None else p + s
        st = st_ref[slot0 + i]
        inter = _dot_nt((qm * facs[-1][0]).astype(BF16), st.astype(BF16))
        outs.append(_dot(p.astype(BF16), v) + inter)
        st_ref[slot0 + i] = st * chunk_decay + _dot_tn(v, kts[-1])
    return outs


def _chunk_rows(n):
    return pl.ds(pl.multiple_of(n * CHUNK, CHUNK), CHUNK)


def _scan_both(n_chunks, n_groups, n_sub, hierarchical, prep, load_v, of_ref, ob_ref, st_ref, mask_ref):
    st_ref[...] = jnp.zeros_like(st_ref)
    n_levels = _store_masks(mask_ref, hierarchical)
    lane_masks = _lane_masks(n_sub)

    def body(n, carry):
        for gi in range(n_groups):
            for forward, o_ref in ((True, of_ref), (False, ob_ref)):
                rows = _chunk_rows(n if forward else n_chunks - 1 - n)
                q, k, factors = prep(gi, forward, rows)
                slot0 = (gi * 2 + (0 if forward else 1)) * n_sub
                outs = _scan_step(q, k, factors, forward, mask_ref, n_levels, lane_masks,
                                  load_v(gi, rows), st_ref, slot0)
                for i, o in enumerate(outs):
                    c0 = (gi * n_sub + i) * DV
                    o_ref[rows, c0:c0 + DV] = o
        return carry

    lax.fori_loop(0, n_chunks, body, 0)


def _finish(n_heads, of_ref, ob_ref, gain_ref, gate_ref, y_ref, center, swish):
    rows_per = 256
    n_blocks = of_ref.shape[0] // rows_per

    def body(b, carry):
        rows = pl.ds(pl.multiple_of(b * rows_per, rows_per), rows_per)
        for i in range(n_heads):
            cols = slice(i * DV, (i + 1) * DV)
            o = of_ref[rows, cols] + ob_ref[rows, cols]
            if center:
                o = o - jnp.mean(o, axis=-1, keepdims=True)
            o = o * lax.rsqrt(jnp.mean(o * o, axis=-1, keepdims=True) + EPS)
            gz = gate_ref[rows, cols].astype(F32)
            act = gz * _sigmoid(gz) if swish else _sigmoid(gz)
            y_ref[rows, cols] = ((o * gain_ref[:, cols]) * act).astype(y_ref.dtype)
        return carry

    lax.fori_loop(0, n_blocks, body, 0)


def _group_cols(gi):
    return slice(gi * LANES, (gi + 1) * LANES)


def _mixer_a_kernel(q_ref, v_ref, og_ref, ff_ref, fb_ref, lb_ref, gain_ref, y_ref,
                    of_ref, ob_ref, st_ref, mask_ref):
    n_groups = q_ref.shape[1] // LANES

    def prep(gi, forward, rows):
        cols = _group_cols(gi)
        lb = lb_ref[:, cols]
        z = (ff_ref if forward else fb_ref)[rows, cols]
        f = lb + (1.0 - lb) * _sigmoid(z)
        g2 = jnp.log2(jnp.maximum(f, TINY))
        key = 1.0 - f
        aq = q_ref[rows, cols].astype(F32)
        return aq * _sigmoid(aq), key, _gated_factors(g2, forward)

    _scan_both(q_ref.shape[0] // CHUNK, n_groups, 1, True, prep,
               lambda gi, rows: [v_ref[rows, _group_cols(gi)]],
               of_ref, ob_ref, st_ref, mask_ref)
    _finish(n_groups, of_ref, ob_ref, gain_ref, og_ref, y_ref, center=False, swish=False)


def _rotate_half(x):
    lane = lax.broadcasted_iota(jnp.int32, x.shape, 1)
    half = BC_DK // 2
    first = (lane % BC_DK) < half
    return jnp.where(first, pltpu.roll(x, LANES - half, axis=1), pltpu.roll(x, half, axis=1))


def _sub_values(v_ref, gi, rows):
    return [v_ref[rows, (2 * gi + i) * DV:(2 * gi + i + 1) * DV] for i in range(2)]


def _mixer_b_kernel(q_ref, k_ref, v_ref, sg_ref, cos_ref, sin_ref, lgf_ref, lgb_ref, gain_ref, y_ref,
                    of_ref, ob_ref, st_ref, mask_ref):
    n_groups = q_ref.shape[1] // LANES
    packs = [(_fixed_factors(lgb_ref[:, _group_cols(gi)], False), _fixed_factors(lgf_ref[:, _group_cols(gi)], True))
             for gi in range(n_groups)]

    def prep(gi, forward, rows):
        cols = _group_cols(gi)
        cos = cos_ref[rows, :]
        sin = sin_ref[rows, :]
        q = q_ref[rows, cols].astype(F32)
        k = k_ref[rows, cols].astype(F32)
        q = q * cos + _rotate_half(q) * sin
        k = (k * cos + _rotate_half(k) * sin) * (BC_DK ** -0.5)
        return q, k, packs[gi][1 if forward else 0]

    _scan_both(q_ref.shape[0] // CHUNK, n_groups, 2, False, prep,
               functools.partial(_sub_values, v_ref), of_ref, ob_ref, st_ref, mask_ref)
    _finish(2 * n_groups, of_ref, ob_ref, gain_ref, sg_ref, y_ref, center=True, swish=True)


def _mixer_c_kernel(q_ref, k_ref, v_ref, sg_ref, lr_ref, waf_ref, wab_ref, baf_ref, bab_ref, gain_ref, y_ref,
                    of_ref, ob_ref, st_ref, mask_ref):
    n_groups = q_ref.shape[1] // LANES

    def prep(gi, forward, rows):
        cols = _group_cols(gi)
        w_ref, b_ref = (waf_ref, baf_ref) if forward else (wab_ref, bab_ref)
        x = _dot(lr_ref[rows, :].astype(BF16), w_ref[:, cols]) + b_ref[:, cols]
        log_sig = jnp.minimum(x, 0.0) - jnp.log(1.0 + jnp.exp(-jnp.abs(x)))
        g2 = log_sig * (LOG2_E / GLA_TAU)
        q = q_ref[rows, cols].astype(F32) * (BC_DK ** -0.5)
        return q, k_ref[rows, cols].astype(F32), _gated_factors(g2, forward)

    _scan_both(q_ref.shape[0] // CHUNK, n_groups, 2, True, prep,
               functools.partial(_sub_values, v_ref), of_ref, ob_ref, st_ref, mask_ref)
    _finish(2 * n_groups, of_ref, ob_ref, gain_ref, sg_ref, y_ref, center=False, swish=True)


def _mixer_call(kernel, name, bsz, seqlen, grid_groups, qk_width, out_width, hierarchical, in_arrays, in_specs):
    n_states = 2 * (out_width // DV)
    n_masks = 1 + 2 * (N_LEVELS if hierarchical else 1)
    return pl.pallas_call(
        kernel,
        grid=(bsz, grid_groups),
        in_specs=in_specs,
        out_specs=pl.BlockSpec((seqlen, out_width), lambda b, g: (b, g)),
        out_shape=jax.ShapeDtypeStruct((bsz * seqlen, grid_groups * out_width), BF16),
        scratch_shapes=[
            pltpu.VMEM((seqlen, out_width), F32),
            pltpu.VMEM((seqlen, out_width), F32),
            pltpu.VMEM((n_states, DV, LANES), F32),
            pltpu.VMEM((n_masks, CHUNK, CHUNK), F32),
        ],
        compiler_params=pltpu.CompilerParams(
            dimension_semantics=("parallel", "parallel"), vmem_limit_bytes=VMEM_LIMIT),
        name=name,
    )(*in_arrays)


def _zspec(seqlen, width, col0):
    blk = col0 // width
    return pl.BlockSpec((seqlen, width), lambda b, g: (b, blk + g))


def _vspec(width, blk0=0):
    return pl.BlockSpec((1, width), lambda b, g: (0, blk0 + g))


def _mixer_a(zb, zf, lb, gain, bsz, seqlen):
    t, w = seqlen, 2 * LANES
    return _mixer_call(
        _mixer_a_kernel, "mixer_hgrn2", bsz, t, N_HEADS * A_DK // w, w, w, True,
        (zb, zb, zb, zf, zf, lb, gain),
        [_zspec(t, w, ZB_AQ), _zspec(t, w, ZB_AI), _zspec(t, w, ZB_AG),
         _zspec(t, w, ZF_FF), _zspec(t, w, ZF_FB), _vspec(w), _vspec(w)])


def _mixer_b(zb, cos, sin, lgf, lgb, gain, bsz, seqlen):
    t, w, wv = seqlen, N_HEADS * BC_DK, N_HEADS * DV
    table = pl.BlockSpec((t, LANES), lambda b, g: (0, 0))
    return _mixer_call(
        _mixer_b_kernel, "mixer_retention", bsz, t, 1, w, wv, False,
        (zb, zb, zb, zb, cos, sin, lgf, lgb, gain),
        [_zspec(t, w, ZB_BQ), _zspec(t, w, ZB_BK), _zspec(t, wv, ZB_BV), _zspec(t, wv, ZB_BG),
         table, table, _vspec(w), _vspec(w), _vspec(wv)])


def _mixer_c(zb, zf, wal, bal, gain, bsz, seqlen):
    t, w, wv = seqlen, N_HEADS * BC_DK, N_HEADS * DV
    return _mixer_call(
        _mixer_c_kernel, "mixer_gla", bsz, t, 1, w, wv, True,
        (zb, zb, zb, zb, zf, wal, wal, bal, bal, gain),
        [_zspec(t, w, ZB_CQ), _zspec(t, w, ZB_CK), _zspec(t, wv, ZB_CV), _zspec(t, wv, ZB_CG),
         pl.BlockSpec((t, LANES), lambda b, g: (b, ZF_LR // LANES)),
         pl.BlockSpec((LANES, w), lambda b, g: (0, 0)),
         pl.BlockSpec((LANES, w), lambda b, g: (0, 1)),
         _vspec(w), _vspec(w, 1), _vspec(wv)])


def _split_w_in(w):
    o = {}
    off = 0
    for name, size in (("a_q", 512), ("a_ff", 512), ("a_fb", 512), ("a_i", 512), ("a_g", 512),
                       ("b_q", 256), ("b_k", 256), ("b_v", 512), ("b_g", 512),
                       ("c_q", 256), ("c_k", 256), ("c_v", 512), ("c_g", 512), ("c_lr", 32),
                       ("gate_a", 1024), ("gate_b", 1024), ("gate_c", 1024)):
        o[name] = w[:, off:off + size]
        off += size
    wb = jnp.concatenate([o[n] for n in ("gate_a", "gate_b", "gate_c", "a_q", "a_i", "a_g",
                                         "b_q", "b_k", "b_v", "b_g", "c_q", "c_k", "c_v", "c_g")], axis=1)
    pad = jnp.zeros((w.shape[0], ZF_COLS - ZF_LR - 2 * GLA_RANK), w.dtype)
    wf = jnp.concatenate([o["a_ff"], o["a_fb"], o["c_lr"], pad], axis=1)
    return wb.astype(BF16), wf.astype(BF16)


def _rotary_tables(seqlen):
    pos = jnp.arange(seqlen, dtype=F32)
    inv_freq = 10000.0 ** (-jnp.arange(0, BC_DK, 2, dtype=F32) / BC_DK)
    ang = pos[:, None] * inv_freq[None, :]
    cos, sin = jnp.cos(ang), jnp.sin(ang)
    reps = LANES // BC_DK
    return (jnp.tile(jnp.concatenate([cos, cos], axis=1), (1, reps)),
            jnp.tile(jnp.concatenate([-sin, sin], axis=1), (1, reps)))


def _retention_log_decays():
    h = jnp.arange(N_HEADS, dtype=F32)
    fwd = jnp.log1p(-jnp.exp2(-5.0 - h))
    bwd = jnp.log1p(-jnp.exp2(-5.0 - h[::-1]))
    spread = lambda v: jnp.repeat(v, BC_DK)[None, :]
    return spread(fwd), spread(bwd)


def kernel(x, c, norm1_g, w_ada, b_ada, w_in, lb_logits, norm_a_g, norm_b_g, norm_c_g, w_alpha, b_alpha,
           w_pa, w_pb, w_pc, w_out, norm2_g, w_ffn_in, w_ffn_out, norm_f_g):
    bsz, seqlen, d = x.shape
    depth = w_in.shape[0]
    assert d == D_MODEL and seqlen % 256 == 0
    tm = min(512, seqlen)
    tm_in = min(1024, seqlen)

    mod = _modulation(c, w_ada, b_ada)
    lbs = _lower_bounds(lb_logits)
    cos, sin = _rotary_tables(seqlen)
    lgf, lgb = _retention_log_decays()

    x2 = x.reshape(bsz * seqlen, d)
    for l in range(depth):
        sh1, sc1, g1, sh2, sc2, g2 = [mod[l, :, i * d:(i + 1) * d].reshape(bsz, 1, d) for i in range(6)]
        wb, wf = _split_w_in(w_in[l])
        gain1 = norm1_g[l].reshape(1, d)
        zb = _inproj(x2, sh1, sc1, gain1, wb, BF16, seqlen, tm_in, 1280, "inproj_main")
        zf = _inproj(x2, sh1, sc1, gain1, wf, F32, seqlen, tm_in, ZF_COLS, "inproj_gates")

        wal = jnp.zeros((LANES, 2 * N_HEADS * BC_DK), F32)
        wal = wal.at[0:GLA_RANK, 0:N_HEADS * BC_DK].set(w_alpha[l, 0])
        wal = wal.at[GLA_RANK:2 * GLA_RANK, N_HEADS * BC_DK:].set(w_alpha[l, 1])
        bal = b_alpha[l].reshape(1, 2 * N_HEADS * BC_DK)

        ya = _mixer_a(zb, zf, lbs[l].reshape(1, -1), norm_a_g[l].reshape(1, -1), bsz, seqlen)
        yb = _mixer_b(zb, cos, sin, lgf, lgb, norm_b_g[l].reshape(1, -1), bsz, seqlen)
        yc = _mixer_c(zb, zf, wal.astype(BF16), bal, norm_c_g[l].reshape(1, -1), bsz, seqlen)

        x2 = _merge(x2, ya, yb, yc, zb, w_pa[l].astype(BF16), w_pb[l].astype(BF16), w_pc[l].astype(BF16),
                    w_out[l].astype(BF16), g1, seqlen, tm)
        x2 = _ffn(x2, sh2, sc2, g2, norm2_g[l].reshape(1, d), w_ffn_in[l].astype(BF16),
                  w_ffn_out[l].astype(BF16), norm_f_g.reshape(1, d), seqlen, tm, D_FF // 2,
                  final_norm=(l == depth - 1))
    return x2.reshape(bsz, seqlen, d)
```

```python
import functools

import jax
import jax.numpy as jnp
from jax import lax
from jax.experimental import pallas as pl
from jax.experimental.pallas import tpu as pltpu

F32 = jnp.float32
BF16 = jnp.bfloat16

D_MODEL = 1024
N_HEADS = 4
A_DK = 128
BC_DK = 64
DV = 128
GLA_RANK = 16
GLA_TAU = 16.0
D_FF = 2816
EPS = 1e-6
TINY = 1e-30
LOG2_E = 1.4426950408889634

LANES = 128
CHUNK = 128
VMEM_LIMIT = 48 * 1024 * 1024

ZB_GATE = 0
ZB_AQ, ZB_AI, ZB_AG = 3072, 3584, 4096
ZB_BQ, ZB_BK, ZB_BV, ZB_BG = 4608, 4864, 5120, 5632
ZB_CQ, ZB_CK, ZB_CV, ZB_CG = 6144, 6400, 6656, 7168
ZB_COLS = 7680
ZF_FF, ZF_FB, ZF_LR = 0, 512, 1024
ZF_COLS = 1152


def _dot(a, b):
    return jnp.dot(a, b, preferred_element_type=F32)


def _dot_nt(a, b):
    return lax.dot_general(a, b, (((1,), (1,)), ((), ())), preferred_element_type=F32)


def _dot_tn(a, b):
    return lax.dot_general(a, b, (((0,), (0,)), ((), ())), preferred_element_type=F32)


def _sigmoid(x):
    return jax.nn.sigmoid(x)


def _mod_kernel(c_ref, w_ref, b_ref, o_ref):
    c = c_ref[...]
    a = c * _sigmoid(c)
    w = w_ref[...]
    a_hi = a.astype(BF16)
    a_lo = (a - a_hi.astype(F32)).astype(BF16)
    w_hi = w.astype(BF16)
    w_lo = (w - w_hi.astype(F32)).astype(BF16)
    o_ref[...] = _dot(a_hi, w_hi) + _dot(a_hi, w_lo) + _dot(a_lo, w_hi) + b_ref[...]


def _modulation(c, w_ada, b_ada):
    depth, d, n6 = w_ada.shape
    bsz = c.shape[0]
    return pl.pallas_call(
        _mod_kernel,
        grid=(depth, n6 // d),
        in_specs=[
            pl.BlockSpec((bsz, d), lambda l, j: (0, 0)),
            pl.BlockSpec((None, d, d), lambda l, j: (l, 0, j)),
            pl.BlockSpec((None, 1, d), lambda l, j: (l, 0, j)),
        ],
        out_specs=pl.BlockSpec((None, bsz, d), lambda l, j: (l, 0, j)),
        out_shape=jax.ShapeDtypeStruct((depth, bsz, n6), F32),
        name="adaln_modulation",
    )(c, w_ada, b_ada.reshape(depth, 1, n6))


def _lb_kernel(x_ref, o_ref):
    depth = x_ref.shape[0]
    rows = [x_ref[i:i + 1, :] for i in range(depth)]
    m = rows[0]
    for r in rows[1:]:
        m = jnp.maximum(m, r)
    e = [jnp.exp(r - m) for r in rows]
    s = e[0]
    for t in e[1:]:
        s = s + t
    p = [t / s for t in e]
    acc = p[0]
    o_ref[0:1, :] = jnp.maximum(acc - p[0], 0.0)
    for i in range(1, depth):
        acc = acc + p[i]
        o_ref[i:i + 1, :] = jnp.maximum(acc - p[0], 0.0)


def _lower_bounds(lb_logits):
    return pl.pallas_call(
        _lb_kernel,
        out_shape=jax.ShapeDtypeStruct(lb_logits.shape, F32),
        name="hgrn2_lower_bounds",
    )(lb_logits)


def _norm_mod(x, gain, sc, sh):
    y = x * lax.rsqrt(jnp.mean(x * x, axis=-1, keepdims=True) + EPS)
    return (y * gain) * (1.0 + sc) + sh


def _inproj_kernel(x_ref, sh_ref, sc_ref, g_ref, w_ref, o_ref, h_ref):
    @pl.when(pl.program_id(1) == 0)
    def _():
        h_ref[...] = _norm_mod(x_ref[...], g_ref[...], sc_ref[0], sh_ref[0]).astype(BF16)

    o_ref[...] = _dot(h_ref[...], w_ref[...]).astype(o_ref.dtype)


def _inproj(x2, sh, sc, gain, w, out_dtype, seqlen, tm, tn, name):
    m, d = x2.shape
    n = w.shape[1]
    per = seqlen // tm
    return pl.pallas_call(
        _inproj_kernel,
        grid=(m // tm, n // tn),
        in_specs=[
            pl.BlockSpec((tm, d), lambda i, j: (i, 0)),
            pl.BlockSpec((1, 1, d), lambda i, j: (i // per, 0, 0)),
            pl.BlockSpec((1, 1, d), lambda i, j: (i // per, 0, 0)),
            pl.BlockSpec((1, d), lambda i, j: (0, 0)),
            pl.BlockSpec((d, tn), lambda i, j: (0, j)),
        ],
        out_specs=pl.BlockSpec((tm, tn), lambda i, j: (i, j)),
        out_shape=jax.ShapeDtypeStruct((m, n), out_dtype),
        scratch_shapes=[pltpu.VMEM((tm, d), BF16)],
        compiler_params=pltpu.CompilerParams(
            dimension_semantics=("parallel", "arbitrary"), vmem_limit_bytes=VMEM_LIMIT),
        name=name,
    )(x2, sh, sc, gain, w)


def _merge_kernel(x_ref, ya_ref, yb_ref, yc_ref, ga_ref, gb_ref, gc_ref,
                  wpa_ref, wpb_ref, wpc_ref, wout_ref, g1_ref, o_ref):
    merged = (_sigmoid(ga_ref[...].astype(F32)) * _dot(ya_ref[...], wpa_ref[...])
              + _sigmoid(gb_ref[...].astype(F32)) * _dot(yb_ref[...], wpb_ref[...])
              + _sigmoid(gc_ref[...].astype(F32)) * _dot(yc_ref[...], wpc_ref[...]))
    out = _dot(merged.astype(BF16), wout_ref[...])
    o_ref[...] = x_ref[...] + g1_ref[0] * out


def _merge(x2, ya, yb, yc, zb, wpa, wpb, wpc, wout, g1, seqlen, tm):
    m, d = x2.shape
    w = ya.shape[1]
    per = seqlen // tm
    row = lambda i: (i, 0)
    const = lambda i: (0, 0)
    return pl.pallas_call(
        _merge_kernel,
        grid=(m // tm,),
        in_specs=[
            pl.BlockSpec((tm, d), row),
            pl.BlockSpec((tm, w), row), pl.BlockSpec((tm, w), row), pl.BlockSpec((tm, w), row),
            pl.BlockSpec((tm, d), lambda i: (i, ZB_GATE // d)),
            pl.BlockSpec((tm, d), lambda i: (i, ZB_GATE // d + 1)),
            pl.BlockSpec((tm, d), lambda i: (i, ZB_GATE // d + 2)),
            pl.BlockSpec((w, d), const), pl.BlockSpec((w, d), const), pl.BlockSpec((w, d), const),
            pl.BlockSpec((d, d), const),
            pl.BlockSpec((1, 1, d), lambda i: (i // per, 0, 0)),
        ],
        out_specs=pl.BlockSpec((tm, d), row),
        out_shape=jax.ShapeDtypeStruct((m, d), F32),
        compiler_params=pltpu.CompilerParams(
            dimension_semantics=("parallel",), vmem_limit_bytes=VMEM_LIMIT),
        name="merge_outproj",
    )(x2, ya, yb, yc, zb, zb, zb, wpa, wpb, wpc, wout, g1)


def _ffn_kernel(x_ref, sh_ref, sc_ref, g2_ref, ng_ref, wg_ref, wu_ref, wo_ref, nf_ref,
                o_ref, h_ref, acc_ref, *, final_norm):
    k = pl.program_id(1)

    @pl.when(k == 0)
    def _():
        h_ref[...] = _norm_mod(x_ref[...], ng_ref[...], sc_ref[0], sh_ref[0]).astype(BF16)
        acc_ref[...] = jnp.zeros_like(acc_ref)

    h = h_ref[...]
    gate = _dot(h, wg_ref[...])
    up = _dot(h, wu_ref[...])
    act = (gate * _sigmoid(gate) * up).astype(BF16)
    acc_ref[...] += _dot(act, wo_ref[...])

    @pl.when(k == pl.num_programs(1) - 1)
    def _():
        xn = x_ref[...] + g2_ref[0] * acc_ref[...]
        if final_norm:
            xn = xn * lax.rsqrt(jnp.mean(xn * xn, axis=-1, keepdims=True) + EPS) * nf_ref[...]
        o_ref[...] = xn


def _ffn(x2, sh, sc, g2, ng, w_in, w_out, nf, seqlen, tm, tf, final_norm):
    m, d = x2.shape
    dff = w_out.shape[0]
    nk = dff // tf
    per = seqlen // tm
    bvec = lambda i, k: (i // per, 0, 0)
    return pl.pallas_call(
        functools.partial(_ffn_kernel, final_norm=final_norm),
        grid=(m // tm, nk),
        in_specs=[
            pl.BlockSpec((tm, d), lambda i, k: (i, 0)),
            pl.BlockSpec((1, 1, d), bvec), pl.BlockSpec((1, 1, d), bvec), pl.BlockSpec((1, 1, d), bvec),
            pl.BlockSpec((1, d), lambda i, k: (0, 0)),
            pl.BlockSpec((d, tf), lambda i, k: (0, k)),
            pl.BlockSpec((d, tf), lambda i, k: (0, nk + k)),
            pl.BlockSpec((tf, d), lambda i, k: (k, 0)),
            pl.BlockSpec((1, d), lambda i, k: (0, 0)),
        ],
        out_specs=pl.BlockSpec((tm, d), lambda i, k: (i, 0)),
        out_shape=jax.ShapeDtypeStruct((m, d), F32),
        scratch_shapes=[pltpu.VMEM((tm, d), BF16), pltpu.VMEM((tm, d), F32)],
        compiler_params=pltpu.CompilerParams(
            dimension_semantics=("parallel", "arbitrary"), vmem_limit_bytes=VMEM_LIMIT),
        name="swiglu_ffn",
    )(x2, sh, sc, g2, ng, w_in, w_in, w_out, nf)


N_LEVELS = CHUNK.bit_length() - 1


def _block_sums(g):
    row = lax.broadcasted_iota(jnp.int32, g.shape, 0)
    lb, tb = g, g
    out = [(lb, tb)]
    b = 1
    while b < CHUNK:
        right = (row & b) != 0
        from_left = pltpu.roll(tb, b, axis=0)
        from_right = pltpu.roll(tb, CHUNK - b, axis=0)
        lb = lb + jnp.where(right, from_left, 0.0)
        tb = tb + jnp.where(right, from_left, from_right)
        out.append((lb, tb))
        b *= 2
    return out


def _gated_factors(g2, forward):
    sums = _block_sums(g2)
    facs = [(jnp.exp2(g2), None)]
    for lb, tb in sums[1:]:
        if forward:
            facs.append((jnp.exp2(lb), jnp.exp2(tb - lb)))
        else:
            facs.append((jnp.exp2((tb - lb) + g2), jnp.exp2(lb - g2)))
    return facs, jnp.exp2(sums[-1][1][0:1, :])


def _fixed_factors(log_decay, forward):
    pos = lax.broadcasted_iota(jnp.int32, (CHUNK, LANES), 0).astype(F32)
    mid = CHUNK // 2
    if forward:
        facs = [(jnp.exp((pos - (mid - 1.0)) * log_decay), jnp.exp(((mid - 1.0) - pos) * log_decay)),
                (jnp.exp((pos + 1.0) * log_decay), jnp.exp((CHUNK - 1.0 - pos) * log_decay))]
    else:
        facs = [(jnp.exp((mid - pos) * log_decay), jnp.exp((pos - mid) * log_decay)),
                (jnp.exp((CHUNK - pos) * log_decay), jnp.exp(pos * log_decay))]
    return facs, jnp.exp(float(CHUNK) * log_decay)


def _store_masks(mask_ref, hierarchical):
    row = lax.broadcasted_iota(jnp.int32, (CHUNK, CHUNK), 0)
    col = lax.broadcasted_iota(jnp.int32, (CHUNK, CHUNK), 1)
    one = lambda m: jnp.where(m, 1.0, 0.0)
    mask_ref[0] = one(row == col)
    n = N_LEVELS if hierarchical else 1
    for level in range(n):
        same_level = ((row ^ col) >> level) == 1 if hierarchical else (row != col)
        mask_ref[1 + level] = one(same_level & (row > col))
        mask_ref[1 + n + level] = one(same_level & (col > row))
    return n


def _lane_masks(n_sub):
    if n_sub == 1:
        return [None]
    lane = lax.broadcasted_iota(jnp.int32, (1, LANES), 1)
    width = LANES // n_sub
    return [jnp.where((lane >= i * width) & (lane < (i + 1) * width), 1.0, 0.0) for i in range(n_sub)]


def _chunk_rows(n):
    if isinstance(n, int):
        return pl.ds(n * CHUNK, CHUNK)
    return pl.ds(pl.multiple_of(n * CHUNK, CHUNK), CHUNK)


def _scan_both(n_chunks, n_groups, n_sub, hierarchical, prep, load_v, of_ref, ob_ref, st_ref, mask_ref,
               *slot_refs):
    st_ref[...] = jnp.zeros_like(st_ref)
    n_levels = _store_masks(mask_ref, hierarchical)
    lane_masks = _lane_masks(n_sub)
    chains = [(gi, forward) for gi in range(n_groups) for forward in (True, False)]
    slots = (slot_refs[:4], slot_refs[4:])

    def rows_of(n, forward):
        return _chunk_rows(n if forward else n_chunks - 1 - n)

    def stage(n, slot):
        qt_ref, kt_ref, p0_ref, dec_ref = slots[slot]
        for c, (gi, forward) in enumerate(chains):
            q, k, (facs, chunk_decay) = prep(gi, forward, rows_of(n, forward))
            for level, (fq, fk) in enumerate(facs):
                qt_ref[c, level] = (q * fq).astype(BF16)
                kt_ref[c, level] = (k if fk is None else k * fk).astype(BF16)
            dec_ref[c] = chunk_decay
            if forward:
                for i in range(n_sub):
                    qm = q if lane_masks[i] is None else q * lane_masks[i]
                    p0_ref[gi * n_sub + i] = mask_ref[0] * jnp.sum(qm * k, axis=-1, keepdims=True)

    def consume(n, slot):
        qt_ref, kt_ref, p0_ref, dec_ref = slots[slot]
        for c, (gi, forward) in enumerate(chains):
            rows = rows_of(n, forward)
            o_ref = of_ref if forward else ob_ref
            mask0 = 1 if forward else 1 + n_levels
            for i, v in enumerate(load_v(gi, rows)):
                head = gi * n_sub + i
                sub = (lambda t: t) if lane_masks[i] is None else (lambda t, m=lane_masks[i].astype(BF16): t * m)
                p = p0_ref[head] if forward else None
                for level in range(n_levels):
                    s = _dot_nt(sub(qt_ref[c, level]), kt_ref[c, level]) * mask_ref[mask0 + level]
                    p = s if p is None else p + s
                st = st_ref[c * n_sub + i]
                inter = _dot_nt(sub(qt_ref[c, n_levels]), st.astype(BF16))
                o_ref[rows, head * DV:(head + 1) * DV] = _dot(p.astype(BF16), v) + inter
                st_ref[c * n_sub + i] = st * dec_ref[c] + _dot_tn(v, kt_ref[c, n_levels])

    def body(m, carry):
        n = 2 * m
        consume(n, 0)
        stage(n + 1, 1)
        consume(n + 1, 1)
        stage(jnp.minimum(n + 2, n_chunks - 1), 0)
        return carry

    stage(0, 0)
    lax.fori_loop(0, n_chunks // 2, body, 0)


def _finish(n_heads, of_ref, ob_ref, gain_ref, gate_ref, y_ref, center, swish):
    rows_per = 256
    n_blocks = of_ref.shape[0] // rows_per

    def body(b, carry):
        rows = pl.ds(pl.multiple_of(b * rows_per, rows_per), rows_per)
        for i in range(n_heads):
            cols = slice(i * DV, (i + 1) * DV)
            o = of_ref[rows, cols] + ob_ref[rows, cols]
            if center:
                o = o - jnp.mean(o, axis=-1, keepdims=True)
            o = o * lax.rsqrt(jnp.mean(o * o, axis=-1, keepdims=True) + EPS)
            gz = gate_ref[rows, cols].astype(F32)
            act = gz * _sigmoid(gz) if swish else _sigmoid(gz)
            y_ref[rows, cols] = ((o * gain_ref[:, cols]) * act).astype(y_ref.dtype)
        return carry

    lax.fori_loop(0, n_blocks, body, 0)


def _group_cols(gi):
    return slice(gi * LANES, (gi + 1) * LANES)


def _mixer_a_kernel(q_ref, v_ref, og_ref, ff_ref, fb_ref, lb_ref, gain_ref, y_ref,
                    of_ref, ob_ref, *scan_refs):
    n_groups = q_ref.shape[1] // LANES

    def prep(gi, forward, rows):
        cols = _group_cols(gi)
        lb = lb_ref[:, cols]
        z = (ff_ref if forward else fb_ref)[rows, cols]
        f = lb + (1.0 - lb) * _sigmoid(z)
        g2 = jnp.log2(jnp.maximum(f, TINY))
        key = 1.0 - f
        aq = q_ref[rows, cols].astype(F32)
        return aq * _sigmoid(aq), key, _gated_factors(g2, forward)

    _scan_both(q_ref.shape[0] // CHUNK, n_groups, 1, True, prep,
               lambda gi, rows: [v_ref[rows, _group_cols(gi)]],
               of_ref, ob_ref, *scan_refs)
    _finish(n_groups, of_ref, ob_ref, gain_ref, og_ref, y_ref, center=False, swish=False)


def _rotate_half(x):
    lane = lax.broadcasted_iota(jnp.int32, x.shape, 1)
    half = BC_DK // 2
    first = (lane % BC_DK) < half
    return jnp.where(first, pltpu.roll(x, LANES - half, axis=1), pltpu.roll(x, half, axis=1))


def _sub_values(v_ref, gi, rows):
    return [v_ref[rows, (2 * gi + i) * DV:(2 * gi + i + 1) * DV] for i in range(2)]


def _mixer_b_kernel(q_ref, k_ref, v_ref, sg_ref, cos_ref, sin_ref, lgf_ref, lgb_ref, gain_ref, y_ref,
                    of_ref, ob_ref, *scan_refs):
    n_groups = q_ref.shape[1] // LANES
    packs = [(_fixed_factors(lgb_ref[:, _group_cols(gi)], False), _fixed_factors(lgf_ref[:, _group_cols(gi)], True))
             for gi in range(n_groups)]

    def prep(gi, forward, rows):
        cols = _group_cols(gi)
        cos = cos_ref[rows, :]
        sin = sin_ref[rows, :]
        q = q_ref[rows, cols].astype(F32)
        k = k_ref[rows, cols].astype(F32)
        q = q * cos + _rotate_half(q) * sin
        k = (k * cos + _rotate_half(k) * sin) * (BC_DK ** -0.5)
        return q, k, packs[gi][1 if forward else 0]

    _scan_both(q_ref.shape[0] // CHUNK, n_groups, 2, False, prep,
               functools.partial(_sub_values, v_ref), of_ref, ob_ref, *scan_refs)
    _finish(2 * n_groups, of_ref, ob_ref, gain_ref, sg_ref, y_ref, center=True, swish=True)


def _mixer_c_kernel(q_ref, k_ref, v_ref, sg_ref, lr_ref, waf_ref, wab_ref, baf_ref, bab_ref, gain_ref, y_ref,
                    of_ref, ob_ref, *scan_refs):
    n_groups = q_ref.shape[1] // LANES

    def prep(gi, forward, rows):
        cols = _group_cols(gi)
        w_ref, b_ref = (waf_ref, baf_ref) if forward else (wab_ref, bab_ref)
        x = _dot(lr_ref[rows, :].astype(BF16), w_ref[:, cols]) + b_ref[:, cols]
        log_sig = jnp.minimum(x, 0.0) - jnp.log(1.0 + jnp.exp(-jnp.abs(x)))
        g2 = log_sig * (LOG2_E / GLA_TAU)
        q = q_ref[rows, cols].astype(F32) * (BC_DK ** -0.5)
        return q, k_ref[rows, cols].astype(F32), _gated_factors(g2, forward)

    _scan_both(q_ref.shape[0] // CHUNK, n_groups, 2, True, prep,
               functools.partial(_sub_values, v_ref), of_ref, ob_ref, *scan_refs)
    _finish(2 * n_groups, of_ref, ob_ref, gain_ref, sg_ref, y_ref, center=False, swish=True)


def _mixer_call(kernel, name, bsz, seqlen, grid_groups, qk_width, out_width, hierarchical, in_arrays, in_specs):
    n_heads = out_width // DV
    n_chains = 2 * (qk_width // LANES)
    n_states = 2 * n_heads
    n_levels = N_LEVELS if hierarchical else 1
    n_masks = 1 + 2 * n_levels
    return pl.pallas_call(
        kernel,
        grid=(bsz, grid_groups),
        in_specs=in_specs,
        out_specs=pl.BlockSpec((seqlen, out_width), lambda b, g: (b, g)),
        out_shape=jax.ShapeDtypeStruct((bsz * seqlen, grid_groups * out_width), BF16),
        scratch_shapes=[
            pltpu.VMEM((seqlen, out_width), F32),
            pltpu.VMEM((seqlen, out_width), F32),
            pltpu.VMEM((n_states, DV, LANES), F32),
            pltpu.VMEM((n_masks, CHUNK, CHUNK), F32),
        ] + 2 * [
            pltpu.VMEM((n_chains, n_levels + 1, CHUNK, LANES), BF16),
            pltpu.VMEM((n_chains, n_levels + 1, CHUNK, LANES), BF16),
            pltpu.VMEM((n_heads, CHUNK, CHUNK), F32),
            pltpu.VMEM((n_chains, 1, LANES), F32),
        ],
        compiler_params=pltpu.CompilerParams(
            dimension_semantics=("parallel", "parallel"), vmem_limit_bytes=VMEM_LIMIT),
        name=name,
    )(*in_arrays)


def _zspec(seqlen, width, col0):
    blk = col0 // width
    return pl.BlockSpec((seqlen, width), lambda b, g: (b, blk + g))


def _vspec(width, blk0=0):
    return pl.BlockSpec((1, width), lambda b, g: (0, blk0 + g))


def _mixer_a(zb, zf, lb, gain, bsz, seqlen):
    t, w = seqlen, 2 * LANES
    return _mixer_call(
        _mixer_a_kernel, "mixer_hgrn2", bsz, t, N_HEADS * A_DK // w, w, w, True,
        (zb, zb, zb, zf, zf, lb, gain),
        [_zspec(t, w, ZB_AQ), _zspec(t, w, ZB_AI), _zspec(t, w, ZB_AG),
         _zspec(t, w, ZF_FF), _zspec(t, w, ZF_FB), _vspec(w), _vspec(w)])


def _mixer_b(zb, cos, sin, lgf, lgb, gain, bsz, seqlen):
    t, w, wv = seqlen, N_HEADS * BC_DK, N_HEADS * DV
    table = pl.BlockSpec((t, LANES), lambda b, g: (0, 0))
    return _mixer_call(
        _mixer_b_kernel, "mixer_retention", bsz, t, 1, w, wv, False,
        (zb, zb, zb, zb, cos, sin, lgf, lgb, gain),
        [_zspec(t, w, ZB_BQ), _zspec(t, w, ZB_BK), _zspec(t, wv, ZB_BV), _zspec(t, wv, ZB_BG),
         table, table, _vspec(w), _vspec(w), _vspec(wv)])


def _mixer_c(zb, zf, wal, bal, gain, bsz, seqlen):
    t, w, wv = seqlen, N_HEADS * BC_DK, N_HEADS * DV
    return _mixer_call(
        _mixer_c_kernel, "mixer_gla", bsz, t, 1, w, wv, True,
        (zb, zb, zb, zb, zf, wal, wal, bal, bal, gain),
        [_zspec(t, w, ZB_CQ), _zspec(t, w, ZB_CK), _zspec(t, wv, ZB_CV), _zspec(t, wv, ZB_CG),
         pl.BlockSpec((t, LANES), lambda b, g: (b, ZF_LR // LANES)),
         pl.BlockSpec((LANES, w), lambda b, g: (0, 0)),
         pl.BlockSpec((LANES, w), lambda b, g: (0, 1)),
         _vspec(w), _vspec(w, 1), _vspec(wv)])


def _split_w_in(w):
    o = {}
    off = 0
    for name, size in (("a_q", 512), ("a_ff", 512), ("a_fb", 512), ("a_i", 512), ("a_g", 512),
                       ("b_q", 256), ("b_k", 256), ("b_v", 512), ("b_g", 512),
                       ("c_q", 256), ("c_k", 256), ("c_v", 512), ("c_g", 512), ("c_lr", 32),
                       ("gate_a", 1024), ("gate_b", 1024), ("gate_c", 1024)):
        o[name] = w[:, off:off + size]
        off += size
    wb = jnp.concatenate([o[n] for n in ("gate_a", "gate_b", "gate_c", "a_q", "a_i", "a_g",
                                         "b_q", "b_k", "b_v", "b_g", "c_q", "c_k", "c_v", "c_g")], axis=1)
    pad = jnp.zeros((w.shape[0], ZF_COLS - ZF_LR - 2 * GLA_RANK), w.dtype)
    wf = jnp.concatenate([o["a_ff"], o["a_fb"], o["c_lr"], pad], axis=1)
    return wb.astype(BF16), wf.astype(BF16)


def _rotary_tables(seqlen):
    pos = jnp.arange(seqlen, dtype=F32)
    inv_freq = 10000.0 ** (-jnp.arange(0, BC_DK, 2, dtype=F32) / BC_DK)
    ang = pos[:, None] * inv_freq[None, :]
    cos, sin = jnp.cos(ang), jnp.sin(ang)
    reps = LANES // BC_DK
    return (jnp.tile(jnp.concatenate([cos, cos], axis=1), (1, reps)),
            jnp.tile(jnp.concatenate([-sin, sin], axis=1), (1, reps)))


def _retention_log_decays():
    h = jnp.arange(N_HEADS, dtype=F32)
    fwd = jnp.log1p(-jnp.exp2(-5.0 - h))
    bwd = jnp.log1p(-jnp.exp2(-5.0 - h[::-1]))
    spread = lambda v: jnp.repeat(v, BC_DK)[None, :]
    return spread(fwd), spread(bwd)


def kernel(x, c, norm1_g, w_ada, b_ada, w_in, lb_logits, norm_a_g, norm_b_g, norm_c_g, w_alpha, b_alpha,
           w_pa, w_pb, w_pc, w_out, norm2_g, w_ffn_in, w_ffn_out, norm_f_g):
    bsz, seqlen, d = x.shape
    depth = w_in.shape[0]
    assert d == D_MODEL and seqlen % 256 == 0
    tm = min(512, seqlen)
    tm_in = min(1024, seqlen)

    mod = _modulation(c, w_ada, b_ada)
    lbs = _lower_bounds(lb_logits)
    cos, sin = _rotary_tables(seqlen)
    lgf, lgb = _retention_log_decays()

    x2 = x.reshape(bsz * seqlen, d)
    for l in range(depth):
        sh1, sc1, g1, sh2, sc2, g2 = [mod[l, :, i * d:(i + 1) * d].reshape(bsz, 1, d) for i in range(6)]
        wb, wf = _split_w_in(w_in[l])
        gain1 = norm1_g[l].reshape(1, d)
        zb = _inproj(x2, sh1, sc1, gain1, wb, BF16, seqlen, tm_in, 1280, "inproj_main")
        zf = _inproj(x2, sh1, sc1, gain1, wf, F32, seqlen, tm_in, ZF_COLS, "inproj_gates")

        wal = jnp.zeros((LANES, 2 * N_HEADS * BC_DK), F32)
        wal = wal.at[0:GLA_RANK, 0:N_HEADS * BC_DK].set(w_alpha[l, 0])
        wal = wal.at[GLA_RANK:2 * GLA_RANK, N_HEADS * BC_DK:].set(w_alpha[l, 1])
        bal = b_alpha[l].reshape(1, 2 * N_HEADS * BC_DK)

        ya = _mixer_a(zb, zf, lbs[l].reshape(1, -1), norm_a_g[l].reshape(1, -1), bsz, seqlen)
        yb = _mixer_b(zb, cos, sin, lgf, lgb, norm_b_g[l].reshape(1, -1), bsz, seqlen)
        yc = _mixer_c(zb, zf, wal.astype(BF16), bal, norm_c_g[l].reshape(1, -1), bsz, seqlen)

        x2 = _merge(x2, ya, yb, yc, zb, w_pa[l].astype(BF16), w_pb[l].astype(BF16), w_pc[l].astype(BF16),
                    w_out[l].astype(BF16), g1, seqlen, tm)
        x2 = _ffn(x2, sh2, sc2, g2, norm2_g[l].reshape(1, d), w_ffn_in[l].astype(BF16),
                  w_ffn_out[l].astype(BF16), norm_f_g.reshape(1, d), seqlen, tm, D_FF // 2,
                  final_norm=(l == depth - 1))
    return x2.reshape(bsz, seqlen, d)
```

```python
import functools
from typing import Any, Callable, NamedTuple

import jax
import jax.numpy as jnp
from jax import lax
from jax.experimental import pallas as pl
from jax.experimental.pallas import tpu as pltpu

F32 = jnp.float32
BF16 = jnp.bfloat16

D_MODEL = 1024
N_HEADS = 4
A_DK = 128
BC_DK = 64
DV = 128
GLA_RANK = 16
GLA_TAU = 16.0
D_FF = 2816
EPS = 1e-6
TINY = 1e-30
LOG2_E = 1.4426950408889634

LANES = 128
CHUNK = 128
VMEM_LIMIT = 48 * 1024 * 1024

ZB_GATE = 0
ZB_AQ, ZB_AI, ZB_AG = 3072, 3584, 4096
ZB_BQ, ZB_BK, ZB_BV, ZB_BG = 4608, 4864, 5120, 5632
ZB_CQ, ZB_CK, ZB_CV, ZB_CG = 6144, 6400, 6656, 7168
ZB_COLS = 7680
ZF_FF, ZF_FB, ZF_LR = 0, 512, 1024
ZF_COLS = 1152


def _dot(a, b):
    return jnp.dot(a, b, preferred_element_type=F32)


def _dot_nt(a, b):
    return lax.dot_general(a, b, (((1,), (1,)), ((), ())), preferred_element_type=F32)


def _dot_tn(a, b):
    return lax.dot_general(a, b, (((0,), (0,)), ((), ())), preferred_element_type=F32)


def _sigmoid(x):
    return jax.nn.sigmoid(x)


def _mod_kernel(c_ref, w_ref, b_ref, o_ref):
    c = c_ref[...]
    a = c * _sigmoid(c)
    w = w_ref[...]
    a_hi = a.astype(BF16)
    a_lo = (a - a_hi.astype(F32)).astype(BF16)
    w_hi = w.astype(BF16)
    w_lo = (w - w_hi.astype(F32)).astype(BF16)
    o_ref[...] = _dot(a_hi, w_hi) + _dot(a_hi, w_lo) + _dot(a_lo, w_hi) + b_ref[...]


def _modulation(c, w_ada, b_ada):
    depth, d, n6 = w_ada.shape
    bsz = c.shape[0]
    return pl.pallas_call(
        _mod_kernel,
        grid=(depth, n6 // d),
        in_specs=[
            pl.BlockSpec((bsz, d), lambda l, j: (0, 0)),
            pl.BlockSpec((None, d, d), lambda l, j: (l, 0, j)),
            pl.BlockSpec((None, 1, d), lambda l, j: (l, 0, j)),
        ],
        out_specs=pl.BlockSpec((None, bsz, d), lambda l, j: (l, 0, j)),
        out_shape=jax.ShapeDtypeStruct((depth, bsz, n6), F32),
        name="adaln_modulation",
    )(c, w_ada, b_ada.reshape(depth, 1, n6))


def _lb_kernel(x_ref, o_ref):
    depth = x_ref.shape[0]
    rows = [x_ref[i:i + 1, :] for i in range(depth)]
    m = rows[0]
    for r in rows[1:]:
        m = jnp.maximum(m, r)
    e = [jnp.exp(r - m) for r in rows]
    s = e[0]
    for t in e[1:]:
        s = s + t
    p = [t / s for t in e]
    acc = p[0]
    o_ref[0:1, :] = jnp.maximum(acc - p[0], 0.0)
    for i in range(1, depth):
        acc = acc + p[i]
        o_ref[i:i + 1, :] = jnp.maximum(acc - p[0], 0.0)


def _lower_bounds(lb_logits):
    return pl.pallas_call(
        _lb_kernel,
        out_shape=jax.ShapeDtypeStruct(lb_logits.shape, F32),
        name="hgrn2_lower_bounds",
    )(lb_logits)


def _norm_mod(x, gain, sc, sh):
    y = x * lax.rsqrt(jnp.mean(x * x, axis=-1, keepdims=True) + EPS)
    return (y * gain) * (1.0 + sc) + sh


def _inproj_kernel(x_ref, sh_ref, sc_ref, g_ref, w_ref, o_ref, h_ref):
    @pl.when(pl.program_id(1) == 0)
    def _():
        h_ref[...] = _norm_mod(x_ref[...], g_ref[...], sc_ref[0], sh_ref[0]).astype(BF16)

    o_ref[...] = _dot(h_ref[...], w_ref[...]).astype(o_ref.dtype)


def _inproj(x2, sh, sc, gain, w, out_dtype, seqlen, tm, tn, name):
    m, d = x2.shape
    n = w.shape[1]
    per = seqlen // tm
    return pl.pallas_call(
        _inproj_kernel,
        grid=(m // tm, n // tn),
        in_specs=[
            pl.BlockSpec((tm, d), lambda i, j: (i, 0)),
            pl.BlockSpec((1, 1, d), lambda i, j: (i // per, 0, 0)),
            pl.BlockSpec((1, 1, d), lambda i, j: (i // per, 0, 0)),
            pl.BlockSpec((1, d), lambda i, j: (0, 0)),
            pl.BlockSpec((d, tn), lambda i, j: (0, j)),
        ],
        out_specs=pl.BlockSpec((tm, tn), lambda i, j: (i, j)),
        out_shape=jax.ShapeDtypeStruct((m, n), out_dtype),
        scratch_shapes=[pltpu.VMEM((tm, d), BF16)],
        compiler_params=pltpu.CompilerParams(
            dimension_semantics=("parallel", "arbitrary"), vmem_limit_bytes=VMEM_LIMIT),
        name=name,
    )(x2, sh, sc, gain, w)


def _merge_kernel(x_ref, ya_ref, yb_ref, yc_ref, ga_ref, gb_ref, gc_ref,
                  wpa_ref, wpb_ref, wpc_ref, wout_ref, g1_ref, o_ref):
    merged = (_sigmoid(ga_ref[...].astype(F32)) * _dot(ya_ref[...], wpa_ref[...])
              + _sigmoid(gb_ref[...].astype(F32)) * _dot(yb_ref[...], wpb_ref[...])
              + _sigmoid(gc_ref[...].astype(F32)) * _dot(yc_ref[...], wpc_ref[...]))
    out = _dot(merged.astype(BF16), wout_ref[...])
    o_ref[...] = x_ref[...] + g1_ref[0] * out


def _merge(x2, ya, yb, yc, zb, wpa, wpb, wpc, wout, g1, seqlen, tm):
    m, d = x2.shape
    w = ya.shape[1]
    per = seqlen // tm
    row = lambda i: (i, 0)
    const = lambda i: (0, 0)
    return pl.pallas_call(
        _merge_kernel,
        grid=(m // tm,),
        in_specs=[
            pl.BlockSpec((tm, d), row),
            pl.BlockSpec((tm, w), row), pl.BlockSpec((tm, w), row), pl.BlockSpec((tm, w), row),
            pl.BlockSpec((tm, d), lambda i: (i, ZB_GATE // d)),
            pl.BlockSpec((tm, d), lambda i: (i, ZB_GATE // d + 1)),
            pl.BlockSpec((tm, d), lambda i: (i, ZB_GATE // d + 2)),
            pl.BlockSpec((w, d), const), pl.BlockSpec((w, d), const), pl.BlockSpec((w, d), const),
            pl.BlockSpec((d, d), const),
            pl.BlockSpec((1, 1, d), lambda i: (i // per, 0, 0)),
        ],
        out_specs=pl.BlockSpec((tm, d), row),
        out_shape=jax.ShapeDtypeStruct((m, d), F32),
        compiler_params=pltpu.CompilerParams(
            dimension_semantics=("parallel",), vmem_limit_bytes=VMEM_LIMIT),
        name="merge_outproj",
    )(x2, ya, yb, yc, zb, zb, zb, wpa, wpb, wpc, wout, g1)


def _ffn_kernel(x_ref, sh_ref, sc_ref, g2_ref, ng_ref, wg_ref, wu_ref, wo_ref, nf_ref,
                o_ref, h_ref, acc_ref, *, final_norm):
    k = pl.program_id(1)

    @pl.when(k == 0)
    def _():
        h_ref[...] = _norm_mod(x_ref[...], ng_ref[...], sc_ref[0], sh_ref[0]).astype(BF16)
        acc_ref[...] = jnp.zeros_like(acc_ref)

    h = h_ref[...]
    gate = _dot(h, wg_ref[...])
    up = _dot(h, wu_ref[...])
    act = (gate * _sigmoid(gate) * up).astype(BF16)
    acc_ref[...] += _dot(act, wo_ref[...])

    @pl.when(k == pl.num_programs(1) - 1)
    def _():
        xn = x_ref[...] + g2_ref[0] * acc_ref[...]
        if final_norm:
            xn = xn * lax.rsqrt(jnp.mean(xn * xn, axis=-1, keepdims=True) + EPS) * nf_ref[...]
        o_ref[...] = xn


def _ffn(x2, sh, sc, g2, ng, w_in, w_out, nf, seqlen, tm, tf, final_norm):
    m, d = x2.shape
    dff = w_out.shape[0]
    nk = dff // tf
    per = seqlen // tm
    bvec = lambda i, k: (i // per, 0, 0)
    return pl.pallas_call(
        functools.partial(_ffn_kernel, final_norm=final_norm),
        grid=(m // tm, nk),
        in_specs=[
            pl.BlockSpec((tm, d), lambda i, k: (i, 0)),
            pl.BlockSpec((1, 1, d), bvec), pl.BlockSpec((1, 1, d), bvec), pl.BlockSpec((1, 1, d), bvec),
            pl.BlockSpec((1, d), lambda i, k: (0, 0)),
            pl.BlockSpec((d, tf), lambda i, k: (0, k)),
            pl.BlockSpec((d, tf), lambda i, k: (0, nk + k)),
            pl.BlockSpec((tf, d), lambda i, k: (k, 0)),
            pl.BlockSpec((1, d), lambda i, k: (0, 0)),
        ],
        out_specs=pl.BlockSpec((tm, d), lambda i, k: (i, 0)),
        out_shape=jax.ShapeDtypeStruct((m, d), F32),
        scratch_shapes=[pltpu.VMEM((tm, d), BF16), pltpu.VMEM((tm, d), F32)],
        compiler_params=pltpu.CompilerParams(
            dimension_semantics=("parallel", "arbitrary"), vmem_limit_bytes=VMEM_LIMIT),
        name="swiglu_ffn",
    )(x2, sh, sc, g2, ng, w_in, w_in, w_out, nf)


N_LEVELS = CHUNK.bit_length() - 1


def _block_sums(g):
    row = lax.broadcasted_iota(jnp.int32, g.shape, 0)
    lb, tb = g, g
    out = [(lb, tb)]
    b = 1
    while b < CHUNK:
        right = (row & b) != 0
        from_left = pltpu.roll(tb, b, axis=0)
        from_right = pltpu.roll(tb, CHUNK - b, axis=0)
        lb = lb + jnp.where(right, from_left, 0.0)
        tb = tb + jnp.where(right, from_left, from_right)
        out.append((lb, tb))
        b *= 2
    return out


def _gated_factors(g2, forward):
    sums = _block_sums(g2)
    facs = [(jnp.exp2(g2), None)]
    for lb, tb in sums[1:]:
        if forward:
            facs.append((jnp.exp2(lb), jnp.exp2(tb - lb)))
        else:
            facs.append((jnp.exp2((tb - lb) + g2), jnp.exp2(lb - g2)))
    return facs, jnp.exp2(sums[-1][1][0:1, :])


def _fixed_factors(log_decay, forward):
    pos = lax.broadcasted_iota(jnp.int32, (CHUNK, LANES), 0).astype(F32)
    mid = CHUNK // 2
    if forward:
        facs = [(jnp.exp((pos - (mid - 1.0)) * log_decay), jnp.exp(((mid - 1.0) - pos) * log_decay)),
                (jnp.exp((pos + 1.0) * log_decay), jnp.exp((CHUNK - 1.0 - pos) * log_decay))]
    else:
        facs = [(jnp.exp((mid - pos) * log_decay), jnp.exp((pos - mid) * log_decay)),
                (jnp.exp((CHUNK - pos) * log_decay), jnp.exp(pos * log_decay))]
    return facs, jnp.exp(float(CHUNK) * log_decay)


def _store_masks(mask_ref, hierarchical):
    row = lax.broadcasted_iota(jnp.int32, (CHUNK, CHUNK), 0)
    col = lax.broadcasted_iota(jnp.int32, (CHUNK, CHUNK), 1)
    one = lambda m: jnp.where(m, 1.0, 0.0)
    mask_ref[0] = one(row == col)
    n = N_LEVELS if hierarchical else 1
    for level in range(n):
        same_level = ((row ^ col) >> level) == 1 if hierarchical else (row != col)
        mask_ref[1 + level] = one(same_level & (row > col))
        mask_ref[1 + n + level] = one(same_level & (col > row))
    return n


def _lane_masks(n_sub):
    if n_sub == 1:
        return [None]
    lane = lax.broadcasted_iota(jnp.int32, (1, LANES), 1)
    width = LANES // n_sub
    return [jnp.where((lane >= i * width) & (lane < (i + 1) * width), 1.0, 0.0) for i in range(n_sub)]


def _chunk_rows(n):
    if isinstance(n, int):
        return pl.ds(n * CHUNK, CHUNK)
    return pl.ds(pl.multiple_of(n * CHUNK, CHUNK), CHUNK)


class _Mixer(NamedTuple):
    n_groups: int
    n_sub: int
    hierarchical: bool
    prep: Callable
    load_v: Callable
    of_ref: Any
    ob_ref: Any
    st_ref: Any
    mask_ref: Any
    slots: tuple


def _scan_mixers(n_chunks, mixers):
    levels = []
    for m in mixers:
        m.st_ref[...] = jnp.zeros_like(m.st_ref)
        levels.append(_store_masks(m.mask_ref, m.hierarchical))

    def rows_of(n, forward):
        return _chunk_rows(n if forward else n_chunks - 1 - n)

    def chains(m):
        return enumerate((gi, forward) for gi in range(m.n_groups) for forward in (True, False))

    def stage(n, slot):
        for m in mixers:
            qt_ref, kt_ref, p0_ref, dec_ref = m.slots[slot]
            lane_masks = _lane_masks(m.n_sub)
            for c, (gi, forward) in chains(m):
                q, k, (facs, chunk_decay) = m.prep(gi, forward, rows_of(n, forward))
                for level, (fq, fk) in enumerate(facs):
                    qt_ref[c, level] = (q * fq).astype(BF16)
                    kt_ref[c, level] = (k if fk is None else k * fk).astype(BF16)
                dec_ref[c] = chunk_decay
                if forward:
                    for i in range(m.n_sub):
                        qm = q if lane_masks[i] is None else q * lane_masks[i]
                        p0_ref[gi * m.n_sub + i] = m.mask_ref[0] * jnp.sum(qm * k, axis=-1, keepdims=True)

    def consume(n, slot):
        for m, n_levels in zip(mixers, levels):
            qt_ref, kt_ref, p0_ref, dec_ref = m.slots[slot]
            lane_masks = _lane_masks(m.n_sub)
            for c, (gi, forward) in chains(m):
                rows = rows_of(n, forward)
                o_ref = m.of_ref if forward else m.ob_ref
                mask0 = 1 if forward else 1 + n_levels
                for i, v in enumerate(m.load_v(gi, rows)):
                    head = gi * m.n_sub + i
                    sub = (lambda t: t) if lane_masks[i] is None else (lambda t, lm=lane_masks[i].astype(BF16): t * lm)
                    p = p0_ref[head] if forward else None
                    for level in range(n_levels):
                        s = _dot_nt(sub(qt_ref[c, level]), kt_ref[c, level]) * m.mask_ref[mask0 + level]
                        p = s if p is None else p + s
                    st = m.st_ref[c * m.n_sub + i]
                    inter = _dot_nt(sub(qt_ref[c, n_levels]), st.astype(BF16))
                    o_ref[rows, head * DV:(head + 1) * DV] = _dot(p.astype(BF16), v) + inter
                    m.st_ref[c * m.n_sub + i] = st * dec_ref[c] + _dot_tn(v, kt_ref[c, n_levels])

    def body(step, carry):
        n = 2 * step
        consume(n, 0)
        stage(n + 1, 1)
        consume(n + 1, 1)
        stage(jnp.minimum(n + 2, n_chunks - 1), 0)
        return carry

    stage(0, 0)
    lax.fori_loop(0, n_chunks // 2, body, 0)


def _finish(mixer, gain_ref, gate_ref, y_ref, center, swish):
    rows_per = 256
    n_blocks = mixer.of_ref.shape[0] // rows_per

    def body(b, carry):
        rows = pl.ds(pl.multiple_of(b * rows_per, rows_per), rows_per)
        for i in range(mixer.n_groups * mixer.n_sub):
            cols = slice(i * DV, (i + 1) * DV)
            o = mixer.of_ref[rows, cols] + mixer.ob_ref[rows, cols]
            if center:
                o = o - jnp.mean(o, axis=-1, keepdims=True)
            o = o * lax.rsqrt(jnp.mean(o * o, axis=-1, keepdims=True) + EPS)
            gz = gate_ref[rows, cols].astype(F32)
            act = gz * _sigmoid(gz) if swish else _sigmoid(gz)
            y_ref[rows, cols] = ((o * gain_ref[:, cols]) * act).astype(y_ref.dtype)
        return carry

    lax.fori_loop(0, n_blocks, body, 0)


def _group_cols(gi):
    return slice(gi * LANES, (gi + 1) * LANES)


N_MIXER_SCRATCH = 12


def _mixer(n_groups, n_sub, hierarchical, prep, load_v, scratch):
    of_ref, ob_ref, st_ref, mask_ref = scratch[:4]
    return _Mixer(n_groups, n_sub, hierarchical, prep, load_v, of_ref, ob_ref, st_ref, mask_ref,
                  (scratch[4:8], scratch[8:12]))


def _mixer_scratch(seqlen, n_groups, n_sub, hierarchical):
    n_heads, n_chains = n_groups * n_sub, 2 * n_groups
    n_levels = N_LEVELS if hierarchical else 1
    slot = [pltpu.VMEM((n_chains, n_levels + 1, CHUNK, LANES), BF16),
            pltpu.VMEM((n_chains, n_levels + 1, CHUNK, LANES), BF16),
            pltpu.VMEM((n_heads, CHUNK, CHUNK), F32),
            pltpu.VMEM((n_chains, 1, LANES), F32)]
    return [pltpu.VMEM((seqlen, n_heads * DV), F32),
            pltpu.VMEM((seqlen, n_heads * DV), F32),
            pltpu.VMEM((2 * n_heads, DV, LANES), F32),
            pltpu.VMEM((1 + 2 * n_levels, CHUNK, CHUNK), F32)] + 2 * slot


def _hgrn2_mixer(q_ref, v_ref, ff_ref, fb_ref, lb_ref, scratch):
    def prep(gi, forward, rows):
        cols = _group_cols(gi)
        lb = lb_ref[:, cols]
        z = (ff_ref if forward else fb_ref)[rows, cols]
        f = lb + (1.0 - lb) * _sigmoid(z)
        g2 = jnp.log2(jnp.maximum(f, TINY))
        key = 1.0 - f
        aq = q_ref[rows, cols].astype(F32)
        return aq * _sigmoid(aq), key, _gated_factors(g2, forward)

    return _mixer(q_ref.shape[1] // LANES, 1, True, prep,
                  lambda gi, rows: [v_ref[rows, _group_cols(gi)]], scratch)


def _rotate_half(x):
    lane = lax.broadcasted_iota(jnp.int32, x.shape, 1)
    half = BC_DK // 2
    first = (lane % BC_DK) < half
    return jnp.where(first, pltpu.roll(x, LANES - half, axis=1), pltpu.roll(x, half, axis=1))


def _sub_values(v_ref, gi, rows):
    return [v_ref[rows, (2 * gi + i) * DV:(2 * gi + i + 1) * DV] for i in range(2)]


def _retention_mixer(q_ref, k_ref, v_ref, cos_ref, sin_ref, lgf_ref, lgb_ref, scratch):
    n_groups = q_ref.shape[1] // LANES
    packs = [(_fixed_factors(lgb_ref[:, _group_cols(gi)], False), _fixed_factors(lgf_ref[:, _group_cols(gi)], True))
             for gi in range(n_groups)]

    def prep(gi, forward, rows):
        cols = _group_cols(gi)
        cos = cos_ref[rows, :]
        sin = sin_ref[rows, :]
        q = q_ref[rows, cols].astype(F32)
        k = k_ref[rows, cols].astype(F32)
        q = q * cos + _rotate_half(q) * sin
        k = (k * cos + _rotate_half(k) * sin) * (BC_DK ** -0.5)
        return q, k, packs[gi][1 if forward else 0]

    return _mixer(n_groups, 2, False, prep, functools.partial(_sub_values, v_ref), scratch)


def _gla_mixer(q_ref, k_ref, v_ref, lr_ref, waf_ref, wab_ref, baf_ref, bab_ref, scratch):
    def prep(gi, forward, rows):
        cols = _group_cols(gi)
        w_ref, b_ref = (waf_ref, baf_ref) if forward else (wab_ref, bab_ref)
        x = _dot(lr_ref[rows, :].astype(BF16), w_ref[:, cols]) + b_ref[:, cols]
        log_sig = jnp.minimum(x, 0.0) - jnp.log(1.0 + jnp.exp(-jnp.abs(x)))
        g2 = log_sig * (LOG2_E / GLA_TAU)
        q = q_ref[rows, cols].astype(F32) * (BC_DK ** -0.5)
        return q, k_ref[rows, cols].astype(F32), _gated_factors(g2, forward)

    return _mixer(q_ref.shape[1] // LANES, 2, True, prep, functools.partial(_sub_values, v_ref), scratch)


def _mixers_ac_kernel(aq_ref, av_ref, aog_ref, ff_ref, fb_ref, lb_ref, again_ref,
                      cq_ref, ck_ref, cv_ref, csg_ref, lr_ref, waf_ref, wab_ref, baf_ref, bab_ref, cgain_ref,
                      ya_ref, yc_ref, *scratch):
    hgrn2 = _hgrn2_mixer(aq_ref, av_ref, ff_ref, fb_ref, lb_ref, scratch[:N_MIXER_SCRATCH])
    gla = _gla_mixer(cq_ref, ck_ref, cv_ref, lr_ref, waf_ref, wab_ref, baf_ref, bab_ref,
                     scratch[N_MIXER_SCRATCH:])
    _scan_mixers(aq_ref.shape[0] // CHUNK, [hgrn2, gla])
    _finish(hgrn2, again_ref, aog_ref, ya_ref, center=False, swish=False)
    _finish(gla, cgain_ref, csg_ref, yc_ref, center=False, swish=True)


def _mixer_b_kernel(q_ref, k_ref, v_ref, sg_ref, cos_ref, sin_ref, lgf_ref, lgb_ref, gain_ref, y_ref, *scratch):
    retention = _retention_mixer(q_ref, k_ref, v_ref, cos_ref, sin_ref, lgf_ref, lgb_ref, scratch)
    _scan_mixers(q_ref.shape[0] // CHUNK, [retention])
    _finish(retention, gain_ref, sg_ref, y_ref, center=True, swish=True)


def _zspec(seqlen, width, col0):
    blk = col0 // width
    return pl.BlockSpec((seqlen, width), lambda b, g: (b, blk + g))


def _vspec(width, blk0=0):
    return pl.BlockSpec((1, width), lambda b, g: (0, blk0 + g))


def _mixer_params():
    return pltpu.CompilerParams(dimension_semantics=("parallel", "parallel"), vmem_limit_bytes=VMEM_LIMIT)


def _mixers_ac(zb, zf, lb, gain_a, wal, bal, gain_c, bsz, seqlen):
    t, w = seqlen, 2 * DV
    n_steps = N_HEADS * DV // w
    half = N_HEADS * BC_DK // LANES
    out = pl.BlockSpec((t, w), lambda b, g: (b, g))
    return pl.pallas_call(
        _mixers_ac_kernel,
        grid=(bsz, n_steps),
        in_specs=[_zspec(t, w, ZB_AQ), _zspec(t, w, ZB_AI), _zspec(t, w, ZB_AG),
                  _zspec(t, w, ZF_FF), _zspec(t, w, ZF_FB), _vspec(w), _vspec(w),
                  _zspec(t, LANES, ZB_CQ), _zspec(t, LANES, ZB_CK), _zspec(t, w, ZB_CV), _zspec(t, w, ZB_CG),
                  pl.BlockSpec((t, LANES), lambda b, g: (b, ZF_LR // LANES)),
                  pl.BlockSpec((LANES, LANES), lambda b, g: (0, g)),
                  pl.BlockSpec((LANES, LANES), lambda b, g: (0, half + g)),
                  _vspec(LANES), _vspec(LANES, half), _vspec(w)],
        out_specs=[out, out],
        out_shape=[jax.ShapeDtypeStruct((bsz * t, n_steps * w), BF16)] * 2,
        scratch_shapes=_mixer_scratch(t, 2, 1, True) + _mixer_scratch(t, 1, 2, True),
        compiler_params=_mixer_params(),
        name="mixers_hgrn2_gla",
    )(zb, zb, zb, zf, zf, lb, gain_a, zb, zb, zb, zb, zf, wal, wal, bal, bal, gain_c)


def _mixer_b(zb, cos, sin, lgf, lgb, gain, bsz, seqlen):
    t, w, wv = seqlen, N_HEADS * BC_DK, N_HEADS * DV
    table = pl.BlockSpec((t, LANES), lambda b, g: (0, 0))
    return pl.pallas_call(
        _mixer_b_kernel,
        grid=(bsz, 1),
        in_specs=[_zspec(t, w, ZB_BQ), _zspec(t, w, ZB_BK), _zspec(t, wv, ZB_BV), _zspec(t, wv, ZB_BG),
                  table, table, _vspec(w), _vspec(w), _vspec(wv)],
        out_specs=pl.BlockSpec((t, wv), lambda b, g: (b, g)),
        out_shape=jax.ShapeDtypeStruct((bsz * t, wv), BF16),
        scratch_shapes=_mixer_scratch(t, w // LANES, 2, False),
        compiler_params=_mixer_params(),
        name="mixer_retention",
    )(zb, zb, zb, zb, cos, sin, lgf, lgb, gain)


def _split_w_in(w):
    o = {}
    off = 0
    for name, size in (("a_q", 512), ("a_ff", 512), ("a_fb", 512), ("a_i", 512), ("a_g", 512),
                       ("b_q", 256), ("b_k", 256), ("b_v", 512), ("b_g", 512),
                       ("c_q", 256), ("c_k", 256), ("c_v", 512), ("c_g", 512), ("c_lr", 32),
                       ("gate_a", 1024), ("gate_b", 1024), ("gate_c", 1024)):
        o[name] = w[:, off:off + size]
        off += size
    wb = jnp.concatenate([o[n] for n in ("gate_a", "gate_b", "gate_c", "a_q", "a_i", "a_g",
                                         "b_q", "b_k", "b_v", "b_g", "c_q", "c_k", "c_v", "c_g")], axis=1)
    pad = jnp.zeros((w.shape[0], ZF_COLS - ZF_LR - 2 * GLA_RANK), w.dtype)
    wf = jnp.concatenate([o["a_ff"], o["a_fb"], o["c_lr"], pad], axis=1)
    return wb.astype(BF16), wf.astype(BF16)


def _rotary_tables(seqlen):
    pos = jnp.arange(seqlen, dtype=F32)
    inv_freq = 10000.0 ** (-jnp.arange(0, BC_DK, 2, dtype=F32) / BC_DK)
    ang = pos[:, None] * inv_freq[None, :]
    cos, sin = jnp.cos(ang), jnp.sin(ang)
    reps = LANES // BC_DK
    return (jnp.tile(jnp.concatenate([cos, cos], axis=1), (1, reps)),
            jnp.tile(jnp.concatenate([-sin, sin], axis=1), (1, reps)))


def _retention_log_decays():
    h = jnp.arange(N_HEADS, dtype=F32)
    fwd = jnp.log1p(-jnp.exp2(-5.0 - h))
    bwd = jnp.log1p(-jnp.exp2(-5.0 - h[::-1]))
    spread = lambda v: jnp.repeat(v, BC_DK)[None, :]
    return spread(fwd), spread(bwd)


def kernel(x, c, norm1_g, w_ada, b_ada, w_in, lb_logits, norm_a_g, norm_b_g, norm_c_g, w_alpha, b_alpha,
           w_pa, w_pb, w_pc, w_out, norm2_g, w_ffn_in, w_ffn_out, norm_f_g):
    bsz, seqlen, d = x.shape
    depth = w_in.shape[0]
    assert d == D_MODEL and seqlen % 256 == 0
    tm = min(512, seqlen)
    tm_in = min(1024, seqlen)

    mod = _modulation(c, w_ada, b_ada)
    lbs = _lower_bounds(lb_logits)
    cos, sin = _rotary_tables(seqlen)
    lgf, lgb = _retention_log_decays()

    x2 = x.reshape(bsz * seqlen, d)
    for l in range(depth):
        sh1, sc1, g1, sh2, sc2, g2 = [mod[l, :, i * d:(i + 1) * d].reshape(bsz, 1, d) for i in range(6)]
        wb, wf = _split_w_in(w_in[l])
        gain1 = norm1_g[l].reshape(1, d)
        zb = _inproj(x2, sh1, sc1, gain1, wb, BF16, seqlen, tm_in, 1280, "inproj_main")
        zf = _inproj(x2, sh1, sc1, gain1, wf, F32, seqlen, tm_in, ZF_COLS, "inproj_gates")

        wal = jnp.zeros((LANES, 2 * N_HEADS * BC_DK), F32)
        wal = wal.at[0:GLA_RANK, 0:N_HEADS * BC_DK].set(w_alpha[l, 0])
        wal = wal.at[GLA_RANK:2 * GLA_RANK, N_HEADS * BC_DK:].set(w_alpha[l, 1])
        bal = b_alpha[l].reshape(1, 2 * N_HEADS * BC_DK)

        ya, yc = _mixers_ac(zb, zf, lbs[l].reshape(1, -1), norm_a_g[l].reshape(1, -1),
                            wal.astype(BF16), bal, norm_c_g[l].reshape(1, -1), bsz, seqlen)
        yb = _mixer_b(zb, cos, sin, lgf, lgb, norm_b_g[l].reshape(1, -1), bsz, seqlen)

        x2 = _merge(x2, ya, yb, yc, zb, w_pa[l].astype(BF16), w_pb[l].astype(BF16), w_pc[l].astype(BF16),
                    w_out[l].astype(BF16), g1, seqlen, tm)
        x2 = _ffn(x2, sh2, sc2, g2, norm2_g[l].reshape(1, d), w_ffn_in[l].astype(BF16),
                  w_ffn_out[l].astype(BF16), norm_f_g.reshape(1, d), seqlen, tm, D_FF // 2,
                  final_norm=(l == depth - 1))
    return x2.reshape(bsz, seqlen, d)
```

```python
import functools
from typing import Any, Callable, NamedTuple

import jax
import jax.numpy as jnp
from jax import lax
from jax.experimental import pallas as pl
from jax.experimental.pallas import tpu as pltpu

F32 = jnp.float32
BF16 = jnp.bfloat16

D_MODEL = 1024
N_HEADS = 4
A_DK = 128
BC_DK = 64
DV = 128
GLA_RANK = 16
GLA_TAU = 16.0
D_FF = 2816
EPS = 1e-6
TINY = 1e-30
LOG2_E = 1.4426950408889634

LANES = 128
CHUNK = 128
VMEM_LIMIT = 48 * 1024 * 1024

ZB_GATE = 0
ZB_AQ, ZB_AI, ZB_AG = 3072, 3584, 4096
ZB_BQ, ZB_BK, ZB_BV, ZB_BG = 4608, 4864, 5120, 5632
ZB_CQ, ZB_CK, ZB_CV, ZB_CG = 6144, 6400, 6656, 7168
ZB_COLS = 7680
ZF_FF, ZF_FB, ZF_LR = 0, 512, 1024
ZF_COLS = 1152


def _dot(a, b):
    return jnp.dot(a, b, preferred_element_type=F32)


def _dot_nt(a, b):
    return lax.dot_general(a, b, (((1,), (1,)), ((), ())), preferred_element_type=F32)


def _dot_tn(a, b):
    return lax.dot_general(a, b, (((0,), (0,)), ((), ())), preferred_element_type=F32)


def _sigmoid(x):
    return jax.nn.sigmoid(x)


def _mod_kernel(c_ref, w_ref, b_ref, o_ref):
    c = c_ref[...]
    a = c * _sigmoid(c)
    w = w_ref[...]
    a_hi = a.astype(BF16)
    a_lo = (a - a_hi.astype(F32)).astype(BF16)
    w_hi = w.astype(BF16)
    w_lo = (w - w_hi.astype(F32)).astype(BF16)
    o_ref[...] = _dot(a_hi, w_hi) + _dot(a_hi, w_lo) + _dot(a_lo, w_hi) + b_ref[...]


def _modulation(c, w_ada, b_ada):
    depth, d, n6 = w_ada.shape
    bsz = c.shape[0]
    return pl.pallas_call(
        _mod_kernel,
        grid=(depth, n6 // d),
        in_specs=[
            pl.BlockSpec((bsz, d), lambda l, j: (0, 0)),
            pl.BlockSpec((None, d, d), lambda l, j: (l, 0, j)),
            pl.BlockSpec((None, 1, d), lambda l, j: (l, 0, j)),
        ],
        out_specs=pl.BlockSpec((None, bsz, d), lambda l, j: (l, 0, j)),
        out_shape=jax.ShapeDtypeStruct((depth, bsz, n6), F32),
        name="adaln_modulation",
    )(c, w_ada, b_ada.reshape(depth, 1, n6))


def _lb_kernel(x_ref, o_ref):
    depth = x_ref.shape[0]
    rows = [x_ref[i:i + 1, :] for i in range(depth)]
    m = rows[0]
    for r in rows[1:]:
        m = jnp.maximum(m, r)
    e = [jnp.exp(r - m) for r in rows]
    s = e[0]
    for t in e[1:]:
        s = s + t
    p = [t / s for t in e]
    acc = p[0]
    o_ref[0:1, :] = jnp.maximum(acc - p[0], 0.0)
    for i in range(1, depth):
        acc = acc + p[i]
        o_ref[i:i + 1, :] = jnp.maximum(acc - p[0], 0.0)


def _lower_bounds(lb_logits):
    return pl.pallas_call(
        _lb_kernel,
        out_shape=jax.ShapeDtypeStruct(lb_logits.shape, F32),
        name="hgrn2_lower_bounds",
    )(lb_logits)


def _norm_mod(x, gain, sc, sh):
    y = x * lax.rsqrt(jnp.mean(x * x, axis=-1, keepdims=True) + EPS)
    return (y * gain) * (1.0 + sc) + sh


def _inproj_kernel(x_ref, sh_ref, sc_ref, g_ref, w_ref, o_ref, h_ref):
    @pl.when(pl.program_id(1) == 0)
    def _():
        h_ref[...] = _norm_mod(x_ref[...], g_ref[...], sc_ref[0], sh_ref[0]).astype(BF16)

    o_ref[...] = _dot(h_ref[...], w_ref[...]).astype(o_ref.dtype)


def _inproj(x2, sh, sc, gain, w, out_dtype, seqlen, tm, tn, name):
    m, d = x2.shape
    n = w.shape[1]
    per = seqlen // tm
    return pl.pallas_call(
        _inproj_kernel,
        grid=(m // tm, n // tn),
        in_specs=[
            pl.BlockSpec((tm, d), lambda i, j: (i, 0)),
            pl.BlockSpec((1, 1, d), lambda i, j: (i // per, 0, 0)),
            pl.BlockSpec((1, 1, d), lambda i, j: (i // per, 0, 0)),
            pl.BlockSpec((1, d), lambda i, j: (0, 0)),
            pl.BlockSpec((d, tn), lambda i, j: (0, j)),
        ],
        out_specs=pl.BlockSpec((tm, tn), lambda i, j: (i, j)),
        out_shape=jax.ShapeDtypeStruct((m, n), out_dtype),
        scratch_shapes=[pltpu.VMEM((tm, d), BF16)],
        compiler_params=pltpu.CompilerParams(
            dimension_semantics=("parallel", "arbitrary"), vmem_limit_bytes=VMEM_LIMIT),
        name=name,
    )(x2, sh, sc, gain, w)


def _merge_kernel(x_ref, ya_ref, yb_ref, yc_ref, ga_ref, gb_ref, gc_ref,
                  wpa_ref, wpb_ref, wpc_ref, wout_ref, g1_ref, o_ref):
    merged = (_sigmoid(ga_ref[...].astype(F32)) * _dot(ya_ref[...], wpa_ref[...])
              + _sigmoid(gb_ref[...].astype(F32)) * _dot(yb_ref[...], wpb_ref[...])
              + _sigmoid(gc_ref[...].astype(F32)) * _dot(yc_ref[...], wpc_ref[...]))
    out = _dot(merged.astype(BF16), wout_ref[...])
    o_ref[...] = x_ref[...] + g1_ref[0] * out


def _merge(x2, ya, yb, yc, zb, wpa, wpb, wpc, wout, g1, seqlen, tm):
    m, d = x2.shape
    w = ya.shape[1]
    per = seqlen // tm
    row = lambda i: (i, 0)
    const = lambda i: (0, 0)
    return pl.pallas_call(
        _merge_kernel,
        grid=(m // tm,),
        in_specs=[
            pl.BlockSpec((tm, d), row),
            pl.BlockSpec((tm, w), row), pl.BlockSpec((tm, w), row), pl.BlockSpec((tm, w), row),
            pl.BlockSpec((tm, d), lambda i: (i, ZB_GATE // d)),
            pl.BlockSpec((tm, d), lambda i: (i, ZB_GATE // d + 1)),
            pl.BlockSpec((tm, d), lambda i: (i, ZB_GATE // d + 2)),
            pl.BlockSpec((w, d), const), pl.BlockSpec((w, d), const), pl.BlockSpec((w, d), const),
            pl.BlockSpec((d, d), const),
            pl.BlockSpec((1, 1, d), lambda i: (i // per, 0, 0)),
        ],
        out_specs=pl.BlockSpec((tm, d), row),
        out_shape=jax.ShapeDtypeStruct((m, d), F32),
        compiler_params=pltpu.CompilerParams(
            dimension_semantics=("parallel",), vmem_limit_bytes=VMEM_LIMIT),
        name="merge_outproj",
    )(x2, ya, yb, yc, zb, zb, zb, wpa, wpb, wpc, wout, g1)


def _ffn_kernel(x_ref, sh_ref, sc_ref, g2_ref, ng_ref, wg_ref, wu_ref, wo_ref, nf_ref,
                o_ref, h_ref, acc_ref, *, final_norm):
    k = pl.program_id(1)

    @pl.when(k == 0)
    def _():
        h_ref[...] = _norm_mod(x_ref[...], ng_ref[...], sc_ref[0], sh_ref[0]).astype(BF16)
        acc_ref[...] = jnp.zeros_like(acc_ref)

    h = h_ref[...]
    gate = _dot(h, wg_ref[...])
    up = _dot(h, wu_ref[...])
    act = (gate * _sigmoid(gate) * up).astype(BF16)
    acc_ref[...] += _dot(act, wo_ref[...])

    @pl.when(k == pl.num_programs(1) - 1)
    def _():
        xn = x_ref[...] + g2_ref[0] * acc_ref[...]
        if final_norm:
            xn = xn * lax.rsqrt(jnp.mean(xn * xn, axis=-1, keepdims=True) + EPS) * nf_ref[...]
        o_ref[...] = xn


def _ffn(x2, sh, sc, g2, ng, w_in, w_out, nf, seqlen, tm, tf, final_norm):
    m, d = x2.shape
    dff = w_out.shape[0]
    nk = dff // tf
    per = seqlen // tm
    bvec = lambda i, k: (i // per, 0, 0)
    return pl.pallas_call(
        functools.partial(_ffn_kernel, final_norm=final_norm),
        grid=(m // tm, nk),
        in_specs=[
            pl.BlockSpec((tm, d), lambda i, k: (i, 0)),
            pl.BlockSpec((1, 1, d), bvec), pl.BlockSpec((1, 1, d), bvec), pl.BlockSpec((1, 1, d), bvec),
            pl.BlockSpec((1, d), lambda i, k: (0, 0)),
            pl.BlockSpec((d, tf), lambda i, k: (0, k)),
            pl.BlockSpec((d, tf), lambda i, k: (0, nk + k)),
            pl.BlockSpec((tf, d), lambda i, k: (k, 0)),
            pl.BlockSpec((1, d), lambda i, k: (0, 0)),
        ],
        out_specs=pl.BlockSpec((tm, d), lambda i, k: (i, 0)),
        out_shape=jax.ShapeDtypeStruct((m, d), F32),
        scratch_shapes=[pltpu.VMEM((tm, d), BF16), pltpu.VMEM((tm, d), F32)],
        compiler_params=pltpu.CompilerParams(
            dimension_semantics=("parallel", "arbitrary"), vmem_limit_bytes=VMEM_LIMIT),
        name="swiglu_ffn",
    )(x2, sh, sc, g2, ng, w_in, w_in, w_out, nf)


N_LEVELS = CHUNK.bit_length() - 1


def _block_sums(g):
    row = lax.broadcasted_iota(jnp.int32, g.shape, 0)
    lb, tb = g, g
    out = [(lb, tb)]
    b = 1
    while b < CHUNK:
        right = (row & b) != 0
        from_left = pltpu.roll(tb, b, axis=0)
        from_right = pltpu.roll(tb, CHUNK - b, axis=0)
        lb = lb + jnp.where(right, from_left, 0.0)
        tb = tb + jnp.where(right, from_left, from_right)
        out.append((lb, tb))
        b *= 2
    return out


def _gated_factors(g2, forward):
    sums = _block_sums(g2)
    facs = [(jnp.exp2(g2), None)]
    for lb, tb in sums[1:]:
        if forward:
            facs.append((jnp.exp2(lb), jnp.exp2(tb - lb)))
        else:
            facs.append((jnp.exp2((tb - lb) + g2), jnp.exp2(lb - g2)))
    return facs, jnp.exp2(sums[-1][1][0:1, :])


def _fixed_factors(log_decay, forward):
    pos = lax.broadcasted_iota(jnp.int32, (CHUNK, LANES), 0).astype(F32)
    mid = CHUNK // 2
    if forward:
        facs = [(jnp.exp((pos - (mid - 1.0)) * log_decay), jnp.exp(((mid - 1.0) - pos) * log_decay)),
                (jnp.exp((pos + 1.0) * log_decay), jnp.exp((CHUNK - 1.0 - pos) * log_decay))]
    else:
        facs = [(jnp.exp((mid - pos) * log_decay), jnp.exp((pos - mid) * log_decay)),
                (jnp.exp((CHUNK - pos) * log_decay), jnp.exp(pos * log_decay))]
    return facs, jnp.exp(float(CHUNK) * log_decay)


def _store_masks(mask_ref, hierarchical):
    row = lax.broadcasted_iota(jnp.int32, (CHUNK, CHUNK), 0)
    col = lax.broadcasted_iota(jnp.int32, (CHUNK, CHUNK), 1)
    one = lambda m: jnp.where(m, 1.0, 0.0).astype(BF16)
    mask_ref[0] = one(row == col)
    n = N_LEVELS if hierarchical else 1
    for level in range(n):
        same_level = ((row ^ col) >> level) == 1 if hierarchical else (row != col)
        mask_ref[1 + level] = one(same_level & (row > col))
        mask_ref[1 + n + level] = one(same_level & (col > row))
    return n


def _lane_masks(n_sub):
    if n_sub == 1:
        return [None]
    lane = lax.broadcasted_iota(jnp.int32, (1, LANES), 1)
    width = LANES // n_sub
    return [jnp.where((lane >= i * width) & (lane < (i + 1) * width), 1.0, 0.0) for i in range(n_sub)]


def _chunk_rows(n):
    if isinstance(n, int):
        return pl.ds(n * CHUNK, CHUNK)
    return pl.ds(pl.multiple_of(n * CHUNK, CHUNK), CHUNK)


class _Mixer(NamedTuple):
    n_groups: int
    n_sub: int
    hierarchical: bool
    prep: Callable
    load_v: Callable
    of_ref: Any
    ob_ref: Any
    st_ref: Any
    mask_ref: Any
    slots: tuple


def _scan_mixers(n_chunks, mixers):
    levels = []
    for m in mixers:
        m.st_ref[...] = jnp.zeros_like(m.st_ref)
        levels.append(_store_masks(m.mask_ref, m.hierarchical))

    def rows_of(n, forward):
        return _chunk_rows(n if forward else n_chunks - 1 - n)

    def chains(m):
        return enumerate((gi, forward) for gi in range(m.n_groups) for forward in (True, False))

    def stage(n, slot):
        for m in mixers:
            qt_ref, kt_ref, p0_ref, dec_ref = m.slots[slot]
            lane_masks = _lane_masks(m.n_sub)
            for c, (gi, forward) in chains(m):
                q, k, (facs, chunk_decay) = m.prep(gi, forward, rows_of(n, forward))
                qb, kb = q.astype(BF16), k.astype(BF16)
                for level, (fq, fk) in enumerate(facs):
                    qt_ref[c, level] = qb * fq.astype(BF16)
                    if m.hierarchical and level < len(facs) - 1:
                        kt_ref[c, level] = (k if fk is None else k * fk).T.astype(BF16)
                    else:
                        kt_ref[c, level] = kb * fk.astype(BF16)
                dec_ref[c] = chunk_decay
                if forward:
                    for i in range(m.n_sub):
                        qm = q if lane_masks[i] is None else q * lane_masks[i]
                        diag = jnp.sum(qm * k, axis=-1, keepdims=True).astype(BF16)
                        p0_ref[gi * m.n_sub + i] = m.mask_ref[0] * diag

    def consume(n, slot):
        for m, n_levels in zip(mixers, levels):
            qt_ref, kt_ref, p0_ref, dec_ref = m.slots[slot]
            lane_masks = _lane_masks(m.n_sub)
            for c, (gi, forward) in chains(m):
                rows = rows_of(n, forward)
                o_ref = m.of_ref if forward else m.ob_ref
                mask0 = 1 if forward else 1 + n_levels
                for i, v in enumerate(m.load_v(gi, rows)):
                    head = gi * m.n_sub + i
                    sub = (lambda t: t) if lane_masks[i] is None else (lambda t, lm=lane_masks[i].astype(BF16): t * lm)
                    p = p0_ref[head] if forward else None
                    for level in range(n_levels):
                        scores = _dot if m.hierarchical else _dot_nt
                        s = scores(sub(qt_ref[c, level]), kt_ref[c, level]).astype(BF16)
                        s = s * m.mask_ref[mask0 + level]
                        p = s if p is None else p + s
                    st = m.st_ref[c * m.n_sub + i]
                    inter = _dot_nt(sub(qt_ref[c, n_levels]), st.astype(BF16))
                    o_ref[rows, head * DV:(head + 1) * DV] = _dot(p, v) + inter
                    m.st_ref[c * m.n_sub + i] = st * dec_ref[c] + _dot_tn(v, kt_ref[c, n_levels])

    def body(step, carry):
        n = 2 * step
        consume(n, 0)
        stage(n + 1, 1)
        consume(n + 1, 1)
        stage(jnp.minimum(n + 2, n_chunks - 1), 0)
        return carry

    stage(0, 0)
    lax.fori_loop(0, n_chunks // 2, body, 0)


def _finish(mixer, gain_ref, gate_ref, y_ref, center, swish):
    rows_per = 256
    n_blocks = mixer.of_ref.shape[0] // rows_per

    def body(b, carry):
        rows = pl.ds(pl.multiple_of(b * rows_per, rows_per), rows_per)
        for i in range(mixer.n_groups * mixer.n_sub):
            cols = slice(i * DV, (i + 1) * DV)
            o = mixer.of_ref[rows, cols] + mixer.ob_ref[rows, cols]
            if center:
                o = o - jnp.mean(o, axis=-1, keepdims=True)
            o = o * lax.rsqrt(jnp.mean(o * o, axis=-1, keepdims=True) + EPS)
            gz = gate_ref[rows, cols].astype(F32)
            act = gz * _sigmoid(gz) if swish else _sigmoid(gz)
            y_ref[rows, cols] = ((o * gain_ref[:, cols]) * act).astype(y_ref.dtype)
        return carry

    lax.fori_loop(0, n_blocks, body, 0)


def _group_cols(gi):
    return slice(gi * LANES, (gi + 1) * LANES)


N_SLOTS = 2
N_MIXER_SCRATCH = 4 + 4 * N_SLOTS


def _mixer(n_groups, n_sub, hierarchical, prep, load_v, scratch):
    of_ref, ob_ref, st_ref, mask_ref = scratch[:4]
    return _Mixer(n_groups, n_sub, hierarchical, prep, load_v, of_ref, ob_ref, st_ref, mask_ref,
                  tuple(scratch[4 + 4 * s:8 + 4 * s] for s in range(N_SLOTS)))


def _mixer_scratch(seqlen, n_groups, n_sub, hierarchical):
    n_heads, n_chains = n_groups * n_sub, 2 * n_groups
    n_levels = N_LEVELS if hierarchical else 1
    slot = [pltpu.VMEM((n_chains, n_levels + 1, CHUNK, LANES), BF16),
            pltpu.VMEM((n_chains, n_levels + 1, CHUNK, LANES), BF16),
            pltpu.VMEM((n_heads, CHUNK, CHUNK), BF16),
            pltpu.VMEM((n_chains, 1, LANES), F32)]
    return [pltpu.VMEM((seqlen, n_heads * DV), F32),
            pltpu.VMEM((seqlen, n_heads * DV), F32),
            pltpu.VMEM((2 * n_heads, DV, LANES), F32),
            pltpu.VMEM((1 + 2 * n_levels, CHUNK, CHUNK), BF16)] + N_SLOTS * slot


def _hgrn2_mixer(q_ref, v_ref, ff_ref, fb_ref, lb_ref, scratch):
    def prep(gi, forward, rows):
        cols = _group_cols(gi)
        lb = lb_ref[:, cols]
        z = (ff_ref if forward else fb_ref)[rows, cols]
        f = lb + (1.0 - lb) * _sigmoid(z)
        g2 = jnp.log2(jnp.maximum(f, TINY))
        key = 1.0 - f
        aq = q_ref[rows, cols].astype(F32)
        return aq * _sigmoid(aq), key, _gated_factors(g2, forward)

    return _mixer(q_ref.shape[1] // LANES, 1, True, prep,
                  lambda gi, rows: [v_ref[rows, _group_cols(gi)]], scratch)


def _rotate_half(x):
    lane = lax.broadcasted_iota(jnp.int32, x.shape, 1)
    half = BC_DK // 2
    first = (lane % BC_DK) < half
    return jnp.where(first, pltpu.roll(x, LANES - half, axis=1), pltpu.roll(x, half, axis=1))


def _sub_values(v_ref, gi, rows):
    return [v_ref[rows, (2 * gi + i) * DV:(2 * gi + i + 1) * DV] for i in range(2)]


def _retention_mixer(q_ref, k_ref, v_ref, cos_ref, sin_ref, lgf_ref, lgb_ref, scratch):
    n_groups = q_ref.shape[1] // LANES
    packs = [(_fixed_factors(lgb_ref[:, _group_cols(gi)], False), _fixed_factors(lgf_ref[:, _group_cols(gi)], True))
             for gi in range(n_groups)]

    def prep(gi, forward, rows):
        cols = _group_cols(gi)
        cos = cos_ref[rows, :]
        sin = sin_ref[rows, :]
        q = q_ref[rows, cols].astype(F32)
        k = k_ref[rows, cols].astype(F32)
        q = q * cos + _rotate_half(q) * sin
        k = (k * cos + _rotate_half(k) * sin) * (BC_DK ** -0.5)
        return q, k, packs[gi][1 if forward else 0]

    return _mixer(n_groups, 2, False, prep, functools.partial(_sub_values, v_ref), scratch)


def _gla_mixer(q_ref, k_ref, v_ref, lr_ref, waf_ref, wab_ref, baf_ref, bab_ref, scratch):
    def prep(gi, forward, rows):
        cols = _group_cols(gi)
        w_ref, b_ref = (waf_ref, baf_ref) if forward else (wab_ref, bab_ref)
        x = _dot(lr_ref[rows, :].astype(BF16), w_ref[:, cols]) + b_ref[:, cols]
        log_sig = jnp.minimum(x, 0.0) - jnp.log(1.0 + jnp.exp(-jnp.abs(x)))
        g2 = log_sig * (LOG2_E / GLA_TAU)
        q = q_ref[rows, cols].astype(F32) * (BC_DK ** -0.5)
        return q, k_ref[rows, cols].astype(F32), _gated_factors(g2, forward)

    return _mixer(q_ref.shape[1] // LANES, 2, True, prep, functools.partial(_sub_values, v_ref), scratch)


def _mixers_ac_kernel(aq_ref, av_ref, aog_ref, ff_ref, fb_ref, lb_ref, again_ref,
                      cq_ref, ck_ref, cv_ref, csg_ref, lr_ref, waf_ref, wab_ref, baf_ref, bab_ref, cgain_ref,
                      ya_ref, yc_ref, *scratch):
    hgrn2 = _hgrn2_mixer(aq_ref, av_ref, ff_ref, fb_ref, lb_ref, scratch[:N_MIXER_SCRATCH])
    gla = _gla_mixer(cq_ref, ck_ref, cv_ref, lr_ref, waf_ref, wab_ref, baf_ref, bab_ref,
                     scratch[N_MIXER_SCRATCH:])
    _scan_mixers(aq_ref.shape[0] // CHUNK, [hgrn2, gla])
    _finish(hgrn2, again_ref, aog_ref, ya_ref, center=False, swish=False)
    _finish(gla, cgain_ref, csg_ref, yc_ref, center=False, swish=True)


def _mixer_b_kernel(q_ref, k_ref, v_ref, sg_ref, cos_ref, sin_ref, lgf_ref, lgb_ref, gain_ref, y_ref, *scratch):
    retention = _retention_mixer(q_ref, k_ref, v_ref, cos_ref, sin_ref, lgf_ref, lgb_ref, scratch)
    _scan_mixers(q_ref.shape[0] // CHUNK, [retention])
    _finish(retention, gain_ref, sg_ref, y_ref, center=True, swish=True)


def _zspec(seqlen, width, col0):
    blk = col0 // width
    return pl.BlockSpec((seqlen, width), lambda b, g: (b, blk + g))


def _vspec(width, blk0=0):
    return pl.BlockSpec((1, width), lambda b, g: (0, blk0 + g))


def _mixer_params():
    return pltpu.CompilerParams(dimension_semantics=("parallel", "parallel"), vmem_limit_bytes=VMEM_LIMIT)


def _mixers_ac(zb, zf, lb, gain_a, wal, bal, gain_c, bsz, seqlen):
    t, w = seqlen, 2 * DV
    n_steps = N_HEADS * DV // w
    half = N_HEADS * BC_DK // LANES
    out = pl.BlockSpec((t, w), lambda b, g: (b, g))
    return pl.pallas_call(
        _mixers_ac_kernel,
        grid=(bsz, n_steps),
        in_specs=[_zspec(t, w, ZB_AQ), _zspec(t, w, ZB_AI), _zspec(t, w, ZB_AG),
                  _zspec(t, w, ZF_FF), _zspec(t, w, ZF_FB), _vspec(w), _vspec(w),
                  _zspec(t, LANES, ZB_CQ), _zspec(t, LANES, ZB_CK), _zspec(t, w, ZB_CV), _zspec(t, w, ZB_CG),
                  pl.BlockSpec((t, LANES), lambda b, g: (b, ZF_LR // LANES)),
                  pl.BlockSpec((LANES, LANES), lambda b, g: (0, g)),
                  pl.BlockSpec((LANES, LANES), lambda b, g: (0, half + g)),
                  _vspec(LANES), _vspec(LANES, half), _vspec(w)],
        out_specs=[out, out],
        out_shape=[jax.ShapeDtypeStruct((bsz * t, n_steps * w), BF16)] * 2,
        scratch_shapes=_mixer_scratch(t, 2, 1, True) + _mixer_scratch(t, 1, 2, True),
        compiler_params=_mixer_params(),
        name="mixers_hgrn2_gla",
    )(zb, zb, zb, zf, zf, lb, gain_a, zb, zb, zb, zb, zf, wal, wal, bal, bal, gain_c)


def _mixer_b(zb, cos, sin, lgf, lgb, gain, bsz, seqlen):
    t, w, wv = seqlen, N_HEADS * BC_DK, N_HEADS * DV
    table = pl.BlockSpec((t, LANES), lambda b, g: (0, 0))
    return pl.pallas_call(
        _mixer_b_kernel,
        grid=(bsz, 1),
        in_specs=[_zspec(t, w, ZB_BQ), _zspec(t, w, ZB_BK), _zspec(t, wv, ZB_BV), _zspec(t, wv, ZB_BG),
                  table, table, _vspec(w), _vspec(w), _vspec(wv)],
        out_specs=pl.BlockSpec((t, wv), lambda b, g: (b, g)),
        out_shape=jax.ShapeDtypeStruct((bsz * t, wv), BF16),
        scratch_shapes=_mixer_scratch(t, w // LANES, 2, False),
        compiler_params=_mixer_params(),
        name="mixer_retention",
    )(zb, zb, zb, zb, cos, sin, lgf, lgb, gain)


def _split_w_in(w):
    o = {}
    off = 0
    for name, size in (("a_q", 512), ("a_ff", 512), ("a_fb", 512), ("a_i", 512), ("a_g", 512),
                       ("b_q", 256), ("b_k", 256), ("b_v", 512), ("b_g", 512),
                       ("c_q", 256), ("c_k", 256), ("c_v", 512), ("c_g", 512), ("c_lr", 32),
                       ("gate_a", 1024), ("gate_b", 1024), ("gate_c", 1024)):
        o[name] = w[:, off:off + size]
        off += size
    wb = jnp.concatenate([o[n] for n in ("gate_a", "gate_b", "gate_c", "a_q", "a_i", "a_g",
                                         "b_q", "b_k", "b_v", "b_g", "c_q", "c_k", "c_v", "c_g")], axis=1)
    pad = jnp.zeros((w.shape[0], ZF_COLS - ZF_LR - 2 * GLA_RANK), w.dtype)
    wf = jnp.concatenate([o["a_ff"], o["a_fb"], o["c_lr"], pad], axis=1)
    return wb.astype(BF16), wf.astype(BF16)


def _rotary_tables(seqlen):
    pos = jnp.arange(seqlen, dtype=F32)
    inv_freq = 10000.0 ** (-jnp.arange(0, BC_DK, 2, dtype=F32) / BC_DK)
    ang = pos[:, None] * inv_freq[None, :]
    cos, sin = jnp.cos(ang), jnp.sin(ang)
    reps = LANES // BC_DK
    return (jnp.tile(jnp.concatenate([cos, cos], axis=1), (1, reps)),
            jnp.tile(jnp.concatenate([-sin, sin], axis=1), (1, reps)))


def _retention_log_decays():
    h = jnp.arange(N_HEADS, dtype=F32)
    fwd = jnp.log1p(-jnp.exp2(-5.0 - h))
    bwd = jnp.log1p(-jnp.exp2(-5.0 - h[::-1]))
    spread = lambda v: jnp.repeat(v, BC_DK)[None, :]
    return spread(fwd), spread(bwd)


def kernel(x, c, norm1_g, w_ada, b_ada, w_in, lb_logits, norm_a_g, norm_b_g, norm_c_g, w_alpha, b_alpha,
           w_pa, w_pb, w_pc, w_out, norm2_g, w_ffn_in, w_ffn_out, norm_f_g):
    bsz, seqlen, d = x.shape
    depth = w_in.shape[0]
    assert d == D_MODEL and seqlen % 256 == 0
    tm = min(512, seqlen)
    tm_in = min(1024, seqlen)

    mod = _modulation(c, w_ada, b_ada)
    lbs = _lower_bounds(lb_logits)
    cos, sin = _rotary_tables(seqlen)
    lgf, lgb = _retention_log_decays()

    x2 = x.reshape(bsz * seqlen, d)
    for l in range(depth):
        sh1, sc1, g1, sh2, sc2, g2 = [mod[l, :, i * d:(i + 1) * d].reshape(bsz, 1, d) for i in range(6)]
        wb, wf = _split_w_in(w_in[l])
        gain1 = norm1_g[l].reshape(1, d)
        zb = _inproj(x2, sh1, sc1, gain1, wb, BF16, seqlen, tm_in, 1280, "inproj_main")
        zf = _inproj(x2, sh1, sc1, gain1, wf, F32, seqlen, tm_in, ZF_COLS, "inproj_gates")

        wal = jnp.zeros((LANES, 2 * N_HEADS * BC_DK), F32)
        wal = wal.at[0:GLA_RANK, 0:N_HEADS * BC_DK].set(w_alpha[l, 0])
        wal = wal.at[GLA_RANK:2 * GLA_RANK, N_HEADS * BC_DK:].set(w_alpha[l, 1])
        bal = b_alpha[l].reshape(1, 2 * N_HEADS * BC_DK)

        ya, yc = _mixers_ac(zb, zf, lbs[l].reshape(1, -1), norm_a_g[l].reshape(1, -1),
                            wal.astype(BF16), bal, norm_c_g[l].reshape(1, -1), bsz, seqlen)
        yb = _mixer_b(zb, cos, sin, lgf, lgb, norm_b_g[l].reshape(1, -1), bsz, seqlen)

        x2 = _merge(x2, ya, yb, yc, zb, w_pa[l].astype(BF16), w_pb[l].astype(BF16), w_pc[l].astype(BF16),
                    w_out[l].astype(BF16), g1, seqlen, tm)
        x2 = _ffn(x2, sh2, sc2, g2, norm2_g[l].reshape(1, d), w_ffn_in[l].astype(BF16),
                  w_ffn_out[l].astype(BF16), norm_f_g.reshape(1, d), seqlen, tm, D_FF // 2,
                  final_norm=(l == depth - 1))
    return x2.reshape(bsz, seqlen, d)
```

```python
import functools
from typing import Any, Callable, NamedTuple

import jax
import jax.numpy as jnp
from jax import lax
from jax.experimental import pallas as pl
from jax.experimental.pallas import tpu as pltpu

F32 = jnp.float32
BF16 = jnp.bfloat16

D_MODEL = 1024
N_HEADS = 4
A_DK = 128
BC_DK = 64
DV = 128
GLA_RANK = 16
GLA_TAU = 16.0
D_FF = 2816
EPS = 1e-6
TINY = 1e-30
LOG2_E = 1.4426950408889634

LANES = 128
CHUNK = 128
assert CHUNK == LANES
VMEM_LIMIT = 48 * 1024 * 1024

TM_INPROJ, TN_INPROJ = 512, 1280
TM_DENSE = 512
TF_FFN = D_FF // 2

ZB_GATE = 0
ZB_AQ, ZB_AI, ZB_AG = 3072, 3584, 4096
ZB_BQ, ZB_BK, ZB_BV, ZB_BG = 4608, 4864, 5120, 5632
ZB_CQ, ZB_CK, ZB_CV, ZB_CG = 6144, 6400, 6656, 7168
ZB_COLS = 7680
ZF_FF, ZF_FB, ZF_LR = 0, 512, 1024
ZF_COLS = 1152


def _dot(a, b):
    return jnp.dot(a, b, preferred_element_type=F32)


def _dot_nt(a, b):
    return lax.dot_general(a, b, (((1,), (1,)), ((), ())), preferred_element_type=F32)


def _dot_tn(a, b):
    return lax.dot_general(a, b, (((0,), (0,)), ((), ())), preferred_element_type=F32)


def _sigmoid(x):
    return jax.nn.sigmoid(x)


def _mod_kernel(c_ref, w_ref, b_ref, o_ref):
    c = c_ref[...]
    a = c * _sigmoid(c)
    w = w_ref[...]
    a_hi = a.astype(BF16)
    a_lo = (a - a_hi.astype(F32)).astype(BF16)
    w_hi = w.astype(BF16)
    w_lo = (w - w_hi.astype(F32)).astype(BF16)
    o_ref[...] = _dot(a_hi, w_hi) + _dot(a_hi, w_lo) + _dot(a_lo, w_hi) + b_ref[...]


def _modulation(c, w_ada, b_ada):
    depth, d, n6 = w_ada.shape
    bsz = c.shape[0]
    return pl.pallas_call(
        _mod_kernel,
        grid=(depth, n6 // d),
        in_specs=[
            pl.BlockSpec((bsz, d), lambda l, j: (0, 0)),
            pl.BlockSpec((None, d, d), lambda l, j: (l, 0, j)),
            pl.BlockSpec((None, 1, d), lambda l, j: (l, 0, j)),
        ],
        out_specs=pl.BlockSpec((None, bsz, d), lambda l, j: (l, 0, j)),
        out_shape=jax.ShapeDtypeStruct((depth, bsz, n6), F32),
        name="adaln_modulation",
    )(c, w_ada, b_ada.reshape(depth, 1, n6))


def _lb_kernel(x_ref, o_ref):
    depth = x_ref.shape[0]
    rows = [x_ref[i:i + 1, :] for i in range(depth)]
    m = rows[0]
    for r in rows[1:]:
        m = jnp.maximum(m, r)
    e = [jnp.exp(r - m) for r in rows]
    s = e[0]
    for t in e[1:]:
        s = s + t
    p = [t / s for t in e]
    acc = p[0]
    o_ref[0:1, :] = jnp.maximum(acc - p[0], 0.0)
    for i in range(1, depth):
        acc = acc + p[i]
        o_ref[i:i + 1, :] = jnp.maximum(acc - p[0], 0.0)


def _lower_bounds(lb_logits):
    return pl.pallas_call(
        _lb_kernel,
        out_shape=jax.ShapeDtypeStruct(lb_logits.shape, F32),
        name="hgrn2_lower_bounds",
    )(lb_logits)


def _norm_mod(x, gain, sc, sh):
    y = x * lax.rsqrt(jnp.mean(x * x, axis=-1, keepdims=True) + EPS)
    return (y * gain) * (1.0 + sc) + sh


def _inproj_kernel(x_ref, sh_ref, sc_ref, g_ref, wb_ref, wf_ref, zb_ref, zf_ref, h_ref):
    h_ref[...] = _norm_mod(x_ref[...], g_ref[...], sc_ref[0], sh_ref[0]).astype(BF16)
    for j in range(zb_ref.shape[1] // TN_INPROJ):
        cols = slice(j * TN_INPROJ, (j + 1) * TN_INPROJ)
        zb_ref[:, cols] = _dot(h_ref[...], wb_ref[:, cols]).astype(zb_ref.dtype)
    zf_ref[...] = _dot(h_ref[...], wf_ref[...])


def _inproj(x2, sh, sc, gain, wb, wf, seqlen, tm):
    m, d = x2.shape
    per = seqlen // tm
    bvec = lambda i: (i // per, 0, 0)
    resident = lambda w: pl.BlockSpec(w.shape, lambda i: (0, 0), pipeline_mode=pl.Buffered(1))
    return pl.pallas_call(
        _inproj_kernel,
        grid=(m // tm,),
        in_specs=[
            pl.BlockSpec((tm, d), lambda i: (i, 0)),
            pl.BlockSpec((1, 1, d), bvec), pl.BlockSpec((1, 1, d), bvec),
            pl.BlockSpec((1, d), lambda i: (0, 0)),
            resident(wb), resident(wf),
        ],
        out_specs=[pl.BlockSpec((tm, wb.shape[1]), lambda i: (i, 0)),
                   pl.BlockSpec((tm, wf.shape[1]), lambda i: (i, 0))],
        out_shape=[jax.ShapeDtypeStruct((m, wb.shape[1]), BF16), jax.ShapeDtypeStruct((m, wf.shape[1]), F32)],
        scratch_shapes=[pltpu.VMEM((tm, d), BF16)],
        compiler_params=pltpu.CompilerParams(dimension_semantics=("parallel",), vmem_limit_bytes=VMEM_LIMIT),
        name="inproj",
    )(x2, sh, sc, gain, wb, wf)


def _merge_kernel(x_ref, ya_ref, yb_ref, yc_ref, ga_ref, gb_ref, gc_ref,
                  wpa_ref, wpb_ref, wpc_ref, wout_ref, g1_ref, o_ref):
    merged = (_sigmoid(ga_ref[...].astype(F32)) * _dot(ya_ref[...], wpa_ref[...])
              + _sigmoid(gb_ref[...].astype(F32)) * _dot(yb_ref[...], wpb_ref[...])
              + _sigmoid(gc_ref[...].astype(F32)) * _dot(yc_ref[...], wpc_ref[...]))
    out = _dot(merged.astype(BF16), wout_ref[...])
    o_ref[...] = x_ref[...] + g1_ref[0] * out


def _merge(x2, ya, yb, yc, zb, wpa, wpb, wpc, wout, g1, seqlen, tm):
    m, d = x2.shape
    w = ya.shape[1]
    per = seqlen // tm
    row = lambda i: (i, 0)
    const = lambda i: (0, 0)
    return pl.pallas_call(
        _merge_kernel,
        grid=(m // tm,),
        in_specs=[
            pl.BlockSpec((tm, d), row),
            pl.BlockSpec((tm, w), row), pl.BlockSpec((tm, w), row), pl.BlockSpec((tm, w), row),
            pl.BlockSpec((tm, d), lambda i: (i, ZB_GATE // d)),
            pl.BlockSpec((tm, d), lambda i: (i, ZB_GATE // d + 1)),
            pl.BlockSpec((tm, d), lambda i: (i, ZB_GATE // d + 2)),
            pl.BlockSpec((w, d), const), pl.BlockSpec((w, d), const), pl.BlockSpec((w, d), const),
            pl.BlockSpec((d, d), const),
            pl.BlockSpec((1, 1, d), lambda i: (i // per, 0, 0)),
        ],
        out_specs=pl.BlockSpec((tm, d), row),
        out_shape=jax.ShapeDtypeStruct((m, d), F32),
        compiler_params=pltpu.CompilerParams(
            dimension_semantics=("parallel",), vmem_limit_bytes=VMEM_LIMIT),
        name="merge_outproj",
    )(x2, ya, yb, yc, zb, zb, zb, wpa, wpb, wpc, wout, g1)


def _ffn_kernel(x_ref, sh_ref, sc_ref, g2_ref, ng_ref, wi_ref, wo_ref, nf_ref, o_ref, h_ref, *, final_norm):
    dff = wo_ref.shape[0]
    h_ref[...] = _norm_mod(x_ref[...], ng_ref[...], sc_ref[0], sh_ref[0]).astype(BF16)
    acc = None
    for k in range(dff // TF_FFN):
        gate = _dot(h_ref[...], wi_ref[:, k * TF_FFN:(k + 1) * TF_FFN])
        up = _dot(h_ref[...], wi_ref[:, dff + k * TF_FFN:dff + (k + 1) * TF_FFN])
        act = (gate * _sigmoid(gate) * up).astype(BF16)
        part = _dot(act, wo_ref[k * TF_FFN:(k + 1) * TF_FFN, :])
        acc = part if acc is None else acc + part
    xn = x_ref[...] + g2_ref[0] * acc
    if final_norm:
        xn = xn * lax.rsqrt(jnp.mean(xn * xn, axis=-1, keepdims=True) + EPS) * nf_ref[...]
    o_ref[...] = xn


def _ffn(x2, sh, sc, g2, ng, w_in, w_out, nf, seqlen, tm, final_norm):
    m, d = x2.shape
    per = seqlen // tm
    bvec = lambda i: (i // per, 0, 0)
    resident = lambda w: pl.BlockSpec(w.shape, lambda i: (0, 0), pipeline_mode=pl.Buffered(1))
    return pl.pallas_call(
        functools.partial(_ffn_kernel, final_norm=final_norm),
        grid=(m // tm,),
        in_specs=[
            pl.BlockSpec((tm, d), lambda i: (i, 0)),
            pl.BlockSpec((1, 1, d), bvec), pl.BlockSpec((1, 1, d), bvec), pl.BlockSpec((1, 1, d), bvec),
            pl.BlockSpec((1, d), lambda i: (0, 0)),
            resident(w_in), resident(w_out),
            pl.BlockSpec((1, d), lambda i: (0, 0)),
        ],
        out_specs=pl.BlockSpec((tm, d), lambda i: (i, 0)),
        out_shape=jax.ShapeDtypeStruct((m, d), F32),
        scratch_shapes=[pltpu.VMEM((tm, d), BF16)],
        compiler_params=pltpu.CompilerParams(dimension_semantics=("parallel",), vmem_limit_bytes=VMEM_LIMIT),
        name="swiglu_ffn",
    )(x2, sh, sc, g2, ng, w_in, w_out, nf)


N_LEVELS = CHUNK.bit_length() - 1


def _block_sums(g):
    row = lax.broadcasted_iota(jnp.int32, g.shape, 0)
    lb, tb = g, g
    out = [(lb, tb)]
    b = 1
    while b < CHUNK:
        right = (row & b) != 0
        from_left = pltpu.roll(tb, b, axis=0)
        from_right = pltpu.roll(tb, CHUNK - b, axis=0)
        lb = lb + jnp.where(right, from_left, 0.0)
        tb = tb + jnp.where(right, from_left, from_right)
        out.append((lb, tb))
        b *= 2
    return out


def _gated_factors(g2, forward):
    sums = _block_sums(g2)
    facs = [(jnp.exp2(g2), None)]
    for lb, tb in sums[1:]:
        if forward:
            facs.append((jnp.exp2(lb), jnp.exp2(tb - lb)))
        else:
            facs.append((jnp.exp2((tb - lb) + g2), jnp.exp2(lb - g2)))
    return facs, jnp.exp2(sums[-1][1][0:1, :])


def _fixed_factors(log_decay, forward):
    pos = lax.broadcasted_iota(jnp.int32, (CHUNK, LANES), 0).astype(F32)
    mid = CHUNK // 2
    if forward:
        facs = [(jnp.exp((pos - (mid - 1.0)) * log_decay), jnp.exp(((mid - 1.0) - pos) * log_decay)),
                (jnp.exp((pos + 1.0) * log_decay), jnp.exp((CHUNK - 1.0 - pos) * log_decay))]
    else:
        facs = [(jnp.exp((mid - pos) * log_decay), jnp.exp((pos - mid) * log_decay)),
                (jnp.exp((CHUNK - pos) * log_decay), jnp.exp(pos * log_decay))]
    return facs, jnp.exp(float(CHUNK) * log_decay)


def _store_masks(mask_ref, hierarchical):
    row = lax.broadcasted_iota(jnp.int32, (CHUNK, CHUNK), 0)
    col = lax.broadcasted_iota(jnp.int32, (CHUNK, CHUNK), 1)
    one = lambda m: jnp.where(m, 1.0, 0.0).astype(BF16)
    mask_ref[0] = one(row == col)
    n = N_LEVELS if hierarchical else 1
    for level in range(n):
        same_level = ((row ^ col) >> level) == 1 if hierarchical else (row != col)
        mask_ref[1 + level] = one(same_level & (row > col))
        mask_ref[1 + n + level] = one(same_level & (col > row))
    return n


def _lane_masks(n_sub):
    if n_sub == 1:
        return [None]
    lane = lax.broadcasted_iota(jnp.int32, (1, LANES), 1)
    width = LANES // n_sub
    return [jnp.where((lane >= i * width) & (lane < (i + 1) * width), 1.0, 0.0) for i in range(n_sub)]


def _chunk_rows(n):
    if isinstance(n, int):
        return pl.ds(n * CHUNK, CHUNK)
    return pl.ds(pl.multiple_of(n * CHUNK, CHUNK), CHUNK)


class _Mixer(NamedTuple):
    n_groups: int
    n_sub: int
    hierarchical: bool
    prep: Callable
    load_v: Callable
    of_ref: Any
    ob_ref: Any
    st_ref: Any
    mask_ref: Any
    slots: tuple


def _scan_mixers(n_chunks, mixers):
    levels = []
    for m in mixers:
        m.st_ref[...] = jnp.zeros_like(m.st_ref)
        levels.append(_store_masks(m.mask_ref, m.hierarchical))

    def rows_of(n, forward):
        return _chunk_rows(n if forward else n_chunks - 1 - n)

    def chains(m):
        return enumerate((gi, forward) for gi in range(m.n_groups) for forward in (True, False))

    def stage(n, slot):
        for m in mixers:
            qt_ref, kt_ref, p0_ref, dec_ref = m.slots[slot]
            lane_masks = _lane_masks(m.n_sub)
            for c, (gi, forward) in chains(m):
                q, k, (facs, chunk_decay) = m.prep(gi, forward, rows_of(n, forward))
                qb, kb = q.astype(BF16), k.astype(BF16)
                for level, (fq, fk) in enumerate(facs):
                    qt_ref[c, level] = qb * fq.astype(BF16)
                    if m.hierarchical and level < len(facs) - 1:
                        kt_ref[c, level] = (k if fk is None else k * fk).T.astype(BF16)
                    else:
                        kt_ref[c, level] = kb * fk.astype(BF16)
                dec_ref[c] = chunk_decay
                if forward:
                    for i in range(m.n_sub):
                        qm = q if lane_masks[i] is None else q * lane_masks[i]
                        diag = jnp.sum(qm * k, axis=-1, keepdims=True).astype(BF16)
                        p0_ref[gi * m.n_sub + i] = m.mask_ref[0] * diag

    def consume(n, slot):
        for m, n_levels in zip(mixers, levels):
            qt_ref, kt_ref, p0_ref, dec_ref = m.slots[slot]
            lane_masks = _lane_masks(m.n_sub)
            for c, (gi, forward) in chains(m):
                rows = rows_of(n, forward)
                o_ref = m.of_ref if forward else m.ob_ref
                mask0 = 1 if forward else 1 + n_levels
                for i, v in enumerate(m.load_v(gi, rows)):
                    head = gi * m.n_sub + i
                    sub = (lambda t: t) if lane_masks[i] is None else (lambda t, lm=lane_masks[i].astype(BF16): t * lm)
                    p = p0_ref[head] if forward else None
                    for level in range(n_levels):
                        scores = _dot if m.hierarchical else _dot_nt
                        s = scores(sub(qt_ref[c, level]), kt_ref[c, level]).astype(BF16)
                        s = s * m.mask_ref[mask0 + level]
                        p = s if p is None else p + s
                    st = m.st_ref[c * m.n_sub + i]
                    inter = _dot_nt(sub(qt_ref[c, n_levels]), st.astype(BF16))
                    o_ref[rows, head * DV:(head + 1) * DV] = _dot(p, v) + inter
                    m.st_ref[c * m.n_sub + i] = st * dec_ref[c] + _dot_tn(v, kt_ref[c, n_levels])

    def body(step, carry):
        n = 2 * step
        consume(n, 0)
        stage(n + 1, 1)
        consume(n + 1, 1)
        stage(jnp.minimum(n + 2, n_chunks - 1), 0)
        return carry

    stage(0, 0)
    lax.fori_loop(0, n_chunks // 2, body, 0)


def _finish(mixer, gain_ref, gate_ref, y_ref, center, swish):
    rows_per = 256
    n_blocks = mixer.of_ref.shape[0] // rows_per

    def body(b, carry):
        rows = pl.ds(pl.multiple_of(b * rows_per, rows_per), rows_per)
        for i in range(mixer.n_groups * mixer.n_sub):
            cols = slice(i * DV, (i + 1) * DV)
            o = mixer.of_ref[rows, cols] + mixer.ob_ref[rows, cols]
            if center:
                o = o - jnp.mean(o, axis=-1, keepdims=True)
            o = o * lax.rsqrt(jnp.mean(o * o, axis=-1, keepdims=True) + EPS)
            gz = gate_ref[rows, cols].astype(F32)
            act = gz * _sigmoid(gz) if swish else _sigmoid(gz)
            y_ref[rows, cols] = ((o * gain_ref[:, cols]) * act).astype(y_ref.dtype)
        return carry

    lax.fori_loop(0, n_blocks, body, 0)


def _group_cols(gi):
    return slice(gi * LANES, (gi + 1) * LANES)


N_SLOTS = 2
N_MIXER_SCRATCH = 4 + 4 * N_SLOTS


def _mixer(n_groups, n_sub, hierarchical, prep, load_v, scratch):
    of_ref, ob_ref, st_ref, mask_ref = scratch[:4]
    return _Mixer(n_groups, n_sub, hierarchical, prep, load_v, of_ref, ob_ref, st_ref, mask_ref,
                  tuple(scratch[4 + 4 * s:8 + 4 * s] for s in range(N_SLOTS)))


def _mixer_scratch(seqlen, n_groups, n_sub, hierarchical):
    n_heads, n_chains = n_groups * n_sub, 2 * n_groups
    n_levels = N_LEVELS if hierarchical else 1
    slot = [pltpu.VMEM((n_chains, n_levels + 1, CHUNK, LANES), BF16),
            pltpu.VMEM((n_chains, n_levels + 1, CHUNK, LANES), BF16),
            pltpu.VMEM((n_heads, CHUNK, CHUNK), BF16),
            pltpu.VMEM((n_chains, 1, LANES), F32)]
    return [pltpu.VMEM((seqlen, n_heads * DV), F32),
            pltpu.VMEM((seqlen, n_heads * DV), F32),
            pltpu.VMEM((2 * n_heads, DV, LANES), F32),
            pltpu.VMEM((1 + 2 * n_levels, CHUNK, CHUNK), BF16)] + N_SLOTS * slot


def _hgrn2_mixer(q_ref, v_ref, ff_ref, fb_ref, lb_ref, scratch):
    def prep(gi, forward, rows):
        cols = _group_cols(gi)
        lb = lb_ref[:, cols]
        z = (ff_ref if forward else fb_ref)[rows, cols]
        f = lb + (1.0 - lb) * _sigmoid(z)
        g2 = jnp.log2(jnp.maximum(f, TINY))
        key = 1.0 - f
        aq = q_ref[rows, cols].astype(F32)
        return aq * _sigmoid(aq), key, _gated_factors(g2, forward)

    return _mixer(q_ref.shape[1] // LANES, 1, True, prep,
                  lambda gi, rows: [v_ref[rows, _group_cols(gi)]], scratch)


def _rotate_half(x):
    lane = lax.broadcasted_iota(jnp.int32, x.shape, 1)
    half = BC_DK // 2
    first = (lane % BC_DK) < half
    return jnp.where(first, pltpu.roll(x, LANES - half, axis=1), pltpu.roll(x, half, axis=1))


def _sub_values(v_ref, gi, rows):
    return [v_ref[rows, (2 * gi + i) * DV:(2 * gi + i + 1) * DV] for i in range(2)]


def _retention_mixer(q_ref, k_ref, v_ref, cos_ref, sin_ref, lgf_ref, lgb_ref, scratch):
    n_groups = q_ref.shape[1] // LANES
    packs = [(_fixed_factors(lgb_ref[:, _group_cols(gi)], False), _fixed_factors(lgf_ref[:, _group_cols(gi)], True))
             for gi in range(n_groups)]

    def prep(gi, forward, rows):
        cols = _group_cols(gi)
        cos = cos_ref[rows, :]
        sin = sin_ref[rows, :]
        q = q_ref[rows, cols].astype(F32)
        k = k_ref[rows, cols].astype(F32)
        q = q * cos + _rotate_half(q) * sin
        k = (k * cos + _rotate_half(k) * sin) * (BC_DK ** -0.5)
        return q, k, packs[gi][1 if forward else 0]

    return _mixer(n_groups, 2, False, prep, functools.partial(_sub_values, v_ref), scratch)


def _gla_mixer(q_ref, k_ref, v_ref, lr_ref, waf_ref, wab_ref, baf_ref, bab_ref, scratch):
    def prep(gi, forward, rows):
        cols = _group_cols(gi)
        w_ref, b_ref = (waf_ref, baf_ref) if forward else (wab_ref, bab_ref)
        x = _dot(lr_ref[rows, :].astype(BF16), w_ref[:, cols]) + b_ref[:, cols]
        log_sig = jnp.minimum(x, 0.0) - jnp.log(1.0 + jnp.exp(-jnp.abs(x)))
        g2 = log_sig * (LOG2_E / GLA_TAU)
        q = q_ref[rows, cols].astype(F32) * (BC_DK ** -0.5)
        return q, k_ref[rows, cols].astype(F32), _gated_factors(g2, forward)

    return _mixer(q_ref.shape[1] // LANES, 2, True, prep, functools.partial(_sub_values, v_ref), scratch)


def _mixers_ac_kernel(aq_ref, av_ref, aog_ref, ff_ref, fb_ref, lb_ref, again_ref,
                      cq_ref, ck_ref, cv_ref, csg_ref, lr_ref, waf_ref, wab_ref, baf_ref, bab_ref, cgain_ref,
                      ya_ref, yc_ref, *scratch):
    hgrn2 = _hgrn2_mixer(aq_ref, av_ref, ff_ref, fb_ref, lb_ref, scratch[:N_MIXER_SCRATCH])
    gla = _gla_mixer(cq_ref, ck_ref, cv_ref, lr_ref, waf_ref, wab_ref, baf_ref, bab_ref,
                     scratch[N_MIXER_SCRATCH:])
    _scan_mixers(aq_ref.shape[0] // CHUNK, [hgrn2, gla])
    _finish(hgrn2, again_ref, aog_ref, ya_ref, center=False, swish=False)
    _finish(gla, cgain_ref, csg_ref, yc_ref, center=False, swish=True)


def _mixer_b_kernel(q_ref, k_ref, v_ref, sg_ref, cos_ref, sin_ref, lgf_ref, lgb_ref, gain_ref, y_ref, *scratch):
    retention = _retention_mixer(q_ref, k_ref, v_ref, cos_ref, sin_ref, lgf_ref, lgb_ref, scratch)
    _scan_mixers(q_ref.shape[0] // CHUNK, [retention])
    _finish(retention, gain_ref, sg_ref, y_ref, center=True, swish=True)


def _zspec(seqlen, width, col0):
    blk = col0 // width
    return pl.BlockSpec((seqlen, width), lambda b, g: (b, blk + g))


def _vspec(width, blk0=0):
    return pl.BlockSpec((1, width), lambda b, g: (0, blk0 + g))


def _mixer_params():
    return pltpu.CompilerParams(dimension_semantics=("parallel", "parallel"), vmem_limit_bytes=VMEM_LIMIT)


def _mixers_ac(zb, zf, lb, gain_a, wal, bal, gain_c, bsz, seqlen):
    t, w = seqlen, 2 * DV
    n_steps = N_HEADS * DV // w
    half = N_HEADS * BC_DK // LANES
    out = pl.BlockSpec((t, w), lambda b, g: (b, g))
    return pl.pallas_call(
        _mixers_ac_kernel,
        grid=(bsz, n_steps),
        in_specs=[_zspec(t, w, ZB_AQ), _zspec(t, w, ZB_AI), _zspec(t, w, ZB_AG),
                  _zspec(t, w, ZF_FF), _zspec(t, w, ZF_FB), _vspec(w), _vspec(w),
                  _zspec(t, LANES, ZB_CQ), _zspec(t, LANES, ZB_CK), _zspec(t, w, ZB_CV), _zspec(t, w, ZB_CG),
                  pl.BlockSpec((t, LANES), lambda b, g: (b, ZF_LR // LANES)),
                  pl.BlockSpec((LANES, LANES), lambda b, g: (0, g)),
                  pl.BlockSpec((LANES, LANES), lambda b, g: (0, half + g)),
                  _vspec(LANES), _vspec(LANES, half), _vspec(w)],
        out_specs=[out, out],
        out_shape=[jax.ShapeDtypeStruct((bsz * t, n_steps * w), BF16)] * 2,
        scratch_shapes=_mixer_scratch(t, 2, 1, True) + _mixer_scratch(t, 1, 2, True),
        compiler_params=_mixer_params(),
        name="mixers_hgrn2_gla",
    )(zb, zb, zb, zf, zf, lb, gain_a, zb, zb, zb, zb, zf, wal, wal, bal, bal, gain_c)


def _mixer_b(zb, cos, sin, lgf, lgb, gain, bsz, seqlen):
    t, w, wv = seqlen, N_HEADS * BC_DK, N_HEADS * DV
    table = pl.BlockSpec((t, LANES), lambda b, g: (0, 0))
    return pl.pallas_call(
        _mixer_b_kernel,
        grid=(bsz, 1),
        in_specs=[_zspec(t, w, ZB_BQ), _zspec(t, w, ZB_BK), _zspec(t, wv, ZB_BV), _zspec(t, wv, ZB_BG),
                  table, table, _vspec(w), _vspec(w), _vspec(wv)],
        out_specs=pl.BlockSpec((t, wv), lambda b, g: (b, g)),
        out_shape=jax.ShapeDtypeStruct((bsz * t, wv), BF16),
        scratch_shapes=_mixer_scratch(t, w // LANES, 2, False),
        compiler_params=_mixer_params(),
        name="mixer_retention",
    )(zb, zb, zb, zb, cos, sin, lgf, lgb, gain)


def _split_w_in(w):
    o = {}
    off = 0
    for name, size in (("a_q", 512), ("a_ff", 512), ("a_fb", 512), ("a_i", 512), ("a_g", 512),
                       ("b_q", 256), ("b_k", 256), ("b_v", 512), ("b_g", 512),
                       ("c_q", 256), ("c_k", 256), ("c_v", 512), ("c_g", 512), ("c_lr", 32),
                       ("gate_a", 1024), ("gate_b", 1024), ("gate_c", 1024)):
        o[name] = w[:, off:off + size]
        off += size
    wb = jnp.concatenate([o[n] for n in ("gate_a", "gate_b", "gate_c", "a_q", "a_i", "a_g",
                                         "b_q", "b_k", "b_v", "b_g", "c_q", "c_k", "c_v", "c_g")], axis=1)
    pad = jnp.zeros((w.shape[0], ZF_COLS - ZF_LR - 2 * GLA_RANK), w.dtype)
    wf = jnp.concatenate([o["a_ff"], o["a_fb"], o["c_lr"], pad], axis=1)
    return wb.astype(BF16), wf.astype(BF16)


def _rotary_tables(seqlen):
    pos = jnp.arange(seqlen, dtype=F32)
    inv_freq = 10000.0 ** (-jnp.arange(0, BC_DK, 2, dtype=F32) / BC_DK)
    ang = pos[:, None] * inv_freq[None, :]
    cos, sin = jnp.cos(ang), jnp.sin(ang)
    reps = LANES // BC_DK
    return (jnp.tile(jnp.concatenate([cos, cos], axis=1), (1, reps)),
            jnp.tile(jnp.concatenate([-sin, sin], axis=1), (1, reps)))


def _retention_log_decays():
    h = jnp.arange(N_HEADS, dtype=F32)
    fwd = jnp.log1p(-jnp.exp2(-5.0 - h))
    bwd = jnp.log1p(-jnp.exp2(-5.0 - h[::-1]))
    spread = lambda v: jnp.repeat(v, BC_DK)[None, :]
    return spread(fwd), spread(bwd)


def kernel(x, c, norm1_g, w_ada, b_ada, w_in, lb_logits, norm_a_g, norm_b_g, norm_c_g, w_alpha, b_alpha,
           w_pa, w_pb, w_pc, w_out, norm2_g, w_ffn_in, w_ffn_out, norm_f_g):
    bsz, seqlen, d = x.shape
    depth = w_in.shape[0]
    assert d == D_MODEL and seqlen % (2 * CHUNK) == 0
    tm = min(TM_DENSE, seqlen)
    tm_in = min(TM_INPROJ, seqlen)
    assert seqlen % tm == 0 and seqlen % tm_in == 0

    mod = _modulation(c, w_ada, b_ada)
    lbs = _lower_bounds(lb_logits)
    cos, sin = _rotary_tables(seqlen)
    lgf, lgb = _retention_log_decays()

    x2 = x.reshape(bsz * seqlen, d)
    for l in range(depth):
        sh1, sc1, g1, sh2, sc2, g2 = [mod[l, :, i * d:(i + 1) * d].reshape(bsz, 1, d) for i in range(6)]
        wb, wf = _split_w_in(w_in[l])
        gain1 = norm1_g[l].reshape(1, d)
        zb, zf = _inproj(x2, sh1, sc1, gain1, wb, wf, seqlen, tm_in)

        wal = jnp.zeros((LANES, 2 * N_HEADS * BC_DK), F32)
        wal = wal.at[0:GLA_RANK, 0:N_HEADS * BC_DK].set(w_alpha[l, 0])
        wal = wal.at[GLA_RANK:2 * GLA_RANK, N_HEADS * BC_DK:].set(w_alpha[l, 1])
        bal = b_alpha[l].reshape(1, 2 * N_HEADS * BC_DK)

        ya, yc = _mixers_ac(zb, zf, lbs[l].reshape(1, -1), norm_a_g[l].reshape(1, -1),
                            wal.astype(BF16), bal, norm_c_g[l].reshape(1, -1), bsz, seqlen)
        yb = _mixer_b(zb, cos, sin, lgf, lgb, norm_b_g[l].reshape(1, -1), bsz, seqlen)

        x2 = _merge(x2, ya, yb, yc, zb, w_pa[l].astype(BF16), w_pb[l].astype(BF16), w_pc[l].astype(BF16),
                    w_out[l].astype(BF16), g1, seqlen, tm)
        x2 = _ffn(x2, sh2, sc2, g2, norm2_g[l].reshape(1, d), w_ffn_in[l].astype(BF16),
                  w_ffn_out[l].astype(BF16), norm_f_g.reshape(1, d), seqlen, tm,
                  final_norm=(l == depth - 1))
    return x2.reshape(bsz, seqlen, d)
```

```python
import functools
from typing import Any, Callable, NamedTuple

import jax
import jax.numpy as jnp
from jax import lax
from jax.experimental import pallas as pl
from jax.experimental.pallas import tpu as pltpu

F32 = jnp.float32
BF16 = jnp.bfloat16

D_MODEL = 1024
N_HEADS = 4
A_DK = 128
BC_DK = 64
DV = 128
GLA_RANK = 16
GLA_TAU = 16.0
D_FF = 2816
EPS = 1e-6
TINY = 1e-30
LOG2_E = 1.4426950408889634

LANES = 128
CHUNK = 128
assert CHUNK == LANES
VMEM_LIMIT = 48 * 1024 * 1024

TM_INPROJ, TN_INPROJ = 512, 1280
TM_DENSE = 512
TF_FFN = D_FF // 2

ZB_GATE = 0
ZB_AQ, ZB_AI, ZB_AG = 3072, 3584, 4096
ZB_BQ, ZB_BK, ZB_BV, ZB_BG = 4608, 4864, 5120, 5632
ZB_CQ, ZB_CK, ZB_CV, ZB_CG = 6144, 6400, 6656, 7168
ZB_COLS = 7680
ZF_FF, ZF_FB, ZF_LR = 0, 512, 1024
ZF_COLS = 1152


def _dot(a, b):
    return jnp.dot(a, b, preferred_element_type=F32)


def _dot_nt(a, b):
    return lax.dot_general(a, b, (((1,), (1,)), ((), ())), preferred_element_type=F32)


def _dot_tn(a, b):
    return lax.dot_general(a, b, (((0,), (0,)), ((), ())), preferred_element_type=F32)


def _sigmoid(x):
    return jax.nn.sigmoid(x)


def _mod_kernel(c_ref, w_ref, b_ref, o_ref):
    c = c_ref[...]
    a = c * _sigmoid(c)
    w = w_ref[...]
    a_hi = a.astype(BF16)
    a_lo = (a - a_hi.astype(F32)).astype(BF16)
    w_hi = w.astype(BF16)
    w_lo = (w - w_hi.astype(F32)).astype(BF16)
    o_ref[...] = _dot(a_hi, w_hi) + _dot(a_hi, w_lo) + _dot(a_lo, w_hi) + b_ref[...]


def _modulation(c, w_ada, b_ada):
    depth, d, n6 = w_ada.shape
    bsz = c.shape[0]
    return pl.pallas_call(
        _mod_kernel,
        grid=(depth, n6 // d),
        in_specs=[
            pl.BlockSpec((bsz, d), lambda l, j: (0, 0)),
            pl.BlockSpec((None, d, d), lambda l, j: (l, 0, j)),
            pl.BlockSpec((None, 1, d), lambda l, j: (l, 0, j)),
        ],
        out_specs=pl.BlockSpec((None, bsz, d), lambda l, j: (l, 0, j)),
        out_shape=jax.ShapeDtypeStruct((depth, bsz, n6), F32),
        name="adaln_modulation",
    )(c, w_ada, b_ada.reshape(depth, 1, n6))


def _lb_kernel(x_ref, o_ref):
    depth = x_ref.shape[0]
    rows = [x_ref[i:i + 1, :] for i in range(depth)]
    m = rows[0]
    for r in rows[1:]:
        m = jnp.maximum(m, r)
    e = [jnp.exp(r - m) for r in rows]
    s = e[0]
    for t in e[1:]:
        s = s + t
    p = [t / s for t in e]
    acc = p[0]
    o_ref[0:1, :] = jnp.maximum(acc - p[0], 0.0)
    for i in range(1, depth):
        acc = acc + p[i]
        o_ref[i:i + 1, :] = jnp.maximum(acc - p[0], 0.0)


def _lower_bounds(lb_logits):
    return pl.pallas_call(
        _lb_kernel,
        out_shape=jax.ShapeDtypeStruct(lb_logits.shape, F32),
        name="hgrn2_lower_bounds",
    )(lb_logits)


def _norm_mod(x, gain, sc, sh):
    y = x * lax.rsqrt(jnp.mean(x * x, axis=-1, keepdims=True) + EPS)
    return (y * gain) * (1.0 + sc) + sh


def _inproj_kernel(x_ref, sh_ref, sc_ref, g_ref, wb_ref, wf_ref, zb_ref, zf_ref, h_ref):
    h_ref[...] = _norm_mod(x_ref[...], g_ref[...], sc_ref[0], sh_ref[0]).astype(BF16)
    for j in range(zb_ref.shape[1] // TN_INPROJ):
        cols = slice(j * TN_INPROJ, (j + 1) * TN_INPROJ)
        zb_ref[:, cols] = _dot(h_ref[...], wb_ref[:, cols]).astype(zb_ref.dtype)
    zf_ref[...] = _dot(h_ref[...], wf_ref[...])


def _inproj(x2, sh, sc, gain, wb, wf, seqlen, tm):
    m, d = x2.shape
    per = seqlen // tm
    bvec = lambda i: (i // per, 0, 0)
    resident = lambda w: pl.BlockSpec(w.shape, lambda i: (0, 0), pipeline_mode=pl.Buffered(1))
    return pl.pallas_call(
        _inproj_kernel,
        grid=(m // tm,),
        in_specs=[
            pl.BlockSpec((tm, d), lambda i: (i, 0)),
            pl.BlockSpec((1, 1, d), bvec), pl.BlockSpec((1, 1, d), bvec),
            pl.BlockSpec((1, d), lambda i: (0, 0)),
            resident(wb), resident(wf),
        ],
        out_specs=[pl.BlockSpec((tm, wb.shape[1]), lambda i: (i, 0)),
                   pl.BlockSpec((tm, wf.shape[1]), lambda i: (i, 0))],
        out_shape=[jax.ShapeDtypeStruct((m, wb.shape[1]), BF16), jax.ShapeDtypeStruct((m, wf.shape[1]), F32)],
        scratch_shapes=[pltpu.VMEM((tm, d), BF16)],
        compiler_params=pltpu.CompilerParams(dimension_semantics=("parallel",), vmem_limit_bytes=VMEM_LIMIT),
        name="inproj",
    )(x2, sh, sc, gain, wb, wf)


def _merge_kernel(x_ref, ya_ref, yb_ref, yc_ref, ga_ref, gb_ref, gc_ref,
                  wpa_ref, wpb_ref, wpc_ref, wout_ref, g1_ref, o_ref):
    merged = (_sigmoid(ga_ref[...].astype(F32)) * _dot(ya_ref[...], wpa_ref[...])
              + _sigmoid(gb_ref[...].astype(F32)) * _dot(yb_ref[...], wpb_ref[...])
              + _sigmoid(gc_ref[...].astype(F32)) * _dot(yc_ref[...], wpc_ref[...]))
    out = _dot(merged.astype(BF16), wout_ref[...])
    o_ref[...] = x_ref[...] + g1_ref[0] * out


def _merge(x2, ya, yb, yc, zb, wpa, wpb, wpc, wout, g1, seqlen, tm):
    m, d = x2.shape
    w = ya.shape[1]
    per = seqlen // tm
    row = lambda i: (i, 0)
    const = lambda i: (0, 0)
    return pl.pallas_call(
        _merge_kernel,
        grid=(m // tm,),
        in_specs=[
            pl.BlockSpec((tm, d), row),
            pl.BlockSpec((tm, w), row), pl.BlockSpec((tm, w), row), pl.BlockSpec((tm, w), row),
            pl.BlockSpec((tm, d), lambda i: (i, ZB_GATE // d)),
            pl.BlockSpec((tm, d), lambda i: (i, ZB_GATE // d + 1)),
            pl.BlockSpec((tm, d), lambda i: (i, ZB_GATE // d + 2)),
            pl.BlockSpec((w, d), const), pl.BlockSpec((w, d), const), pl.BlockSpec((w, d), const),
            pl.BlockSpec((d, d), const),
            pl.BlockSpec((1, 1, d), lambda i: (i // per, 0, 0)),
        ],
        out_specs=pl.BlockSpec((tm, d), row),
        out_shape=jax.ShapeDtypeStruct((m, d), F32),
        compiler_params=pltpu.CompilerParams(
            dimension_semantics=("parallel",), vmem_limit_bytes=VMEM_LIMIT),
        name="merge_outproj",
    )(x2, ya, yb, yc, zb, zb, zb, wpa, wpb, wpc, wout, g1)


def _ffn_kernel(x_ref, sh_ref, sc_ref, g2_ref, ng_ref, wi_ref, wo_ref, nf_ref, o_ref, h_ref, *, final_norm):
    dff = wo_ref.shape[0]
    h_ref[...] = _norm_mod(x_ref[...], ng_ref[...], sc_ref[0], sh_ref[0]).astype(BF16)
    acc = None
    for k in range(dff // TF_FFN):
        gate = _dot(h_ref[...], wi_ref[:, k * TF_FFN:(k + 1) * TF_FFN])
        up = _dot(h_ref[...], wi_ref[:, dff + k * TF_FFN:dff + (k + 1) * TF_FFN])
        act = (gate * _sigmoid(gate) * up).astype(BF16)
        part = _dot(act, wo_ref[k * TF_FFN:(k + 1) * TF_FFN, :])
        acc = part if acc is None else acc + part
    xn = x_ref[...] + g2_ref[0] * acc
    if final_norm:
        xn = xn * lax.rsqrt(jnp.mean(xn * xn, axis=-1, keepdims=True) + EPS) * nf_ref[...]
    o_ref[...] = xn


def _ffn(x2, sh, sc, g2, ng, w_in, w_out, nf, seqlen, tm, final_norm):
    m, d = x2.shape
    per = seqlen // tm
    bvec = lambda i: (i // per, 0, 0)
    resident = lambda w: pl.BlockSpec(w.shape, lambda i: (0, 0), pipeline_mode=pl.Buffered(1))
    return pl.pallas_call(
        functools.partial(_ffn_kernel, final_norm=final_norm),
        grid=(m // tm,),
        in_specs=[
            pl.BlockSpec((tm, d), lambda i: (i, 0)),
            pl.BlockSpec((1, 1, d), bvec), pl.BlockSpec((1, 1, d), bvec), pl.BlockSpec((1, 1, d), bvec),
            pl.BlockSpec((1, d), lambda i: (0, 0)),
            resident(w_in), resident(w_out),
            pl.BlockSpec((1, d), lambda i: (0, 0)),
        ],
        out_specs=pl.BlockSpec((tm, d), lambda i: (i, 0)),
        out_shape=jax.ShapeDtypeStruct((m, d), F32),
        scratch_shapes=[pltpu.VMEM((tm, d), BF16)],
        compiler_params=pltpu.CompilerParams(dimension_semantics=("parallel",), vmem_limit_bytes=VMEM_LIMIT),
        name="swiglu_ffn",
    )(x2, sh, sc, g2, ng, w_in, w_out, nf)


N_LEVELS = CHUNK.bit_length() - 1


def _block_sums(g):
    row = lax.broadcasted_iota(jnp.int32, g.shape, 0)
    lb, tb = g, g
    out = [(lb, tb)]
    b = 1
    while b < CHUNK:
        right = (row & b) != 0
        from_left = pltpu.roll(tb, b, axis=0)
        from_right = pltpu.roll(tb, CHUNK - b, axis=0)
        lb = lb + jnp.where(right, from_left, 0.0)
        tb = tb + jnp.where(right, from_left, from_right)
        out.append((lb, tb))
        b *= 2
    return out


def _gated_factors(g2, forward):
    sums = _block_sums(g2)
    facs = [(jnp.exp2(g2), None)]
    for lb, tb in sums[1:]:
        if forward:
            facs.append((jnp.exp2(lb), jnp.exp2(tb - lb)))
        else:
            facs.append((jnp.exp2((tb - lb) + g2), jnp.exp2(lb - g2)))
    return facs, jnp.exp2(sums[-1][1][0:1, :])


def _fixed_factors(log_decay, forward):
    pos = lax.broadcasted_iota(jnp.int32, (CHUNK, LANES), 0).astype(F32)
    mid = CHUNK // 2
    if forward:
        facs = [(jnp.exp((pos - (mid - 1.0)) * log_decay), jnp.exp(((mid - 1.0) - pos) * log_decay)),
                (jnp.exp((pos + 1.0) * log_decay), jnp.exp((CHUNK - 1.0 - pos) * log_decay))]
    else:
        facs = [(jnp.exp((mid - pos) * log_decay), jnp.exp((pos - mid) * log_decay)),
                (jnp.exp((CHUNK - pos) * log_decay), jnp.exp(pos * log_decay))]
    return facs, jnp.exp(float(CHUNK) * log_decay)


def _store_masks(mask_ref, hierarchical):
    row = lax.broadcasted_iota(jnp.int32, (CHUNK, CHUNK), 0)
    col = lax.broadcasted_iota(jnp.int32, (CHUNK, CHUNK), 1)
    one = lambda m: jnp.where(m, 1.0, 0.0).astype(BF16)
    mask_ref[0] = one(row == col)
    n = N_LEVELS if hierarchical else 1
    for level in range(n):
        same_level = ((row ^ col) >> level) == 1 if hierarchical else (row != col)
        mask_ref[1 + level] = one(same_level & (row > col))
        mask_ref[1 + n + level] = one(same_level & (col > row))
    return n


def _lane_masks(n_sub):
    if n_sub == 1:
        return [None]
    lane = lax.broadcasted_iota(jnp.int32, (1, LANES), 1)
    width = LANES // n_sub
    return [jnp.where((lane >= i * width) & (lane < (i + 1) * width), 1.0, 0.0) for i in range(n_sub)]


def _chunk_rows(n):
    if isinstance(n, int):
        return pl.ds(n * CHUNK, CHUNK)
    return pl.ds(pl.multiple_of(n * CHUNK, CHUNK), CHUNK)


class _Mixer(NamedTuple):
    n_groups: int
    n_sub: int
    hierarchical: bool
    prep: Callable
    load_v: Callable
    of_ref: Any
    ob_ref: Any
    st_ref: Any
    mask_ref: Any
    slots: tuple


def _scan_mixers(n_chunks, mixers):
    levels = []
    for m in mixers:
        m.st_ref[...] = jnp.zeros_like(m.st_ref)
        levels.append(_store_masks(m.mask_ref, m.hierarchical))
        if m.hierarchical and m.n_sub > 1:
            for _, kt_ref, _, _ in m.slots:
                kt_ref[...] = jnp.zeros_like(kt_ref)

    def rows_of(n, forward):
        return _chunk_rows(n if forward else n_chunks - 1 - n)

    def chains(m):
        return enumerate((gi, forward) for gi in range(m.n_groups) for forward in (True, False))

    def stage(n, slot):
        for m in mixers:
            qt_ref, kt_ref, p0_ref, dec_ref = m.slots[slot]
            lane_masks = _lane_masks(m.n_sub)
            for c, (gi, forward) in chains(m):
                q, k, (facs, chunk_decay) = m.prep(gi, forward, rows_of(n, forward))
                qb, kb = q.astype(BF16), k.astype(BF16)
                for level, (fq, fk) in enumerate(facs):
                    qt_ref[c, level] = qb * fq.astype(BF16)
                    if m.hierarchical and level < len(facs) - 1:
                        kt = (kb if fk is None else kb * fk.astype(BF16)).T
                        width = LANES // m.n_sub
                        for i in range(m.n_sub):
                            kt_ref[c, level, i, i * width:(i + 1) * width, :] = kt[i * width:(i + 1) * width, :]
                    else:
                        kt_ref[c, level, 0] = kb * fk.astype(BF16)
                dec_ref[c] = chunk_decay
                if forward:
                    for i in range(m.n_sub):
                        qm = q if lane_masks[i] is None else q * lane_masks[i]
                        diag = jnp.sum(qm * k, axis=-1, keepdims=True).astype(BF16)
                        p0_ref[gi * m.n_sub + i] = m.mask_ref[0] * diag

    def consume(n, slot):
        for m, n_levels in zip(mixers, levels):
            qt_ref, kt_ref, p0_ref, dec_ref = m.slots[slot]
            lane_masks = _lane_masks(m.n_sub)
            for c, (gi, forward) in chains(m):
                rows = rows_of(n, forward)
                o_ref = m.of_ref if forward else m.ob_ref
                mask0 = 1 if forward else 1 + n_levels
                for i, v in enumerate(m.load_v(gi, rows)):
                    head = gi * m.n_sub + i
                    sub = (lambda t: t) if lane_masks[i] is None else (lambda t, lm=lane_masks[i].astype(BF16): t * lm)
                    p = p0_ref[head] if forward else None
                    for level in range(n_levels):
                        if m.hierarchical:
                            s = _dot(qt_ref[c, level], kt_ref[c, level, i])
                        else:
                            s = _dot_nt(sub(qt_ref[c, level]), kt_ref[c, level, 0])
                        s = s.astype(BF16) * m.mask_ref[mask0 + level]
                        p = s if p is None else p + s
                    st = m.st_ref[c * m.n_sub + i]
                    inter = _dot_nt(sub(qt_ref[c, n_levels]), st.astype(BF16))
                    o_ref[rows, head * DV:(head + 1) * DV] = _dot(p, v) + inter
                    m.st_ref[c * m.n_sub + i] = st * dec_ref[c] + _dot_tn(v, kt_ref[c, n_levels, 0])

    def body(step, carry):
        n = 2 * step
        consume(n, 0)
        stage(n + 1, 1)
        consume(n + 1, 1)
        stage(jnp.minimum(n + 2, n_chunks - 1), 0)
        return carry

    stage(0, 0)
    lax.fori_loop(0, n_chunks // 2, body, 0)


def _finish(mixer, gain_ref, gate_ref, y_ref, center, swish):
    rows_per = 256
    n_blocks = mixer.of_ref.shape[0] // rows_per

    def body(b, carry):
        rows = pl.ds(pl.multiple_of(b * rows_per, rows_per), rows_per)
        for i in range(mixer.n_groups * mixer.n_sub):
            cols = slice(i * DV, (i + 1) * DV)
            o = mixer.of_ref[rows, cols] + mixer.ob_ref[rows, cols]
            if center:
                o = o - jnp.mean(o, axis=-1, keepdims=True)
            o = o * lax.rsqrt(jnp.mean(o * o, axis=-1, keepdims=True) + EPS)
            gz = gate_ref[rows, cols].astype(F32)
            act = gz * _sigmoid(gz) if swish else _sigmoid(gz)
            y_ref[rows, cols] = ((o * gain_ref[:, cols]) * act).astype(y_ref.dtype)
        return carry

    lax.fori_loop(0, n_blocks, body, 0)


def _group_cols(gi):
    return slice(gi * LANES, (gi + 1) * LANES)


N_SLOTS = 2
N_MIXER_SCRATCH = 4 + 4 * N_SLOTS


def _mixer(n_groups, n_sub, hierarchical, prep, load_v, scratch):
    of_ref, ob_ref, st_ref, mask_ref = scratch[:4]
    return _Mixer(n_groups, n_sub, hierarchical, prep, load_v, of_ref, ob_ref, st_ref, mask_ref,
                  tuple(scratch[4 + 4 * s:8 + 4 * s] for s in range(N_SLOTS)))


def _mixer_scratch(seqlen, n_groups, n_sub, hierarchical):
    n_heads, n_chains = n_groups * n_sub, 2 * n_groups
    n_levels = N_LEVELS if hierarchical else 1
    key_copies = n_sub if hierarchical else 1
    slot = [pltpu.VMEM((n_chains, n_levels + 1, CHUNK, LANES), BF16),
            pltpu.VMEM((n_chains, n_levels + 1, key_copies, CHUNK, LANES), BF16),
            pltpu.VMEM((n_heads, CHUNK, CHUNK), BF16),
            pltpu.VMEM((n_chains, 1, LANES), F32)]
    return [pltpu.VMEM((seqlen, n_heads * DV), F32),
            pltpu.VMEM((seqlen, n_heads * DV), F32),
            pltpu.VMEM((2 * n_heads, DV, LANES), F32),
            pltpu.VMEM((1 + 2 * n_levels, CHUNK, CHUNK), BF16)] + N_SLOTS * slot


def _hgrn2_mixer(q_ref, v_ref, ff_ref, fb_ref, lb_ref, scratch):
    def prep(gi, forward, rows):
        cols = _group_cols(gi)
        lb = lb_ref[:, cols]
        z = (ff_ref if forward else fb_ref)[rows, cols]
        f = lb + (1.0 - lb) * _sigmoid(z)
        g2 = jnp.log2(jnp.maximum(f, TINY))
        key = 1.0 - f
        aq = q_ref[rows, cols].astype(F32)
        return aq * _sigmoid(aq), key, _gated_factors(g2, forward)

    return _mixer(q_ref.shape[1] // LANES, 1, True, prep,
                  lambda gi, rows: [v_ref[rows, _group_cols(gi)]], scratch)


def _rotate_half(x):
    lane = lax.broadcasted_iota(jnp.int32, x.shape, 1)
    half = BC_DK // 2
    first = (lane % BC_DK) < half
    return jnp.where(first, pltpu.roll(x, LANES - half, axis=1), pltpu.roll(x, half, axis=1))


def _sub_values(v_ref, gi, rows):
    return [v_ref[rows, (2 * gi + i) * DV:(2 * gi + i + 1) * DV] for i in range(2)]


def _retention_mixer(q_ref, k_ref, v_ref, cos_ref, sin_ref, lgf_ref, lgb_ref, scratch):
    n_groups = q_ref.shape[1] // LANES
    packs = [(_fixed_factors(lgb_ref[:, _group_cols(gi)], False), _fixed_factors(lgf_ref[:, _group_cols(gi)], True))
             for gi in range(n_groups)]

    def prep(gi, forward, rows):
        cols = _group_cols(gi)
        cos = cos_ref[rows, :]
        sin = sin_ref[rows, :]
        q = q_ref[rows, cols].astype(F32)
        k = k_ref[rows, cols].astype(F32)
        q = q * cos + _rotate_half(q) * sin
        k = (k * cos + _rotate_half(k) * sin) * (BC_DK ** -0.5)
        return q, k, packs[gi][1 if forward else 0]

    return _mixer(n_groups, 2, False, prep, functools.partial(_sub_values, v_ref), scratch)


def _gla_mixer(q_ref, k_ref, v_ref, lr_ref, waf_ref, wab_ref, baf_ref, bab_ref, scratch):
    def prep(gi, forward, rows):
        cols = _group_cols(gi)
        w_ref, b_ref = (waf_ref, baf_ref) if forward else (wab_ref, bab_ref)
        x = _dot(lr_ref[rows, :].astype(BF16), w_ref[:, cols]) + b_ref[:, cols]
        log_sig = jnp.minimum(x, 0.0) - jnp.log(1.0 + jnp.exp(-jnp.abs(x)))
        g2 = log_sig * (LOG2_E / GLA_TAU)
        q = q_ref[rows, cols].astype(F32) * (BC_DK ** -0.5)
        return q, k_ref[rows, cols].astype(F32), _gated_factors(g2, forward)

    return _mixer(q_ref.shape[1] // LANES, 2, True, prep, functools.partial(_sub_values, v_ref), scratch)


def _mixers_ac_kernel(aq_ref, av_ref, aog_ref, ff_ref, fb_ref, lb_ref, again_ref,
                      cq_ref, ck_ref, cv_ref, csg_ref, lr_ref, waf_ref, wab_ref, baf_ref, bab_ref, cgain_ref,
                      ya_ref, yc_ref, *scratch):
    hgrn2 = _hgrn2_mixer(aq_ref, av_ref, ff_ref, fb_ref, lb_ref, scratch[:N_MIXER_SCRATCH])
    gla = _gla_mixer(cq_ref, ck_ref, cv_ref, lr_ref, waf_ref, wab_ref, baf_ref, bab_ref,
                     scratch[N_MIXER_SCRATCH:])
    _scan_mixers(aq_ref.shape[0] // CHUNK, [hgrn2, gla])
    _finish(hgrn2, again_ref, aog_ref, ya_ref, center=False, swish=False)
    _finish(gla, cgain_ref, csg_ref, yc_ref, center=False, swish=True)


def _mixer_b_kernel(q_ref, k_ref, v_ref, sg_ref, cos_ref, sin_ref, lgf_ref, lgb_ref, gain_ref, y_ref, *scratch):
    retention = _retention_mixer(q_ref, k_ref, v_ref, cos_ref, sin_ref, lgf_ref, lgb_ref, scratch)
    _scan_mixers(q_ref.shape[0] // CHUNK, [retention])
    _finish(retention, gain_ref, sg_ref, y_ref, center=True, swish=True)


def _zspec(seqlen, width, col0):
    blk = col0 // width
    return pl.BlockSpec((seqlen, width), lambda b, g: (b, blk + g))


def _vspec(width, blk0=0):
    return pl.BlockSpec((1, width), lambda b, g: (0, blk0 + g))


def _mixer_params():
    return pltpu.CompilerParams(dimension_semantics=("parallel", "parallel"), vmem_limit_bytes=VMEM_LIMIT)


def _mixers_ac(zb, zf, lb, gain_a, wal, bal, gain_c, bsz, seqlen):
    t, w = seqlen, 2 * DV
    n_steps = N_HEADS * DV // w
    half = N_HEADS * BC_DK // LANES
    out = pl.BlockSpec((t, w), lambda b, g: (b, g))
    return pl.pallas_call(
        _mixers_ac_kernel,
        grid=(bsz, n_steps),
        in_specs=[_zspec(t, w, ZB_AQ), _zspec(t, w, ZB_AI), _zspec(t, w, ZB_AG),
                  _zspec(t, w, ZF_FF), _zspec(t, w, ZF_FB), _vspec(w), _vspec(w),
                  _zspec(t, LANES, ZB_CQ), _zspec(t, LANES, ZB_CK), _zspec(t, w, ZB_CV), _zspec(t, w, ZB_CG),
                  pl.BlockSpec((t, LANES), lambda b, g: (b, ZF_LR // LANES)),
                  pl.BlockSpec((LANES, LANES), lambda b, g: (0, g)),
                  pl.BlockSpec((LANES, LANES), lambda b, g: (0, half + g)),
                  _vspec(LANES), _vspec(LANES, half), _vspec(w)],
        out_specs=[out, out],
        out_shape=[jax.ShapeDtypeStruct((bsz * t, n_steps * w), BF16)] * 2,
        scratch_shapes=_mixer_scratch(t, 2, 1, True) + _mixer_scratch(t, 1, 2, True),
        compiler_params=_mixer_params(),
        name="mixers_hgrn2_gla",
    )(zb, zb, zb, zf, zf, lb, gain_a, zb, zb, zb, zb, zf, wal, wal, bal, bal, gain_c)


def _mixer_b(zb, cos, sin, lgf, lgb, gain, bsz, seqlen):
    t, w, wv = seqlen, N_HEADS * BC_DK, N_HEADS * DV
    table = pl.BlockSpec((t, LANES), lambda b, g: (0, 0))
    return pl.pallas_call(
        _mixer_b_kernel,
        grid=(bsz, 1),
        in_specs=[_zspec(t, w, ZB_BQ), _zspec(t, w, ZB_BK), _zspec(t, wv, ZB_BV), _zspec(t, wv, ZB_BG),
                  table, table, _vspec(w), _vspec(w), _vspec(wv)],
        out_specs=pl.BlockSpec((t, wv), lambda b, g: (b, g)),
        out_shape=jax.ShapeDtypeStruct((bsz * t, wv), BF16),
        scratch_shapes=_mixer_scratch(t, w // LANES, 2, False),
        compiler_params=_mixer_params(),
        name="mixer_retention",
    )(zb, zb, zb, zb, cos, sin, lgf, lgb, gain)


def _split_w_in(w):
    o = {}
    off = 0
    for name, size in (("a_q", 512), ("a_ff", 512), ("a_fb", 512), ("a_i", 512), ("a_g", 512),
                       ("b_q", 256), ("b_k", 256), ("b_v", 512), ("b_g", 512),
                       ("c_q", 256), ("c_k", 256), ("c_v", 512), ("c_g", 512), ("c_lr", 32),
                       ("gate_a", 1024), ("gate_b", 1024), ("gate_c", 1024)):
        o[name] = w[:, off:off + size]
        off += size
    wb = jnp.concatenate([o[n] for n in ("gate_a", "gate_b", "gate_c", "a_q", "a_i", "a_g",
                                         "b_q", "b_k", "b_v", "b_g", "c_q", "c_k", "c_v", "c_g")], axis=1)
    pad = jnp.zeros((w.shape[0], ZF_COLS - ZF_LR - 2 * GLA_RANK), w.dtype)
    wf = jnp.concatenate([o["a_ff"], o["a_fb"], o["c_lr"], pad], axis=1)
    return wb.astype(BF16), wf.astype(BF16)


def _rotary_tables(seqlen):
    pos = jnp.arange(seqlen, dtype=F32)
    inv_freq = 10000.0 ** (-jnp.arange(0, BC_DK, 2, dtype=F32) / BC_DK)
    ang = pos[:, None] * inv_freq[None, :]
    cos, sin = jnp.cos(ang), jnp.sin(ang)
    reps = LANES // BC_DK
    return (jnp.tile(jnp.concatenate([cos, cos], axis=1), (1, reps)),
            jnp.tile(jnp.concatenate([-sin, sin], axis=1), (1, reps)))


def _retention_log_decays():
    h = jnp.arange(N_HEADS, dtype=F32)
    fwd = jnp.log1p(-jnp.exp2(-5.0 - h))
    bwd = jnp.log1p(-jnp.exp2(-5.0 - h[::-1]))
    spread = lambda v: jnp.repeat(v, BC_DK)[None, :]
    return spread(fwd), spread(bwd)


def kernel(x, c, norm1_g, w_ada, b_ada, w_in, lb_logits, norm_a_g, norm_b_g, norm_c_g, w_alpha, b_alpha,
           w_pa, w_pb, w_pc, w_out, norm2_g, w_ffn_in, w_ffn_out, norm_f_g):
    bsz, seqlen, d = x.shape
    depth = w_in.shape[0]
    assert d == D_MODEL and seqlen % (2 * CHUNK) == 0
    tm = min(TM_DENSE, seqlen)
    tm_in = min(TM_INPROJ, seqlen)
    assert seqlen % tm == 0 and seqlen % tm_in == 0

    mod = _modulation(c, w_ada, b_ada)
    lbs = _lower_bounds(lb_logits)
    cos, sin = _rotary_tables(seqlen)
    lgf, lgb = _retention_log_decays()

    x2 = x.reshape(bsz * seqlen, d)
    for l in range(depth):
        sh1, sc1, g1, sh2, sc2, g2 = [mod[l, :, i * d:(i + 1) * d].reshape(bsz, 1, d) for i in range(6)]
        wb, wf = _split_w_in(w_in[l])
        gain1 = norm1_g[l].reshape(1, d)
        zb, zf = _inproj(x2, sh1, sc1, gain1, wb, wf, seqlen, tm_in)

        wal = jnp.zeros((LANES, 2 * N_HEADS * BC_DK), F32)
        wal = wal.at[0:GLA_RANK, 0:N_HEADS * BC_DK].set(w_alpha[l, 0])
        wal = wal.at[GLA_RANK:2 * GLA_RANK, N_HEADS * BC_DK:].set(w_alpha[l, 1])
        bal = b_alpha[l].reshape(1, 2 * N_HEADS * BC_DK)

        ya, yc = _mixers_ac(zb, zf, lbs[l].reshape(1, -1), norm_a_g[l].reshape(1, -1),
                            wal.astype(BF16), bal, norm_c_g[l].reshape(1, -1), bsz, seqlen)
        yb = _mixer_b(zb, cos, sin, lgf, lgb, norm_b_g[l].reshape(1, -1), bsz, seqlen)

        x2 = _merge(x2, ya, yb, yc, zb, w_pa[l].astype(BF16), w_pb[l].astype(BF16), w_pc[l].astype(BF16),
                    w_out[l].astype(BF16), g1, seqlen, tm)
        x2 = _ffn(x2, sh2, sc2, g2, norm2_g[l].reshape(1, d), w_ffn_in[l].astype(BF16),
                  w_ffn_out[l].astype(BF16), norm_f_g.reshape(1, d), seqlen, tm,
                  final_norm=(l == depth - 1))
    return x2.reshape(bsz, seqlen, d)
```

```python
import functools
from typing import Any, Callable, NamedTuple

import jax
import jax.numpy as jnp
from jax import lax
from jax.experimental import pallas as pl
from jax.experimental.pallas import tpu as pltpu

F32 = jnp.float32
BF16 = jnp.bfloat16

D_MODEL = 1024
N_HEADS = 4
BC_DK = 64
DV = 128
GLA_RANK = 16
GLA_TAU = 16.0
D_FF = 2816
EPS = 1e-6
TINY = 1e-30
LOG2_E = 1.4426950408889634

LANES = 128
CHUNK = 128
assert CHUNK <= LANES
VMEM_LIMIT = 48 * 1024 * 1024
VMEM_LIMIT_MIXERS = 58 * 1024 * 1024

TM_INPROJ, TN_INPROJ = 512, 1280
TM_DENSE = 512
TF_FFN = D_FF // 2

ZB_PAIR_COLS = 2304
PAIR_AQ, PAIR_AI, PAIR_AG = 0, 256, 512
PAIR_CQ, PAIR_CK, PAIR_CV, PAIR_CG = 768, 896, 1024, 1280
PAIR_BQ, PAIR_BK, PAIR_BV, PAIR_BG = 1536, 1664, 1792, 2048
ZB_GATE = 4608
ZB_GATE_BLOCK = 1536
ZF_PAIR_COLS = 512
PAIR_FF, PAIR_FB = 0, 256
ZF_LR = 1024
ZF_COLS = 1152


def _dot(a, b):
    return jnp.dot(a, b, preferred_element_type=F32)


def _dot_nt(a, b):
    return lax.dot_general(a, b, (((1,), (1,)), ((), ())), preferred_element_type=F32)


def _dot_tn(a, b):
    return lax.dot_general(a, b, (((0,), (0,)), ((), ())), preferred_element_type=F32)


def _sigmoid(x):
    return jax.nn.sigmoid(x)


def _mod_kernel(c_ref, w_ref, b_ref, o_ref):
    c = c_ref[...]
    a = c * _sigmoid(c)
    w = w_ref[...]
    a_hi = a.astype(BF16)
    a_lo = (a - a_hi.astype(F32)).astype(BF16)
    w_hi = w.astype(BF16)
    w_lo = (w - w_hi.astype(F32)).astype(BF16)
    o_ref[...] = _dot(a_hi, w_hi) + _dot(a_hi, w_lo) + _dot(a_lo, w_hi) + b_ref[...]


def _modulation(c, w_ada, b_ada):
    depth, d, n6 = w_ada.shape
    bsz = c.shape[0]
    return pl.pallas_call(
        _mod_kernel,
        grid=(depth, n6 // d),
        in_specs=[
            pl.BlockSpec((bsz, d), lambda l, j: (0, 0)),
            pl.BlockSpec((None, d, d), lambda l, j: (l, 0, j)),
            pl.BlockSpec((None, 1, d), lambda l, j: (l, 0, j)),
        ],
        out_specs=pl.BlockSpec((None, bsz, d), lambda l, j: (l, 0, j)),
        out_shape=jax.ShapeDtypeStruct((depth, bsz, n6), F32),
        name="adaln_modulation",
    )(c, w_ada, b_ada.reshape(depth, 1, n6))


def _lb_kernel(x_ref, o_ref):
    depth = x_ref.shape[0]
    rows = [x_ref[i:i + 1, :] for i in range(depth)]
    m = rows[0]
    for r in rows[1:]:
        m = jnp.maximum(m, r)
    e = [jnp.exp(r - m) for r in rows]
    s = e[0]
    for t in e[1:]:
        s = s + t
    p = [t / s for t in e]
    acc = p[0]
    o_ref[0:1, :] = jnp.maximum(acc - p[0], 0.0)
    for i in range(1, depth):
        acc = acc + p[i]
        o_ref[i:i + 1, :] = jnp.maximum(acc - p[0], 0.0)


def _lower_bounds(lb_logits):
    return pl.pallas_call(
        _lb_kernel,
        out_shape=jax.ShapeDtypeStruct(lb_logits.shape, F32),
        name="hgrn2_lower_bounds",
    )(lb_logits)


def _norm_mod(x, gain, sc, sh):
    y = x * lax.rsqrt(jnp.mean(x * x, axis=-1, keepdims=True) + EPS)
    return (y * gain) * (1.0 + sc) + sh


def _inproj_kernel(x_ref, sh_ref, sc_ref, g_ref, wb_ref, wf_ref, zb_ref, zf_ref, h_ref):
    h_ref[...] = _norm_mod(x_ref[...], g_ref[...], sc_ref[0], sh_ref[0]).astype(BF16)
    for j in range(zb_ref.shape[1] // TN_INPROJ):
        cols = slice(j * TN_INPROJ, (j + 1) * TN_INPROJ)
        zb_ref[:, cols] = _dot(h_ref[...], wb_ref[:, cols]).astype(zb_ref.dtype)
    zf_ref[...] = _dot(h_ref[...], wf_ref[...])


def _inproj(x2, sh, sc, gain, wb, wf, seqlen, tm):
    m, d = x2.shape
    per = seqlen // tm
    bvec = lambda i: (i // per, 0, 0)
    resident = lambda w: pl.BlockSpec(w.shape, lambda i: (0, 0), pipeline_mode=pl.Buffered(1))
    return pl.pallas_call(
        _inproj_kernel,
        grid=(m // tm,),
        in_specs=[
            pl.BlockSpec((tm, d), lambda i: (i, 0)),
            pl.BlockSpec((1, 1, d), bvec), pl.BlockSpec((1, 1, d), bvec),
            pl.BlockSpec((1, d), lambda i: (0, 0)),
            resident(wb), resident(wf),
        ],
        out_specs=[pl.BlockSpec((tm, wb.shape[1]), lambda i: (i, 0)),
                   pl.BlockSpec((tm, wf.shape[1]), lambda i: (i, 0))],
        out_shape=[jax.ShapeDtypeStruct((m, wb.shape[1]), BF16), jax.ShapeDtypeStruct((m, wf.shape[1]), F32)],
        scratch_shapes=[pltpu.VMEM((tm, d), BF16)],
        compiler_params=pltpu.CompilerParams(dimension_semantics=("parallel",), vmem_limit_bytes=VMEM_LIMIT),
        name="inproj",
    )(x2, sh, sc, gain, wb, wf)


def _merge_kernel(x_ref, ya_ref, yb_ref, yc_ref, gates_lo_ref, gates_hi_ref,
                  wpa_ref, wpb_ref, wpc_ref, wout_ref, g1_ref, o_ref):
    d = x_ref.shape[1]
    gates = jnp.concatenate([gates_lo_ref[...], gates_hi_ref[...]], axis=1).astype(F32)
    merged = (_sigmoid(gates[:, 0:d]) * _dot(ya_ref[...], wpa_ref[...])
              + _sigmoid(gates[:, d:2 * d]) * _dot(yb_ref[...], wpb_ref[...])
              + _sigmoid(gates[:, 2 * d:3 * d]) * _dot(yc_ref[...], wpc_ref[...]))
    out = _dot(merged.astype(BF16), wout_ref[...])
    o_ref[...] = x_ref[...] + g1_ref[0] * out


def _merge(x2, ya, yb, yc, zb, wpa, wpb, wpc, wout, g1, seqlen, tm):
    m, d = x2.shape
    w = ya.shape[1]
    per = seqlen // tm
    row = lambda i: (i, 0)
    const = lambda i: (0, 0)
    return pl.pallas_call(
        _merge_kernel,
        grid=(m // tm,),
        in_specs=[
            pl.BlockSpec((tm, d), row),
            pl.BlockSpec((tm, w), row), pl.BlockSpec((tm, w), row), pl.BlockSpec((tm, w), row),
            pl.BlockSpec((tm, ZB_GATE_BLOCK), lambda i: (i, ZB_GATE // ZB_GATE_BLOCK)),
            pl.BlockSpec((tm, ZB_GATE_BLOCK), lambda i: (i, ZB_GATE // ZB_GATE_BLOCK + 1)),
            pl.BlockSpec((w, d), const), pl.BlockSpec((w, d), const), pl.BlockSpec((w, d), const),
            pl.BlockSpec((d, d), const),
            pl.BlockSpec((1, 1, d), lambda i: (i // per, 0, 0)),
        ],
        out_specs=pl.BlockSpec((tm, d), row),
        out_shape=jax.ShapeDtypeStruct((m, d), F32),
        compiler_params=pltpu.CompilerParams(
            dimension_semantics=("parallel",), vmem_limit_bytes=VMEM_LIMIT),
        name="merge_outproj",
    )(x2, ya, yb, yc, zb, zb, wpa, wpb, wpc, wout, g1)


def _ffn_kernel(x_ref, sh_ref, sc_ref, g2_ref, ng_ref, wi_ref, wo_ref, nf_ref, o_ref, h_ref, *, final_norm):
    dff = wo_ref.shape[0]
    h_ref[...] = _norm_mod(x_ref[...], ng_ref[...], sc_ref[0], sh_ref[0]).astype(BF16)
    acc = None
    for k in range(dff // TF_FFN):
        gate = _dot(h_ref[...], wi_ref[:, k * TF_FFN:(k + 1) * TF_FFN])
        up = _dot(h_ref[...], wi_ref[:, dff + k * TF_FFN:dff + (k + 1) * TF_FFN])
        act = (gate * _sigmoid(gate) * up).astype(BF16)
        part = _dot(act, wo_ref[k * TF_FFN:(k + 1) * TF_FFN, :])
        acc = part if acc is None else acc + part
    xn = x_ref[...] + g2_ref[0] * acc
    if final_norm:
        xn = xn * lax.rsqrt(jnp.mean(xn * xn, axis=-1, keepdims=True) + EPS) * nf_ref[...]
    o_ref[...] = xn


def _ffn(x2, sh, sc, g2, ng, w_in, w_out, nf, seqlen, tm, final_norm):
    m, d = x2.shape
    per = seqlen // tm
    bvec = lambda i: (i // per, 0, 0)
    resident = lambda w: pl.BlockSpec(w.shape, lambda i: (0, 0), pipeline_mode=pl.Buffered(1))
    return pl.pallas_call(
        functools.partial(_ffn_kernel, final_norm=final_norm),
        grid=(m // tm,),
        in_specs=[
            pl.BlockSpec((tm, d), lambda i: (i, 0)),
            pl.BlockSpec((1, 1, d), bvec), pl.BlockSpec((1, 1, d), bvec), pl.BlockSpec((1, 1, d), bvec),
            pl.BlockSpec((1, d), lambda i: (0, 0)),
            resident(w_in), resident(w_out),
            pl.BlockSpec((1, d), lambda i: (0, 0)),
        ],
        out_specs=pl.BlockSpec((tm, d), lambda i: (i, 0)),
        out_shape=jax.ShapeDtypeStruct((m, d), F32),
        scratch_shapes=[pltpu.VMEM((tm, d), BF16)],
        compiler_params=pltpu.CompilerParams(dimension_semantics=("parallel",), vmem_limit_bytes=VMEM_LIMIT),
        name="swiglu_ffn",
    )(x2, sh, sc, g2, ng, w_in, w_out, nf)


N_LEVELS = CHUNK.bit_length() - 1


def _gated_factors(g2, forward):
    row = lax.broadcasted_iota(jnp.int32, g2.shape, 0)
    yield jnp.exp2(g2), None, None
    lb, tb = g2, g2
    half = CHUNK // 2
    b = 1
    while b < half:
        right = (row & b) != 0
        from_left = pltpu.roll(tb, b, axis=0)
        from_right = pltpu.roll(tb, CHUNK - b, axis=0)
        lb = lb + jnp.where(right, from_left, 0.0)
        tb = tb + jnp.where(right, from_left, from_right)
        b *= 2
        rest = tb - lb
        fq, fk = (jnp.exp2(lb), jnp.exp2(rest)) if forward else (jnp.exp2(rest + g2), jnp.exp2(lb - g2))
        yield fq, fk, None
    decay_lo, decay_hi = jnp.exp2(tb[0:1, :]), jnp.exp2(tb[half:half + 1, :])
    after_lo = jnp.where(row >= half, decay_lo, 1.0)
    before_hi = jnp.where(row < half, decay_hi, 1.0)
    if forward:
        yield fq * after_lo, fk * before_hi, decay_lo * decay_hi
    else:
        yield fq * before_hi, fk * after_lo, decay_lo * decay_hi


def _fixed_factors(log_decay, forward):
    pos = lax.broadcasted_iota(jnp.int32, (CHUNK, LANES), 0).astype(F32)
    mid = CHUNK // 2
    chunk_decay = jnp.exp(float(CHUNK) * log_decay)
    if forward:
        return [(jnp.exp((pos - (mid - 1.0)) * log_decay), jnp.exp(((mid - 1.0) - pos) * log_decay), None),
                (jnp.exp((pos + 1.0) * log_decay), jnp.exp((CHUNK - 1.0 - pos) * log_decay), chunk_decay)]
    return [(jnp.exp((mid - pos) * log_decay), jnp.exp((pos - mid) * log_decay), None),
            (jnp.exp((CHUNK - pos) * log_decay), jnp.exp(pos * log_decay), chunk_decay)]


def _store_masks(mask_ref, hierarchical):
    row = lax.broadcasted_iota(jnp.int32, (CHUNK, CHUNK), 0)
    col = lax.broadcasted_iota(jnp.int32, (CHUNK, CHUNK), 1)
    one = lambda m: jnp.where(m, 1.0, 0.0).astype(BF16)
    mask_ref[0] = one(row == col)
    n = N_LEVELS if hierarchical else 1
    for level in range(n):
        same_level = ((row ^ col) >> level) == 1 if hierarchical else (row != col)
        mask_ref[1 + level] = one(same_level & (row > col))
        mask_ref[1 + n + level] = one(same_level & (col > row))
    return n


def _lane_masks(n_sub):
    if n_sub == 1:
        return [None]
    lane = lax.broadcasted_iota(jnp.int32, (1, LANES), 1)
    width = LANES // n_sub
    return [jnp.where((lane >= i * width) & (lane < (i + 1) * width), 1.0, 0.0) for i in range(n_sub)]


def _chunk_rows(n):
    if isinstance(n, int):
        return pl.ds(n * CHUNK, CHUNK)
    return pl.ds(pl.multiple_of(n * CHUNK, CHUNK), CHUNK)


class _Mixer(NamedTuple):
    n_groups: int
    n_sub: int
    hierarchical: bool
    prep: Callable
    load_v: Callable
    of_ref: Any
    ob_ref: Any
    st_ref: Any
    mask_ref: Any
    slots: tuple


def _scan_mixers(n_chunks, mixers):
    levels = []
    for m in mixers:
        m.st_ref[...] = jnp.zeros_like(m.st_ref)
        levels.append(_store_masks(m.mask_ref, m.hierarchical))
        if m.hierarchical and m.n_sub > 1:
            for _, kt_ref, _, _ in m.slots:
                kt_ref[...] = jnp.zeros_like(kt_ref)

    def rows_of(n, forward):
        return _chunk_rows(n if forward else n_chunks - 1 - n)

    def chains(m):
        return enumerate((gi, forward) for gi in range(m.n_groups) for forward in (True, False))

    def stage(n, slot):
        for m in mixers:
            qt_ref, kt_ref, p0_ref, dec_ref = m.slots[slot]
            lane_masks = _lane_masks(m.n_sub)
            for c, (gi, forward) in chains(m):
                q, k, factors = m.prep(gi, forward, rows_of(n, forward))
                qb, kb = q.astype(BF16), k.astype(BF16)
                if forward:
                    for i in range(m.n_sub):
                        qm = q if lane_masks[i] is None else q * lane_masks[i]
                        diag = jnp.sum(qm * k, axis=-1, keepdims=True).astype(BF16)
                        p0_ref[gi * m.n_sub + i] = m.mask_ref[0] * diag
                for level, (fq, fk, chunk_decay) in enumerate(factors):
                    qt_ref[c, level] = qb * fq.astype(BF16)
                    if chunk_decay is not None:
                        dec_ref[c] = chunk_decay
                    if m.hierarchical and chunk_decay is None:
                        kt = (kb if fk is None else kb * fk.astype(BF16)).T
                        width = LANES // m.n_sub
                        for i in range(m.n_sub):
                            kt_ref[c, level, i, i * width:(i + 1) * width, 0:CHUNK] = kt[i * width:(i + 1) * width, :]
                    else:
                        kt_ref[c, level, 0, 0:CHUNK, :] = kb * fk.astype(BF16)

    def consume(n, slot):
        for m, n_levels in zip(mixers, levels):
            qt_ref, kt_ref, p0_ref, dec_ref = m.slots[slot]
            lane_masks = _lane_masks(m.n_sub)
            for c, (gi, forward) in chains(m):
                rows = rows_of(n, forward)
                o_ref = m.of_ref if forward else m.ob_ref
                mask0 = 1 if forward else 1 + n_levels
                for i, v in enumerate(m.load_v(gi, rows)):
                    head = gi * m.n_sub + i
                    sub = (lambda t: t) if lane_masks[i] is None else (lambda t, lm=lane_masks[i].astype(BF16): t * lm)
                    p = p0_ref[head] if forward else None
                    for level in range(n_levels):
                        if m.hierarchical:
                            s = _dot(qt_ref[c, level], kt_ref[c, level, i, :, 0:CHUNK])
                        else:
                            s = _dot_nt(sub(qt_ref[c, level]), kt_ref[c, level, 0, 0:CHUNK, :])
                        s = s.astype(BF16) * m.mask_ref[mask0 + level]
                        p = s if p is None else p + s
                    st = m.st_ref[c * m.n_sub + i]
                    inter = _dot_nt(sub(qt_ref[c, n_levels]), st.astype(BF16))
                    o_ref[rows, head * DV:(head + 1) * DV] = (_dot(p, v) + inter).astype(o_ref.dtype)
                    m.st_ref[c * m.n_sub + i] = st * dec_ref[c] + _dot_tn(v, kt_ref[c, n_levels, 0, 0:CHUNK, :])

    def body(step, carry):
        n = 2 * step
        consume(n, 0)
        stage(n + 1, 1)
        consume(n + 1, 1)
        stage(jnp.minimum(n + 2, n_chunks - 1), 0)
        return carry

    stage(0, 0)
    lax.fori_loop(0, n_chunks // 2, body, 0)


def _finish(mixer, gain_ref, gate_ref, y_ref, center, swish):
    rows_per = 256
    n_blocks = mixer.of_ref.shape[0] // rows_per

    def body(b, carry):
        rows = pl.ds(pl.multiple_of(b * rows_per, rows_per), rows_per)
        for i in range(mixer.n_groups * mixer.n_sub):
            cols = slice(i * DV, (i + 1) * DV)
            o = mixer.of_ref[rows, cols].astype(F32) + mixer.ob_ref[rows, cols].astype(F32)
            if center:
                o = o - jnp.mean(o, axis=-1, keepdims=True)
            o = o * lax.rsqrt(jnp.mean(o * o, axis=-1, keepdims=True) + EPS)
            gz = gate_ref[rows, cols].astype(F32)
            act = gz * _sigmoid(gz) if swish else _sigmoid(gz)
            y_ref[rows, cols] = ((o * gain_ref[:, cols]) * act).astype(y_ref.dtype)
        return carry

    lax.fori_loop(0, n_blocks, body, 0)


def _group_cols(gi):
    return slice(gi * LANES, (gi + 1) * LANES)


N_SLOTS = 2
N_MIXER_SCRATCH = 4 + 4 * N_SLOTS


def _mixer(n_groups, n_sub, hierarchical, prep, load_v, scratch):
    of_ref, ob_ref, st_ref, mask_ref = scratch[:4]
    return _Mixer(n_groups, n_sub, hierarchical, prep, load_v, of_ref, ob_ref, st_ref, mask_ref,
                  tuple(scratch[4 + 4 * s:8 + 4 * s] for s in range(N_SLOTS)))


def _mixer_scratch(seqlen, n_groups, n_sub, hierarchical):
    n_heads, n_chains = n_groups * n_sub, 2 * n_groups
    n_levels = N_LEVELS if hierarchical else 1
    key_copies = n_sub if hierarchical else 1
    slot = [pltpu.VMEM((n_chains, n_levels + 1, CHUNK, LANES), BF16),
            pltpu.VMEM((n_chains, n_levels + 1, key_copies, LANES, LANES), BF16),
            pltpu.VMEM((n_heads, CHUNK, CHUNK), BF16),
            pltpu.VMEM((n_chains, 1, LANES), F32)]
    return [pltpu.VMEM((seqlen, n_heads * DV), BF16),
            pltpu.VMEM((seqlen, n_heads * DV), BF16),
            pltpu.VMEM((2 * n_heads, DV, LANES), F32),
            pltpu.VMEM((1 + 2 * n_levels, CHUNK, CHUNK), BF16)] + N_SLOTS * slot


def _hgrn2_mixer(q_ref, v_ref, ff_ref, fb_ref, lb_ref, scratch):
    def prep(gi, forward, rows):
        cols = _group_cols(gi)
        lb = lb_ref[:, cols]
        z = (ff_ref if forward else fb_ref)[rows, cols]
        f = lb + (1.0 - lb) * _sigmoid(z)
        g2 = jnp.log2(jnp.maximum(f, TINY))
        key = 1.0 - f
        aq = q_ref[rows, cols].astype(F32)
        return aq * _sigmoid(aq), key, _gated_factors(g2, forward)

    return _mixer(q_ref.shape[1] // LANES, 1, True, prep,
                  lambda gi, rows: [v_ref[rows, _group_cols(gi)]], scratch)


def _rotate_half(x):
    lane = lax.broadcasted_iota(jnp.int32, x.shape, 1)
    half = BC_DK // 2
    first = (lane % BC_DK) < half
    return jnp.where(first, pltpu.roll(x, LANES - half, axis=1), pltpu.roll(x, half, axis=1))


def _sub_values(v_ref, gi, rows):
    return [v_ref[rows, (2 * gi + i) * DV:(2 * gi + i + 1) * DV] for i in range(2)]


def _retention_mixer(q_ref, k_ref, v_ref, cos_ref, sin_ref, lgf_ref, lgb_ref, scratch):
    n_groups = q_ref.shape[1] // LANES
    packs = [(_fixed_factors(lgb_ref[:, _group_cols(gi)], False), _fixed_factors(lgf_ref[:, _group_cols(gi)], True))
             for gi in range(n_groups)]

    def prep(gi, forward, rows):
        cols = _group_cols(gi)
        cos = cos_ref[rows, :]
        sin = sin_ref[rows, :]
        q = q_ref[rows, cols].astype(F32)
        k = k_ref[rows, cols].astype(F32)
        q = q * cos + _rotate_half(q) * sin
        k = (k * cos + _rotate_half(k) * sin) * (BC_DK ** -0.5)
        return q, k, packs[gi][1 if forward else 0]

    return _mixer(n_groups, 2, False, prep, functools.partial(_sub_values, v_ref), scratch)


def _gla_mixer(q_ref, k_ref, v_ref, lr_ref, waf_ref, wab_ref, baf_ref, bab_ref, scratch):
    def prep(gi, forward, rows):
        cols = _group_cols(gi)
        w_ref, b_ref = (waf_ref, baf_ref) if forward else (wab_ref, bab_ref)
        x = _dot(lr_ref[rows, :].astype(BF16), w_ref[:, cols]) + b_ref[:, cols]
        log_sig = jnp.minimum(x, 0.0) - jnp.log(1.0 + jnp.exp(-jnp.abs(x)))
        g2 = log_sig * (LOG2_E / GLA_TAU)
        q = q_ref[rows, cols].astype(F32) * (BC_DK ** -0.5)
        return q, k_ref[rows, cols].astype(F32), _gated_factors(g2, forward)

    return _mixer(q_ref.shape[1] // LANES, 2, True, prep, functools.partial(_sub_values, v_ref), scratch)


def _cols(ref, start, width):
    return ref.at[:, start:start + width]


def _mixers_kernel(zb_ref, zf_ref, lr_ref, lb_ref, again_ref,
                   cos_ref, sin_ref, lgf_ref, lgb_ref, bgain_ref,
                   waf_ref, wab_ref, baf_ref, bab_ref, cgain_ref,
                   ya_ref, yb_ref, yc_ref, *scratch):
    pair, qk = 2 * DV, 2 * BC_DK
    s1, s2 = N_MIXER_SCRATCH, 2 * N_MIXER_SCRATCH
    hgrn2 = _hgrn2_mixer(_cols(zb_ref, PAIR_AQ, pair), _cols(zb_ref, PAIR_AI, pair),
                         _cols(zf_ref, PAIR_FF, pair), _cols(zf_ref, PAIR_FB, pair), lb_ref, scratch[:s1])
    retention = _retention_mixer(_cols(zb_ref, PAIR_BQ, qk), _cols(zb_ref, PAIR_BK, qk),
                                 _cols(zb_ref, PAIR_BV, pair), cos_ref, sin_ref, lgf_ref, lgb_ref, scratch[s1:s2])
    gla = _gla_mixer(_cols(zb_ref, PAIR_CQ, qk), _cols(zb_ref, PAIR_CK, qk), _cols(zb_ref, PAIR_CV, pair),
                     lr_ref, waf_ref, wab_ref, baf_ref, bab_ref, scratch[s2:])
    _scan_mixers(zb_ref.shape[0] // CHUNK, [hgrn2, retention, gla])
    _finish(hgrn2, again_ref, _cols(zb_ref, PAIR_AG, pair), ya_ref, center=False, swish=False)
    _finish(retention, bgain_ref, _cols(zb_ref, PAIR_BG, pair), yb_ref, center=True, swish=True)
    _finish(gla, cgain_ref, _cols(zb_ref, PAIR_CG, pair), yc_ref, center=False, swish=True)


def _zspec(seqlen, width, col0):
    blk = col0 // width
    return pl.BlockSpec((seqlen, width), lambda b, g: (b, blk + g))


def _vspec(width, blk0=0):
    return pl.BlockSpec((1, width), lambda b, g: (0, blk0 + g))


def _mixers(zb, zf, lb, gain_a, cos, sin, lgf, lgb, gain_b, wal, bal, gain_c, bsz, seqlen):
    t, w = seqlen, 2 * DV
    n_steps = N_HEADS * DV // w
    half = N_HEADS * BC_DK // LANES
    out = pl.BlockSpec((t, w), lambda b, g: (b, g))
    table = pl.BlockSpec((t, LANES), lambda b, g: (0, 0), pipeline_mode=pl.Buffered(1))
    return pl.pallas_call(
        _mixers_kernel,
        grid=(bsz, n_steps),
        in_specs=[_zspec(t, ZB_PAIR_COLS, 0), _zspec(t, ZF_PAIR_COLS, 0),
                  pl.BlockSpec((t, LANES), lambda b, g: (b, ZF_LR // LANES)),
                  _vspec(w), _vspec(w),
                  table, table, _vspec(LANES), _vspec(LANES), _vspec(w),
                  pl.BlockSpec((LANES, LANES), lambda b, g: (0, g)),
                  pl.BlockSpec((LANES, LANES), lambda b, g: (0, half + g)),
                  _vspec(LANES), _vspec(LANES, half), _vspec(w)],
        out_specs=[out, out, out],
        out_shape=[jax.ShapeDtypeStruct((bsz * t, n_steps * w), BF16)] * 3,
        scratch_shapes=(_mixer_scratch(t, 2, 1, True) + _mixer_scratch(t, 1, 2, False)
                        + _mixer_scratch(t, 1, 2, True)),
        compiler_params=pltpu.CompilerParams(dimension_semantics=("parallel", "parallel"),
                                             vmem_limit_bytes=VMEM_LIMIT_MIXERS),
        name="mixers",
    )(zb, zf, zf, lb, gain_a, cos, sin, lgf, lgb, gain_b, wal, wal, bal, bal, gain_c)


def _split_w_in(w):
    o = {}
    off = 0
    for name, size in (("a_q", 512), ("a_ff", 512), ("a_fb", 512), ("a_i", 512), ("a_g", 512),
                       ("b_q", 256), ("b_k", 256), ("b_v", 512), ("b_g", 512),
                       ("c_q", 256), ("c_k", 256), ("c_v", 512), ("c_g", 512), ("c_lr", 32),
                       ("gate_a", 1024), ("gate_b", 1024), ("gate_c", 1024)):
        o[name] = w[:, off:off + size]
        off += size
    n_pairs = N_HEADS // 2
    part = lambda name, g: o[name][:, g * (o[name].shape[1] // n_pairs):(g + 1) * (o[name].shape[1] // n_pairs)]
    pairs = [part(n, g) for g in range(n_pairs)
             for n in ("a_q", "a_i", "a_g", "c_q", "c_k", "c_v", "c_g", "b_q", "b_k", "b_v", "b_g")]
    wb = jnp.concatenate(pairs + [o["gate_a"], o["gate_b"], o["gate_c"]], axis=1)
    pad = jnp.zeros((w.shape[0], ZF_COLS - ZF_LR - 2 * GLA_RANK), w.dtype)
    gates = [part(n, g) for g in range(n_pairs) for n in ("a_ff", "a_fb")]
    wf = jnp.concatenate(gates + [o["c_lr"], pad], axis=1)
    return wb.astype(BF16), wf.astype(BF16)


def _rotary_tables(seqlen):
    pos = jnp.arange(seqlen, dtype=F32)
    inv_freq = 10000.0 ** (-jnp.arange(0, BC_DK, 2, dtype=F32) / BC_DK)
    ang = pos[:, None] * inv_freq[None, :]
    cos, sin = jnp.cos(ang), jnp.sin(ang)
    reps = LANES // BC_DK
    return (jnp.tile(jnp.concatenate([cos, cos], axis=1), (1, reps)),
            jnp.tile(jnp.concatenate([-sin, sin], axis=1), (1, reps)))


def _retention_log_decays():
    h = jnp.arange(N_HEADS, dtype=F32)
    fwd = jnp.log1p(-jnp.exp2(-5.0 - h))
    bwd = jnp.log1p(-jnp.exp2(-5.0 - h[::-1]))
    spread = lambda v: jnp.repeat(v, BC_DK)[None, :]
    return spread(fwd), spread(bwd)


def kernel(x, c, norm1_g, w_ada, b_ada, w_in, lb_logits, norm_a_g, norm_b_g, norm_c_g, w_alpha, b_alpha,
           w_pa, w_pb, w_pc, w_out, norm2_g, w_ffn_in, w_ffn_out, norm_f_g):
    bsz, seqlen, d = x.shape
    depth = w_in.shape[0]
    assert d == D_MODEL and seqlen % (2 * CHUNK) == 0
    tm = min(TM_DENSE, seqlen)
    tm_in = min(TM_INPROJ, seqlen)
    assert seqlen % tm == 0 and seqlen % tm_in == 0

    mod = _modulation(c, w_ada, b_ada)
    lbs = _lower_bounds(lb_logits)
    cos, sin = _rotary_tables(seqlen)
    lgf, lgb = _retention_log_decays()

    x2 = x.reshape(bsz * seqlen, d)
    for l in range(depth):
        sh1, sc1, g1, sh2, sc2, g2 = [mod[l, :, i * d:(i + 1) * d].reshape(bsz, 1, d) for i in range(6)]
        wb, wf = _split_w_in(w_in[l])
        gain1 = norm1_g[l].reshape(1, d)
        zb, zf = _inproj(x2, sh1, sc1, gain1, wb, wf, seqlen, tm_in)

        wal = jnp.zeros((LANES, 2 * N_HEADS * BC_DK), F32)
        wal = wal.at[0:GLA_RANK, 0:N_HEADS * BC_DK].set(w_alpha[l, 0])
        wal = wal.at[GLA_RANK:2 * GLA_RANK, N_HEADS * BC_DK:].set(w_alpha[l, 1])
        bal = b_alpha[l].reshape(1, 2 * N_HEADS * BC_DK)

        ya, yb, yc = _mixers(zb, zf, lbs[l].reshape(1, -1), norm_a_g[l].reshape(1, -1),
                             cos, sin, lgf, lgb, norm_b_g[l].reshape(1, -1),
                             wal.astype(BF16), bal, norm_c_g[l].reshape(1, -1), bsz, seqlen)

        x2 = _merge(x2, ya, yb, yc, zb, w_pa[l].astype(BF16), w_pb[l].astype(BF16), w_pc[l].astype(BF16),
                    w_out[l].astype(BF16), g1, seqlen, tm)
        x2 = _ffn(x2, sh2, sc2, g2, norm2_g[l].reshape(1, d), w_ffn_in[l].astype(BF16),
                  w_ffn_out[l].astype(BF16), norm_f_g.reshape(1, d), seqlen, tm,
                  final_norm=(l == depth - 1))
    return x2.reshape(bsz, seqlen, d)
```

```python
import functools
from typing import Any, Callable, NamedTuple

import jax
import jax.numpy as jnp
from jax import lax
from jax.experimental import pallas as pl
from jax.experimental.pallas import tpu as pltpu

F32 = jnp.float32
BF16 = jnp.bfloat16

D_MODEL = 1024
N_HEADS = 4
BC_DK = 64
DV = 128
GLA_RANK = 16
GLA_TAU = 16.0
D_FF = 2816
EPS = 1e-6
TINY = 1e-30
LOG2_E = 1.4426950408889634

LANES = 128
CHUNK = 128
assert CHUNK <= LANES
VMEM_LIMIT = 48 * 1024 * 1024
VMEM_LIMIT_MIXERS = 58 * 1024 * 1024

TM_INPROJ, TN_INPROJ = 512, 1280
TM_DENSE = 512
TF_FFN = D_FF // 2

ZB_PAIR_COLS = 2304
PAIR_AQ, PAIR_AI, PAIR_AG = 0, 256, 512
PAIR_CQ, PAIR_CK, PAIR_CV, PAIR_CG = 768, 896, 1024, 1280
PAIR_BQ, PAIR_BK, PAIR_BV, PAIR_BG = 1536, 1664, 1792, 2048
ZB_GATE = 4608
ZB_GATE_BLOCK = 1536
ZF_PAIR_COLS = 512
PAIR_FF, PAIR_FB = 0, 256
ZF_LR = 1024
ZF_COLS = 1152


def _dot(a, b):
    return jnp.dot(a, b, preferred_element_type=F32)


def _dot_nt(a, b):
    return lax.dot_general(a, b, (((1,), (1,)), ((), ())), preferred_element_type=F32)


def _dot_tn(a, b):
    return lax.dot_general(a, b, (((0,), (0,)), ((), ())), preferred_element_type=F32)


def _sigmoid(x):
    return jax.nn.sigmoid(x)


def _mod_kernel(c_ref, w_ref, b_ref, o_ref):
    c = c_ref[...]
    a = c * _sigmoid(c)
    w = w_ref[...]
    a_hi = a.astype(BF16)
    a_lo = (a - a_hi.astype(F32)).astype(BF16)
    w_hi = w.astype(BF16)
    w_lo = (w - w_hi.astype(F32)).astype(BF16)
    o_ref[...] = _dot(a_hi, w_hi) + _dot(a_hi, w_lo) + _dot(a_lo, w_hi) + b_ref[...]


def _modulation(c, w_ada, b_ada):
    depth, d, n6 = w_ada.shape
    bsz = c.shape[0]
    return pl.pallas_call(
        _mod_kernel,
        grid=(depth, n6 // d),
        in_specs=[
            pl.BlockSpec((bsz, d), lambda l, j: (0, 0)),
            pl.BlockSpec((None, d, d), lambda l, j: (l, 0, j)),
            pl.BlockSpec((None, 1, d), lambda l, j: (l, 0, j)),
        ],
        out_specs=pl.BlockSpec((None, bsz, d), lambda l, j: (l, 0, j)),
        out_shape=jax.ShapeDtypeStruct((depth, bsz, n6), F32),
        name="adaln_modulation",
    )(c, w_ada, b_ada.reshape(depth, 1, n6))


def _lb_kernel(x_ref, o_ref):
    depth = x_ref.shape[0]
    rows = [x_ref[i:i + 1, :] for i in range(depth)]
    m = rows[0]
    for r in rows[1:]:
        m = jnp.maximum(m, r)
    e = [jnp.exp(r - m) for r in rows]
    s = e[0]
    for t in e[1:]:
        s = s + t
    p = [t / s for t in e]
    acc = p[0]
    o_ref[0:1, :] = jnp.maximum(acc - p[0], 0.0)
    for i in range(1, depth):
        acc = acc + p[i]
        o_ref[i:i + 1, :] = jnp.maximum(acc - p[0], 0.0)


def _lower_bounds(lb_logits):
    return pl.pallas_call(
        _lb_kernel,
        out_shape=jax.ShapeDtypeStruct(lb_logits.shape, F32),
        name="hgrn2_lower_bounds",
    )(lb_logits)


def _norm_mod(x, gain, sc, sh):
    y = x * lax.rsqrt(jnp.mean(x * x, axis=-1, keepdims=True) + EPS)
    return (y * gain) * (1.0 + sc) + sh


def _inproj_kernel(x_ref, sh_ref, sc_ref, g_ref, wb_ref, wf_ref, zb_ref, zf_ref, h_ref):
    h_ref[...] = _norm_mod(x_ref[...], g_ref[...], sc_ref[0], sh_ref[0]).astype(BF16)
    for j in range(zb_ref.shape[1] // TN_INPROJ):
        cols = slice(j * TN_INPROJ, (j + 1) * TN_INPROJ)
        zb_ref[:, cols] = _dot(h_ref[...], wb_ref[:, cols]).astype(zb_ref.dtype)
    zf_ref[...] = _dot(h_ref[...], wf_ref[...])


def _inproj(x2, sh, sc, gain, wb, wf, seqlen, tm):
    m, d = x2.shape
    per = seqlen // tm
    bvec = lambda i: (i // per, 0, 0)
    resident = lambda w: pl.BlockSpec(w.shape, lambda i: (0, 0), pipeline_mode=pl.Buffered(1))
    return pl.pallas_call(
        _inproj_kernel,
        grid=(m // tm,),
        in_specs=[
            pl.BlockSpec((tm, d), lambda i: (i, 0)),
            pl.BlockSpec((1, 1, d), bvec), pl.BlockSpec((1, 1, d), bvec),
            pl.BlockSpec((1, d), lambda i: (0, 0)),
            resident(wb), resident(wf),
        ],
        out_specs=[pl.BlockSpec((tm, wb.shape[1]), lambda i: (i, 0)),
                   pl.BlockSpec((tm, wf.shape[1]), lambda i: (i, 0))],
        out_shape=[jax.ShapeDtypeStruct((m, wb.shape[1]), BF16), jax.ShapeDtypeStruct((m, wf.shape[1]), F32)],
        scratch_shapes=[pltpu.VMEM((tm, d), BF16)],
        compiler_params=pltpu.CompilerParams(dimension_semantics=("parallel",), vmem_limit_bytes=VMEM_LIMIT),
        name="inproj",
    )(x2, sh, sc, gain, wb, wf)


def _merge_kernel(x_ref, ya_ref, yb_ref, yc_ref, gates_lo_ref, gates_hi_ref,
                  wpa_ref, wpb_ref, wpc_ref, wout_ref, g1_ref, o_ref):
    d = x_ref.shape[1]
    gates = jnp.concatenate([gates_lo_ref[...], gates_hi_ref[...]], axis=1).astype(F32)
    merged = (_sigmoid(gates[:, 0:d]) * _dot(ya_ref[...], wpa_ref[...])
              + _sigmoid(gates[:, d:2 * d]) * _dot(yb_ref[...], wpb_ref[...])
              + _sigmoid(gates[:, 2 * d:3 * d]) * _dot(yc_ref[...], wpc_ref[...]))
    out = _dot(merged.astype(BF16), wout_ref[...])
    o_ref[...] = x_ref[...] + g1_ref[0] * out


def _merge(x2, ya, yb, yc, zb, wpa, wpb, wpc, wout, g1, seqlen, tm):
    m, d = x2.shape
    w = ya.shape[1]
    per = seqlen // tm
    row = lambda i: (i, 0)
    const = lambda i: (0, 0)
    return pl.pallas_call(
        _merge_kernel,
        grid=(m // tm,),
        in_specs=[
            pl.BlockSpec((tm, d), row),
            pl.BlockSpec((tm, w), row), pl.BlockSpec((tm, w), row), pl.BlockSpec((tm, w), row),
            pl.BlockSpec((tm, ZB_GATE_BLOCK), lambda i: (i, ZB_GATE // ZB_GATE_BLOCK)),
            pl.BlockSpec((tm, ZB_GATE_BLOCK), lambda i: (i, ZB_GATE // ZB_GATE_BLOCK + 1)),
            pl.BlockSpec((w, d), const), pl.BlockSpec((w, d), const), pl.BlockSpec((w, d), const),
            pl.BlockSpec((d, d), const),
            pl.BlockSpec((1, 1, d), lambda i: (i // per, 0, 0)),
        ],
        out_specs=pl.BlockSpec((tm, d), row),
        out_shape=jax.ShapeDtypeStruct((m, d), F32),
        compiler_params=pltpu.CompilerParams(
            dimension_semantics=("parallel",), vmem_limit_bytes=VMEM_LIMIT),
        name="merge_outproj",
    )(x2, ya, yb, yc, zb, zb, wpa, wpb, wpc, wout, g1)


def _ffn_kernel(x_ref, sh_ref, sc_ref, g2_ref, ng_ref, wi_ref, wo_ref, nf_ref, o_ref, h_ref, *, final_norm):
    dff = wo_ref.shape[0]
    h_ref[...] = _norm_mod(x_ref[...], ng_ref[...], sc_ref[0], sh_ref[0]).astype(BF16)
    acc = None
    for k in range(dff // TF_FFN):
        gate = _dot(h_ref[...], wi_ref[:, k * TF_FFN:(k + 1) * TF_FFN])
        up = _dot(h_ref[...], wi_ref[:, dff + k * TF_FFN:dff + (k + 1) * TF_FFN])
        act = (gate * _sigmoid(gate) * up).astype(BF16)
        part = _dot(act, wo_ref[k * TF_FFN:(k + 1) * TF_FFN, :])
        acc = part if acc is None else acc + part
    xn = x_ref[...] + g2_ref[0] * acc
    if final_norm:
        xn = xn * lax.rsqrt(jnp.mean(xn * xn, axis=-1, keepdims=True) + EPS) * nf_ref[...]
    o_ref[...] = xn


def _ffn(x2, sh, sc, g2, ng, w_in, w_out, nf, seqlen, tm, final_norm):
    m, d = x2.shape
    per = seqlen // tm
    bvec = lambda i: (i // per, 0, 0)
    resident = lambda w: pl.BlockSpec(w.shape, lambda i: (0, 0), pipeline_mode=pl.Buffered(1))
    return pl.pallas_call(
        functools.partial(_ffn_kernel, final_norm=final_norm),
        grid=(m // tm,),
        in_specs=[
            pl.BlockSpec((tm, d), lambda i: (i, 0)),
            pl.BlockSpec((1, 1, d), bvec), pl.BlockSpec((1, 1, d), bvec), pl.BlockSpec((1, 1, d), bvec),
            pl.BlockSpec((1, d), lambda i: (0, 0)),
            resident(w_in), resident(w_out),
            pl.BlockSpec((1, d), lambda i: (0, 0)),
        ],
        out_specs=pl.BlockSpec((tm, d), lambda i: (i, 0)),
        out_shape=jax.ShapeDtypeStruct((m, d), F32),
        scratch_shapes=[pltpu.VMEM((tm, d), BF16)],
        compiler_params=pltpu.CompilerParams(dimension_semantics=("parallel",), vmem_limit_bytes=VMEM_LIMIT),
        name="swiglu_ffn",
    )(x2, sh, sc, g2, ng, w_in, w_out, nf)


N_LEVELS = CHUNK.bit_length() - 1


def _gated_factors(g2, forward):
    row = lax.broadcasted_iota(jnp.int32, g2.shape, 0)
    yield jnp.exp2(g2), None, None
    lb, tb = g2, g2
    half = CHUNK // 2
    b = 1
    while b < half:
        right = (row & b) != 0
        from_left = pltpu.roll(tb, b, axis=0)
        from_right = pltpu.roll(tb, CHUNK - b, axis=0)
        lb = lb + jnp.where(right, from_left, 0.0)
        tb = tb + jnp.where(right, from_left, from_right)
        b *= 2
        rest = tb - lb
        fq, fk = (jnp.exp2(lb), jnp.exp2(rest)) if forward else (jnp.exp2(rest + g2), jnp.exp2(lb - g2))
        yield fq, fk, None
    decay_lo, decay_hi = jnp.exp2(tb[0:1, :]), jnp.exp2(tb[half:half + 1, :])
    after_lo = jnp.where(row >= half, decay_lo, 1.0)
    before_hi = jnp.where(row < half, decay_hi, 1.0)
    if forward:
        yield fq * after_lo, fk * before_hi, decay_lo * decay_hi
    else:
        yield fq * before_hi, fk * after_lo, decay_lo * decay_hi


def _fixed_factors(log_decay, forward):
    pos = lax.broadcasted_iota(jnp.int32, (CHUNK, LANES), 0).astype(F32)
    mid = CHUNK // 2
    chunk_decay = jnp.exp(float(CHUNK) * log_decay)
    if forward:
        return [(jnp.exp((pos - (mid - 1.0)) * log_decay), jnp.exp(((mid - 1.0) - pos) * log_decay), None),
                (jnp.exp((pos + 1.0) * log_decay), jnp.exp((CHUNK - 1.0 - pos) * log_decay), chunk_decay)]
    return [(jnp.exp((mid - pos) * log_decay), jnp.exp((pos - mid) * log_decay), None),
            (jnp.exp((CHUNK - pos) * log_decay), jnp.exp(pos * log_decay), chunk_decay)]


def _store_masks(mask_ref, hierarchical):
    row = lax.broadcasted_iota(jnp.int32, (CHUNK, CHUNK), 0)
    col = lax.broadcasted_iota(jnp.int32, (CHUNK, CHUNK), 1)
    one = lambda m: jnp.where(m, 1.0, 0.0).astype(BF16)
    mask_ref[0] = one(row == col)
    n = N_LEVELS if hierarchical else 1
    for level in range(n):
        same_level = ((row ^ col) >> level) == 1 if hierarchical else (row != col)
        mask_ref[1 + level] = one(same_level & (row > col))
        mask_ref[1 + n + level] = one(same_level & (col > row))
    return n


def _lane_masks(n_sub):
    if n_sub == 1:
        return [None]
    lane = lax.broadcasted_iota(jnp.int32, (1, LANES), 1)
    width = LANES // n_sub
    return [jnp.where((lane >= i * width) & (lane < (i + 1) * width), 1.0, 0.0) for i in range(n_sub)]


def _chunk_rows(n):
    if isinstance(n, int):
        return pl.ds(n * CHUNK, CHUNK)
    return pl.ds(pl.multiple_of(n * CHUNK, CHUNK), CHUNK)


class _Mixer(NamedTuple):
    n_groups: int
    n_sub: int
    hierarchical: bool
    prep: Callable
    load_v: Callable
    of_ref: Any
    ob_ref: Any
    st_ref: Any
    mask_ref: Any
    slots: tuple


def _scan_mixers(n_chunks, mixers):
    levels = []
    for m in mixers:
        m.st_ref[...] = jnp.zeros_like(m.st_ref)
        levels.append(_store_masks(m.mask_ref, m.hierarchical))
        if m.hierarchical and m.n_sub > 1:
            for _, kt_ref, _, _ in m.slots:
                kt_ref[...] = jnp.zeros_like(kt_ref)

    def rows_of(n, forward):
        return _chunk_rows(n if forward else n_chunks - 1 - n)

    def chains(m):
        return enumerate((gi, forward) for gi in range(m.n_groups) for forward in (True, False))

    def stage(n, slot):
        for m in mixers:
            qt_ref, kt_ref, p0_ref, dec_ref = m.slots[slot]
            lane_masks = _lane_masks(m.n_sub)
            for c, (gi, forward) in chains(m):
                q, k, factors = m.prep(gi, forward, rows_of(n, forward))
                qb, kb = q.astype(BF16), k.astype(BF16)
                if forward:
                    for i in range(m.n_sub):
                        qm = q if lane_masks[i] is None else q * lane_masks[i]
                        diag = jnp.sum(qm * k, axis=-1, keepdims=True).astype(BF16)
                        p0_ref[gi * m.n_sub + i] = m.mask_ref[0] * diag
                for level, (fq, fk, chunk_decay) in enumerate(factors):
                    qt_ref[c, level] = qb * fq.astype(BF16)
                    if chunk_decay is not None:
                        dec_ref[c] = chunk_decay
                    if m.hierarchical and chunk_decay is None:
                        kt = (kb if fk is None else kb * fk.astype(BF16)).T
                        width = LANES // m.n_sub
                        for i in range(m.n_sub):
                            kt_ref[c, level, i, i * width:(i + 1) * width, 0:CHUNK] = kt[i * width:(i + 1) * width, :]
                    else:
                        kt_ref[c, level, 0, 0:CHUNK, :] = kb * fk.astype(BF16)

    def consume(n, slot):
        for m, n_levels in zip(mixers, levels):
            qt_ref, kt_ref, p0_ref, dec_ref = m.slots[slot]
            lane_masks = _lane_masks(m.n_sub)
            for c, (gi, forward) in chains(m):
                rows = rows_of(n, forward)
                o_ref = m.of_ref if forward else m.ob_ref
                mask0 = 1 if forward else 1 + n_levels
                for i, v in enumerate(m.load_v(gi, rows)):
                    head = gi * m.n_sub + i
                    sub = (lambda t: t) if lane_masks[i] is None else (lambda t, lm=lane_masks[i].astype(BF16): t * lm)
                    p = p0_ref[head] if forward else None
                    for level in range(n_levels):
                        if m.hierarchical:
                            s = _dot(qt_ref[c, level], kt_ref[c, level, i, :, 0:CHUNK])
                        else:
                            s = _dot_nt(sub(qt_ref[c, level]), kt_ref[c, level, 0, 0:CHUNK, :])
                        s = s.astype(BF16) * m.mask_ref[mask0 + level]
                        p = s if p is None else p + s
                    st = m.st_ref[c * m.n_sub + i]
                    inter = _dot_nt(sub(qt_ref[c, n_levels]), st.astype(BF16))
                    o_ref[rows, head * DV:(head + 1) * DV] = (_dot(p, v) + inter).astype(o_ref.dtype)
                    m.st_ref[c * m.n_sub + i] = st * dec_ref[c] + _dot_tn(v, kt_ref[c, n_levels, 0, 0:CHUNK, :])

    def body(n, carry):
        nxt = jnp.minimum(n + 1, n_chunks - 1)

        @pl.when(n % 2 == 0)
        def _():
            consume(n, 0)
            stage(nxt, 1)

        @pl.when(n % 2 == 1)
        def _():
            consume(n, 1)
            stage(nxt, 0)

        return carry

    stage(0, 0)
    lax.fori_loop(0, n_chunks, body, 0)


def _finish(mixer, gain_ref, gate_ref, y_ref, center, swish):
    rows_per = 256
    n_blocks = mixer.of_ref.shape[0] // rows_per

    def body(b, carry):
        rows = pl.ds(pl.multiple_of(b * rows_per, rows_per), rows_per)
        for i in range(mixer.n_groups * mixer.n_sub):
            cols = slice(i * DV, (i + 1) * DV)
            o = mixer.of_ref[rows, cols].astype(F32) + mixer.ob_ref[rows, cols].astype(F32)
            if center:
                o = o - jnp.mean(o, axis=-1, keepdims=True)
            o = o * lax.rsqrt(jnp.mean(o * o, axis=-1, keepdims=True) + EPS)
            gz = gate_ref[rows, cols].astype(F32)
            act = gz * _sigmoid(gz) if swish else _sigmoid(gz)
            y_ref[rows, cols] = ((o * gain_ref[:, cols]) * act).astype(y_ref.dtype)
        return carry

    lax.fori_loop(0, n_blocks, body, 0)


def _group_cols(gi):
    return slice(gi * LANES, (gi + 1) * LANES)


N_SLOTS = 2
N_MIXER_SCRATCH = 4 + 4 * N_SLOTS


def _mixer(n_groups, n_sub, hierarchical, prep, load_v, scratch):
    of_ref, ob_ref, st_ref, mask_ref = scratch[:4]
    return _Mixer(n_groups, n_sub, hierarchical, prep, load_v, of_ref, ob_ref, st_ref, mask_ref,
                  tuple(scratch[4 + 4 * s:8 + 4 * s] for s in range(N_SLOTS)))


def _mixer_scratch(seqlen, n_groups, n_sub, hierarchical):
    n_heads, n_chains = n_groups * n_sub, 2 * n_groups
    n_levels = N_LEVELS if hierarchical else 1
    key_copies = n_sub if hierarchical else 1
    slot = [pltpu.VMEM((n_chains, n_levels + 1, CHUNK, LANES), BF16),
            pltpu.VMEM((n_chains, n_levels + 1, key_copies, LANES, LANES), BF16),
            pltpu.VMEM((n_heads, CHUNK, CHUNK), BF16),
            pltpu.VMEM((n_chains, 1, LANES), F32)]
    return [pltpu.VMEM((seqlen, n_heads * DV), BF16),
            pltpu.VMEM((seqlen, n_heads * DV), BF16),
            pltpu.VMEM((2 * n_heads, DV, LANES), F32),
            pltpu.VMEM((1 + 2 * n_levels, CHUNK, CHUNK), BF16)] + N_SLOTS * slot


def _hgrn2_mixer(q_ref, v_ref, ff_ref, fb_ref, lb_ref, scratch):
    def prep(gi, forward, rows):
        cols = _group_cols(gi)
        lb = lb_ref[:, cols]
        z = (ff_ref if forward else fb_ref)[rows, cols]
        f = lb + (1.0 - lb) * _sigmoid(z)
        g2 = jnp.log2(jnp.maximum(f, TINY))
        key = 1.0 - f
        aq = q_ref[rows, cols].astype(F32)
        return aq * _sigmoid(aq), key, _gated_factors(g2, forward)

    return _mixer(q_ref.shape[1] // LANES, 1, True, prep,
                  lambda gi, rows: [v_ref[rows, _group_cols(gi)]], scratch)


def _rotate_half(x):
    lane = lax.broadcasted_iota(jnp.int32, x.shape, 1)
    half = BC_DK // 2
    first = (lane % BC_DK) < half
    return jnp.where(first, pltpu.roll(x, LANES - half, axis=1), pltpu.roll(x, half, axis=1))


def _sub_values(v_ref, gi, rows):
    return [v_ref[rows, (2 * gi + i) * DV:(2 * gi + i + 1) * DV] for i in range(2)]


def _retention_mixer(q_ref, k_ref, v_ref, cos_ref, sin_ref, lgf_ref, lgb_ref, scratch):
    n_groups = q_ref.shape[1] // LANES
    packs = [(_fixed_factors(lgb_ref[:, _group_cols(gi)], False), _fixed_factors(lgf_ref[:, _group_cols(gi)], True))
             for gi in range(n_groups)]

    def prep(gi, forward, rows):
        cols = _group_cols(gi)
        cos = cos_ref[rows, :]
        sin = sin_ref[rows, :]
        q = q_ref[rows, cols].astype(F32)
        k = k_ref[rows, cols].astype(F32)
        q = q * cos + _rotate_half(q) * sin
        k = (k * cos + _rotate_half(k) * sin) * (BC_DK ** -0.5)
        return q, k, packs[gi][1 if forward else 0]

    return _mixer(n_groups, 2, False, prep, functools.partial(_sub_values, v_ref), scratch)


def _gla_mixer(q_ref, k_ref, v_ref, lr_ref, waf_ref, wab_ref, baf_ref, bab_ref, scratch):
    def prep(gi, forward, rows):
        cols = _group_cols(gi)
        w_ref, b_ref = (waf_ref, baf_ref) if forward else (wab_ref, bab_ref)
        x = _dot(lr_ref[rows, :].astype(BF16), w_ref[:, cols]) + b_ref[:, cols]
        log_sig = jnp.minimum(x, 0.0) - jnp.log(1.0 + jnp.exp(-jnp.abs(x)))
        g2 = log_sig * (LOG2_E / GLA_TAU)
        q = q_ref[rows, cols].astype(F32) * (BC_DK ** -0.5)
        return q, k_ref[rows, cols].astype(F32), _gated_factors(g2, forward)

    return _mixer(q_ref.shape[1] // LANES, 2, True, prep, functools.partial(_sub_values, v_ref), scratch)


def _cols(ref, start, width):
    return ref.at[:, start:start + width]


def _mixers_kernel(zb_ref, zf_ref, lr_ref, lb_ref, again_ref,
                   cos_ref, sin_ref, lgf_ref, lgb_ref, bgain_ref,
                   waf_ref, wab_ref, baf_ref, bab_ref, cgain_ref,
                   ya_ref, yb_ref, yc_ref, *scratch):
    pair, qk = 2 * DV, 2 * BC_DK
    s1, s2 = N_MIXER_SCRATCH, 2 * N_MIXER_SCRATCH
    hgrn2 = _hgrn2_mixer(_cols(zb_ref, PAIR_AQ, pair), _cols(zb_ref, PAIR_AI, pair),
                         _cols(zf_ref, PAIR_FF, pair), _cols(zf_ref, PAIR_FB, pair), lb_ref, scratch[:s1])
    retention = _retention_mixer(_cols(zb_ref, PAIR_BQ, qk), _cols(zb_ref, PAIR_BK, qk),
                                 _cols(zb_ref, PAIR_BV, pair), cos_ref, sin_ref, lgf_ref, lgb_ref, scratch[s1:s2])
    gla = _gla_mixer(_cols(zb_ref, PAIR_CQ, qk), _cols(zb_ref, PAIR_CK, qk), _cols(zb_ref, PAIR_CV, pair),
                     lr_ref, waf_ref, wab_ref, baf_ref, bab_ref, scratch[s2:])
    _scan_mixers(zb_ref.shape[0] // CHUNK, [hgrn2, retention, gla])
    _finish(hgrn2, again_ref, _cols(zb_ref, PAIR_AG, pair), ya_ref, center=False, swish=False)
    _finish(retention, bgain_ref, _cols(zb_ref, PAIR_BG, pair), yb_ref, center=True, swish=True)
    _finish(gla, cgain_ref, _cols(zb_ref, PAIR_CG, pair), yc_ref, center=False, swish=True)


def _zspec(seqlen, width, col0):
    blk = col0 // width
    return pl.BlockSpec((seqlen, width), lambda b, g: (b, blk + g))


def _vspec(width, blk0=0):
    return pl.BlockSpec((1, width), lambda b, g: (0, blk0 + g))


def _mixers(zb, zf, lb, gain_a, cos, sin, lgf, lgb, gain_b, wal, bal, gain_c, bsz, seqlen):
    t, w = seqlen, 2 * DV
    n_steps = N_HEADS * DV // w
    half = N_HEADS * BC_DK // LANES
    out = pl.BlockSpec((t, w), lambda b, g: (b, g))
    table = pl.BlockSpec((t, LANES), lambda b, g: (0, 0), pipeline_mode=pl.Buffered(1))
    return pl.pallas_call(
        _mixers_kernel,
        grid=(bsz, n_steps),
        in_specs=[_zspec(t, ZB_PAIR_COLS, 0), _zspec(t, ZF_PAIR_COLS, 0),
                  pl.BlockSpec((t, LANES), lambda b, g: (b, ZF_LR // LANES)),
                  _vspec(w), _vspec(w),
                  table, table, _vspec(LANES), _vspec(LANES), _vspec(w),
                  pl.BlockSpec((LANES, LANES), lambda b, g: (0, g)),
                  pl.BlockSpec((LANES, LANES), lambda b, g: (0, half + g)),
                  _vspec(LANES), _vspec(LANES, half), _vspec(w)],
        out_specs=[out, out, out],
        out_shape=[jax.ShapeDtypeStruct((bsz * t, n_steps * w), BF16)] * 3,
        scratch_shapes=(_mixer_scratch(t, 2, 1, True) + _mixer_scratch(t, 1, 2, False)
                        + _mixer_scratch(t, 1, 2, True)),
        compiler_params=pltpu.CompilerParams(dimension_semantics=("parallel", "parallel"),
                                             vmem_limit_bytes=VMEM_LIMIT_MIXERS),
        name="mixers",
    )(zb, zf, zf, lb, gain_a, cos, sin, lgf, lgb, gain_b, wal, wal, bal, bal, gain_c)


def _split_w_in(w):
    o = {}
    off = 0
    for name, size in (("a_q", 512), ("a_ff", 512), ("a_fb", 512), ("a_i", 512), ("a_g", 512),
                       ("b_q", 256), ("b_k", 256), ("b_v", 512), ("b_g", 512),
                       ("c_q", 256), ("c_k", 256), ("c_v", 512), ("c_g", 512), ("c_lr", 32),
                       ("gate_a", 1024), ("gate_b", 1024), ("gate_c", 1024)):
        o[name] = w[:, off:off + size]
        off += size
    n_pairs = N_HEADS // 2
    part = lambda name, g: o[name][:, g * (o[name].shape[1] // n_pairs):(g + 1) * (o[name].shape[1] // n_pairs)]
    pairs = [part(n, g) for g in range(n_pairs)
             for n in ("a_q", "a_i", "a_g", "c_q", "c_k", "c_v", "c_g", "b_q", "b_k", "b_v", "b_g")]
    wb = jnp.concatenate(pairs + [o["gate_a"], o["gate_b"], o["gate_c"]], axis=1)
    pad = jnp.zeros((w.shape[0], ZF_COLS - ZF_LR - 2 * GLA_RANK), w.dtype)
    gates = [part(n, g) for g in range(n_pairs) for n in ("a_ff", "a_fb")]
    wf = jnp.concatenate(gates + [o["c_lr"], pad], axis=1)
    return wb.astype(BF16), wf.astype(BF16)


def _rotary_tables(seqlen):
    pos = jnp.arange(seqlen, dtype=F32)
    inv_freq = 10000.0 ** (-jnp.arange(0, BC_DK, 2, dtype=F32) / BC_DK)
    ang = pos[:, None] * inv_freq[None, :]
    cos, sin = jnp.cos(ang), jnp.sin(ang)
    reps = LANES // BC_DK
    return (jnp.tile(jnp.concatenate([cos, cos], axis=1), (1, reps)),
            jnp.tile(jnp.concatenate([-sin, sin], axis=1), (1, reps)))


def _retention_log_decays():
    h = jnp.arange(N_HEADS, dtype=F32)
    fwd = jnp.log1p(-jnp.exp2(-5.0 - h))
    bwd = jnp.log1p(-jnp.exp2(-5.0 - h[::-1]))
    spread = lambda v: jnp.repeat(v, BC_DK)[None, :]
    return spread(fwd), spread(bwd)


def kernel(x, c, norm1_g, w_ada, b_ada, w_in, lb_logits, norm_a_g, norm_b_g, norm_c_g, w_alpha, b_alpha,
           w_pa, w_pb, w_pc, w_out, norm2_g, w_ffn_in, w_ffn_out, norm_f_g):
    bsz, seqlen, d = x.shape
    depth = w_in.shape[0]
    assert d == D_MODEL and seqlen % (2 * CHUNK) == 0
    tm = min(TM_DENSE, seqlen)
    tm_in = min(TM_INPROJ, seqlen)
    assert seqlen % tm == 0 and seqlen % tm_in == 0

    mod = _modulation(c, w_ada, b_ada)
    lbs = _lower_bounds(lb_logits)
    cos, sin = _rotary_tables(seqlen)
    lgf, lgb = _retention_log_decays()

    x2 = x.reshape(bsz * seqlen, d)
    for l in range(depth):
        sh1, sc1, g1, sh2, sc2, g2 = [mod[l, :, i * d:(i + 1) * d].reshape(bsz, 1, d) for i in range(6)]
        wb, wf = _split_w_in(w_in[l])
        gain1 = norm1_g[l].reshape(1, d)
        zb, zf = _inproj(x2, sh1, sc1, gain1, wb, wf, seqlen, tm_in)

        wal = jnp.zeros((LANES, 2 * N_HEADS * BC_DK), F32)
        wal = wal.at[0:GLA_RANK, 0:N_HEADS * BC_DK].set(w_alpha[l, 0])
        wal = wal.at[GLA_RANK:2 * GLA_RANK, N_HEADS * BC_DK:].set(w_alpha[l, 1])
        bal = b_alpha[l].reshape(1, 2 * N_HEADS * BC_DK)

        ya, yb, yc = _mixers(zb, zf, lbs[l].reshape(1, -1), norm_a_g[l].reshape(1, -1),
                             cos, sin, lgf, lgb, norm_b_g[l].reshape(1, -1),
                             wal.astype(BF16), bal, norm_c_g[l].reshape(1, -1), bsz, seqlen)

        x2 = _merge(x2, ya, yb, yc, zb, w_pa[l].astype(BF16), w_pb[l].astype(BF16), w_pc[l].astype(BF16),
                    w_out[l].astype(BF16), g1, seqlen, tm)
        x2 = _ffn(x2, sh2, sc2, g2, norm2_g[l].reshape(1, d), w_ffn_in[l].astype(BF16),
                  w_ffn_out[l].astype(BF16), norm_f_g.reshape(1, d), seqlen, tm,
                  final_norm=(l == depth - 1))
    return x2.reshape(bsz, seqlen, d)
```

```python
import functools
from typing import Any, Callable, NamedTuple

import jax
import jax.numpy as jnp
from jax import lax
from jax.experimental import pallas as pl
from jax.experimental.pallas import tpu as pltpu

F32 = jnp.float32
BF16 = jnp.bfloat16

D_MODEL = 1024
N_HEADS = 4
BC_DK = 64
DV = 128
GLA_RANK = 16
GLA_TAU = 16.0
D_FF = 2816
EPS = 1e-6
TINY = 1e-30
LOG2_E = 1.4426950408889634

LANES = 128
CHUNK = 128
assert CHUNK <= LANES
VMEM_LIMIT = 48 * 1024 * 1024
VMEM_LIMIT_MIXERS = 58 * 1024 * 1024

TM_INPROJ, TN_INPROJ = 512, 1280
TM_DENSE = 512
TF_FFN = D_FF // 2

ZB_PAIR_COLS = 2304
PAIR_AQ, PAIR_AI, PAIR_AG = 0, 256, 512
PAIR_CQ, PAIR_CK, PAIR_CV, PAIR_CG = 768, 896, 1024, 1280
PAIR_BQ, PAIR_BK, PAIR_BV, PAIR_BG = 1536, 1664, 1792, 2048
ZB_GATE = 4608
ZB_GATE_BLOCK = 1536
ZF_PAIR_COLS = 512
PAIR_FF, PAIR_FB = 0, 256
ZF_LR = 1024
ZF_COLS = 1152


def _dot(a, b):
    return jnp.dot(a, b, preferred_element_type=F32)


def _dot_nt(a, b):
    return lax.dot_general(a, b, (((1,), (1,)), ((), ())), preferred_element_type=F32)


def _dot_tn(a, b):
    return lax.dot_general(a, b, (((0,), (0,)), ((), ())), preferred_element_type=F32)


def _sigmoid(x):
    return jax.nn.sigmoid(x)


def _mod_kernel(c_ref, w_ref, b_ref, o_ref):
    c = c_ref[...]
    a = c * _sigmoid(c)
    w = w_ref[...]
    a_hi = a.astype(BF16)
    a_lo = (a - a_hi.astype(F32)).astype(BF16)
    w_hi = w.astype(BF16)
    w_lo = (w - w_hi.astype(F32)).astype(BF16)
    o_ref[...] = _dot(a_hi, w_hi) + _dot(a_hi, w_lo) + _dot(a_lo, w_hi) + b_ref[...]


def _modulation(c, w_ada, b_ada):
    depth, d, n6 = w_ada.shape
    bsz = c.shape[0]
    return pl.pallas_call(
        _mod_kernel,
        grid=(depth, n6 // d),
        in_specs=[
            pl.BlockSpec((bsz, d), lambda l, j: (0, 0)),
            pl.BlockSpec((None, d, d), lambda l, j: (l, 0, j)),
            pl.BlockSpec((None, 1, d), lambda l, j: (l, 0, j)),
        ],
        out_specs=pl.BlockSpec((None, bsz, d), lambda l, j: (l, 0, j)),
        out_shape=jax.ShapeDtypeStruct((depth, bsz, n6), F32),
        name="adaln_modulation",
    )(c, w_ada, b_ada.reshape(depth, 1, n6))


def _lb_kernel(x_ref, o_ref):
    depth = x_ref.shape[0]
    rows = [x_ref[i:i + 1, :] for i in range(depth)]
    m = rows[0]
    for r in rows[1:]:
        m = jnp.maximum(m, r)
    e = [jnp.exp(r - m) for r in rows]
    s = e[0]
    for t in e[1:]:
        s = s + t
    p = [t / s for t in e]
    acc = p[0]
    o_ref[0:1, :] = jnp.maximum(acc - p[0], 0.0)
    for i in range(1, depth):
        acc = acc + p[i]
        o_ref[i:i + 1, :] = jnp.maximum(acc - p[0], 0.0)


def _lower_bounds(lb_logits):
    return pl.pallas_call(
        _lb_kernel,
        out_shape=jax.ShapeDtypeStruct(lb_logits.shape, F32),
        name="hgrn2_lower_bounds",
    )(lb_logits)


def _norm_mod(x, gain, sc, sh):
    y = x * lax.rsqrt(jnp.mean(x * x, axis=-1, keepdims=True) + EPS)
    return (y * gain) * (1.0 + sc) + sh


def _inproj_kernel(x_ref, sh_ref, sc_ref, g_ref, wb_ref, wf_ref, zb_ref, zf_ref, h_ref):
    h_ref[...] = _norm_mod(x_ref[...], g_ref[...], sc_ref[0], sh_ref[0]).astype(BF16)
    for j in range(zb_ref.shape[1] // TN_INPROJ):
        cols = slice(j * TN_INPROJ, (j + 1) * TN_INPROJ)
        zb_ref[:, cols] = _dot(h_ref[...], wb_ref[:, cols]).astype(zb_ref.dtype)
    zf_ref[...] = _dot(h_ref[...], wf_ref[...])


def _inproj(x2, sh, sc, gain, wb, wf, seqlen, tm):
    m, d = x2.shape
    per = seqlen // tm
    bvec = lambda i: (i // per, 0, 0)
    resident = lambda w: pl.BlockSpec(w.shape, lambda i: (0, 0), pipeline_mode=pl.Buffered(1))
    return pl.pallas_call(
        _inproj_kernel,
        grid=(m // tm,),
        in_specs=[
            pl.BlockSpec((tm, d), lambda i: (i, 0)),
            pl.BlockSpec((1, 1, d), bvec), pl.BlockSpec((1, 1, d), bvec),
            pl.BlockSpec((1, d), lambda i: (0, 0)),
            resident(wb), resident(wf),
        ],
        out_specs=[pl.BlockSpec((tm, wb.shape[1]), lambda i: (i, 0)),
                   pl.BlockSpec((tm, wf.shape[1]), lambda i: (i, 0))],
        out_shape=[jax.ShapeDtypeStruct((m, wb.shape[1]), BF16), jax.ShapeDtypeStruct((m, wf.shape[1]), F32)],
        scratch_shapes=[pltpu.VMEM((tm, d), BF16)],
        compiler_params=pltpu.CompilerParams(dimension_semantics=("parallel",), vmem_limit_bytes=VMEM_LIMIT),
        name="inproj",
    )(x2, sh, sc, gain, wb, wf)


def _merge_ffn_kernel(x_ref, ya_ref, yb_ref, yc_ref, gates_lo_ref, gates_hi_ref,
                      wpa_ref, wpb_ref, wpc_ref, wout_ref, g1_ref,
                      sh_ref, sc_ref, g2_ref, ng_ref, wi_ref, wo_ref, nf_ref, o_ref, h_ref, *, final_norm):
    d = x_ref.shape[1]
    dff = wo_ref.shape[0]
    gates = jnp.concatenate([gates_lo_ref[...], gates_hi_ref[...]], axis=1).astype(F32)
    merged = (_sigmoid(gates[:, 0:d]) * _dot(ya_ref[...], wpa_ref[...])
              + _sigmoid(gates[:, d:2 * d]) * _dot(yb_ref[...], wpb_ref[...])
              + _sigmoid(gates[:, 2 * d:3 * d]) * _dot(yc_ref[...], wpc_ref[...]))
    x = x_ref[...] + g1_ref[0] * _dot(merged.astype(BF16), wout_ref[...])
    h_ref[...] = _norm_mod(x, ng_ref[...], sc_ref[0], sh_ref[0]).astype(BF16)
    acc = None
    for k in range(dff // TF_FFN):
        gate = _dot(h_ref[...], wi_ref[:, k * TF_FFN:(k + 1) * TF_FFN])
        up = _dot(h_ref[...], wi_ref[:, dff + k * TF_FFN:dff + (k + 1) * TF_FFN])
        act = (gate * _sigmoid(gate) * up).astype(BF16)
        part = _dot(act, wo_ref[k * TF_FFN:(k + 1) * TF_FFN, :])
        acc = part if acc is None else acc + part
    xn = x + g2_ref[0] * acc
    if final_norm:
        xn = xn * lax.rsqrt(jnp.mean(xn * xn, axis=-1, keepdims=True) + EPS) * nf_ref[...]
    o_ref[...] = xn


def _merge_ffn(x2, ya, yb, yc, zb, wpa, wpb, wpc, wout, g1, sh, sc, g2, ng, w_in, w_out, nf, seqlen, tm,
               final_norm):
    m, d = x2.shape
    w = ya.shape[1]
    per = seqlen // tm
    row = lambda i: (i, 0)
    bvec = pl.BlockSpec((1, 1, d), lambda i: (i // per, 0, 0))
    vec = pl.BlockSpec((1, d), lambda i: (0, 0))
    resident = lambda a: pl.BlockSpec(a.shape, lambda i: (0, 0), pipeline_mode=pl.Buffered(1))
    return pl.pallas_call(
        functools.partial(_merge_ffn_kernel, final_norm=final_norm),
        grid=(m // tm,),
        in_specs=[
            pl.BlockSpec((tm, d), row),
            pl.BlockSpec((tm, w), row), pl.BlockSpec((tm, w), row), pl.BlockSpec((tm, w), row),
            pl.BlockSpec((tm, ZB_GATE_BLOCK), lambda i: (i, ZB_GATE // ZB_GATE_BLOCK)),
            pl.BlockSpec((tm, ZB_GATE_BLOCK), lambda i: (i, ZB_GATE // ZB_GATE_BLOCK + 1)),
            resident(wpa), resident(wpb), resident(wpc), resident(wout), bvec,
            bvec, bvec, bvec, vec, resident(w_in), resident(w_out), vec,
        ],
        out_specs=pl.BlockSpec((tm, d), row),
        out_shape=jax.ShapeDtypeStruct((m, d), F32),
        scratch_shapes=[pltpu.VMEM((tm, d), BF16)],
        compiler_params=pltpu.CompilerParams(dimension_semantics=("parallel",),
                                             vmem_limit_bytes=VMEM_LIMIT_MIXERS),
        name="merge_ffn",
    )(x2, ya, yb, yc, zb, zb, wpa, wpb, wpc, wout, g1, sh, sc, g2, ng, w_in, w_out, nf)


N_LEVELS = CHUNK.bit_length() - 1


def _gated_factors(g2, forward):
    row = lax.broadcasted_iota(jnp.int32, g2.shape, 0)
    yield jnp.exp2(g2), None, None
    lb, tb = g2, g2
    half = CHUNK // 2
    b = 1
    while b < half:
        right = (row & b) != 0
        from_left = pltpu.roll(tb, b, axis=0)
        from_right = pltpu.roll(tb, CHUNK - b, axis=0)
        lb = lb + jnp.where(right, from_left, 0.0)
        tb = tb + jnp.where(right, from_left, from_right)
        b *= 2
        rest = tb - lb
        fq, fk = (jnp.exp2(lb), jnp.exp2(rest)) if forward else (jnp.exp2(rest + g2), jnp.exp2(lb - g2))
        yield fq, fk, None
    decay_lo, decay_hi = jnp.exp2(tb[0:1, :]), jnp.exp2(tb[half:half + 1, :])
    after_lo = jnp.where(row >= half, decay_lo, 1.0)
    before_hi = jnp.where(row < half, decay_hi, 1.0)
    if forward:
        yield fq * after_lo, fk * before_hi, decay_lo * decay_hi
    else:
        yield fq * before_hi, fk * after_lo, decay_lo * decay_hi


def _fixed_factors(log_decay, forward):
    pos = lax.broadcasted_iota(jnp.int32, (CHUNK, LANES), 0).astype(F32)
    mid = CHUNK // 2
    chunk_decay = jnp.exp(float(CHUNK) * log_decay)
    if forward:
        return [(jnp.exp((pos - (mid - 1.0)) * log_decay), jnp.exp(((mid - 1.0) - pos) * log_decay), None),
                (jnp.exp((pos + 1.0) * log_decay), jnp.exp((CHUNK - 1.0 - pos) * log_decay), chunk_decay)]
    return [(jnp.exp((mid - pos) * log_decay), jnp.exp((pos - mid) * log_decay), None),
            (jnp.exp((CHUNK - pos) * log_decay), jnp.exp(pos * log_decay), chunk_decay)]


def _store_masks(mask_ref, hierarchical):
    row = lax.broadcasted_iota(jnp.int32, (CHUNK, CHUNK), 0)
    col = lax.broadcasted_iota(jnp.int32, (CHUNK, CHUNK), 1)
    one = lambda m: jnp.where(m, 1.0, 0.0).astype(BF16)
    mask_ref[0] = one(row == col)
    n = N_LEVELS if hierarchical else 1
    for level in range(n):
        same_level = ((row ^ col) >> level) == 1 if hierarchical else (row != col)
        mask_ref[1 + level] = one(same_level & (row > col))
        mask_ref[1 + n + level] = one(same_level & (col > row))
    return n


def _lane_masks(n_sub):
    if n_sub == 1:
        return [None]
    lane = lax.broadcasted_iota(jnp.int32, (1, LANES), 1)
    width = LANES // n_sub
    return [jnp.where((lane >= i * width) & (lane < (i + 1) * width), 1.0, 0.0) for i in range(n_sub)]


def _chunk_rows(n):
    if isinstance(n, int):
        return pl.ds(n * CHUNK, CHUNK)
    return pl.ds(pl.multiple_of(n * CHUNK, CHUNK), CHUNK)


class _Mixer(NamedTuple):
    n_groups: int
    n_sub: int
    hierarchical: bool
    prep: Callable
    load_v: Callable
    of_ref: Any
    ob_ref: Any
    st_ref: Any
    mask_ref: Any
    slots: tuple


def _scan_mixers(n_chunks, mixers):
    levels = []
    for m in mixers:
        m.st_ref[...] = jnp.zeros_like(m.st_ref)
        levels.append(_store_masks(m.mask_ref, m.hierarchical))
        if m.hierarchical and m.n_sub > 1:
            for _, kt_ref, _, _ in m.slots:
                kt_ref[...] = jnp.zeros_like(kt_ref)

    def rows_of(n, forward):
        return _chunk_rows(n if forward else n_chunks - 1 - n)

    def chains(m):
        return enumerate((gi, forward) for gi in range(m.n_groups) for forward in (True, False))

    def stage(n, slot):
        for m in mixers:
            qt_ref, kt_ref, p0_ref, dec_ref = m.slots[slot]
            lane_masks = _lane_masks(m.n_sub)
            for c, (gi, forward) in chains(m):
                q, k, factors = m.prep(gi, forward, rows_of(n, forward))
                qb, kb = q.astype(BF16), k.astype(BF16)
                if forward:
                    for i in range(m.n_sub):
                        qm = q if lane_masks[i] is None else q * lane_masks[i]
                        diag = jnp.sum(qm * k, axis=-1, keepdims=True).astype(BF16)
                        p0_ref[gi * m.n_sub + i] = m.mask_ref[0] * diag
                for level, (fq, fk, chunk_decay) in enumerate(factors):
                    qt_ref[c, level] = qb * fq.astype(BF16)
                    if chunk_decay is not None:
                        dec_ref[c] = chunk_decay
                    if m.hierarchical and chunk_decay is None:
                        kt = (kb if fk is None else kb * fk.astype(BF16)).T
                        width = LANES // m.n_sub
                        for i in range(m.n_sub):
                            kt_ref[c, level, i, i * width:(i + 1) * width, 0:CHUNK] = kt[i * width:(i + 1) * width, :]
                    else:
                        kt_ref[c, level, 0, 0:CHUNK, :] = kb * fk.astype(BF16)

    def consume(n, slot):
        for m, n_levels in zip(mixers, levels):
            qt_ref, kt_ref, p0_ref, dec_ref = m.slots[slot]
            lane_masks = _lane_masks(m.n_sub)
            for c, (gi, forward) in chains(m):
                rows = rows_of(n, forward)
                o_ref = m.of_ref if forward else m.ob_ref
                mask0 = 1 if forward else 1 + n_levels
                for i, v in enumerate(m.load_v(gi, rows)):
                    head = gi * m.n_sub + i
                    sub = (lambda t: t) if lane_masks[i] is None else (lambda t, lm=lane_masks[i].astype(BF16): t * lm)
                    p = p0_ref[head] if forward else None
                    for level in range(n_levels):
                        if m.hierarchical:
                            s = _dot(qt_ref[c, level], kt_ref[c, level, i, :, 0:CHUNK])
                        else:
                            s = _dot_nt(sub(qt_ref[c, level]), kt_ref[c, level, 0, 0:CHUNK, :])
                        s = s.astype(BF16) * m.mask_ref[mask0 + level]
                        p = s if p is None else p + s
                    st = m.st_ref[c * m.n_sub + i]
                    inter = _dot_nt(sub(qt_ref[c, n_levels]), st.astype(BF16))
                    o_ref[rows, head * DV:(head + 1) * DV] = (_dot(p, v) + inter).astype(o_ref.dtype)
                    m.st_ref[c * m.n_sub + i] = st * dec_ref[c] + _dot_tn(v, kt_ref[c, n_levels, 0, 0:CHUNK, :])

    def body(step, carry):
        n = 2 * step
        consume(n, 0)
        stage(n + 1, 1)
        consume(n + 1, 1)
        stage(jnp.minimum(n + 2, n_chunks - 1), 0)
        return carry

    stage(0, 0)
    lax.fori_loop(0, n_chunks // 2, body, 0)


def _finish(mixer, gain_ref, gate_ref, y_ref, center, swish):
    rows_per = 256
    n_blocks = mixer.of_ref.shape[0] // rows_per

    def body(b, carry):
        rows = pl.ds(pl.multiple_of(b * rows_per, rows_per), rows_per)
        for i in range(mixer.n_groups * mixer.n_sub):
            cols = slice(i * DV, (i + 1) * DV)
            o = mixer.of_ref[rows, cols].astype(F32) + mixer.ob_ref[rows, cols].astype(F32)
            if center:
                o = o - jnp.mean(o, axis=-1, keepdims=True)
            o = o * lax.rsqrt(jnp.mean(o * o, axis=-1, keepdims=True) + EPS)
            gz = gate_ref[rows, cols].astype(F32)
            act = gz * _sigmoid(gz) if swish else _sigmoid(gz)
            y_ref[rows, cols] = ((o * gain_ref[:, cols]) * act).astype(y_ref.dtype)
        return carry

    lax.fori_loop(0, n_blocks, body, 0)


def _group_cols(gi):
    return slice(gi * LANES, (gi + 1) * LANES)


N_SLOTS = 2
N_MIXER_SCRATCH = 4 + 4 * N_SLOTS


def _mixer(n_groups, n_sub, hierarchical, prep, load_v, scratch):
    of_ref, ob_ref, st_ref, mask_ref = scratch[:4]
    return _Mixer(n_groups, n_sub, hierarchical, prep, load_v, of_ref, ob_ref, st_ref, mask_ref,
                  tuple(scratch[4 + 4 * s:8 + 4 * s] for s in range(N_SLOTS)))


def _mixer_scratch(seqlen, n_groups, n_sub, hierarchical):
    n_heads, n_chains = n_groups * n_sub, 2 * n_groups
    n_levels = N_LEVELS if hierarchical else 1
    key_copies = n_sub if hierarchical else 1
    slot = [pltpu.VMEM((n_chains, n_levels + 1, CHUNK, LANES), BF16),
            pltpu.VMEM((n_chains, n_levels + 1, key_copies, LANES, LANES), BF16),
            pltpu.VMEM((n_heads, CHUNK, CHUNK), BF16),
            pltpu.VMEM((n_chains, 1, LANES), F32)]
    return [pltpu.VMEM((seqlen, n_heads * DV), BF16),
            pltpu.VMEM((seqlen, n_heads * DV), BF16),
            pltpu.VMEM((2 * n_heads, DV, LANES), F32),
            pltpu.VMEM((1 + 2 * n_levels, CHUNK, CHUNK), BF16)] + N_SLOTS * slot


def _hgrn2_mixer(q_ref, v_ref, ff_ref, fb_ref, lb_ref, scratch):
    def prep(gi, forward, rows):
        cols = _group_cols(gi)
        lb = lb_ref[:, cols]
        z = (ff_ref if forward else fb_ref)[rows, cols]
        f = lb + (1.0 - lb) * _sigmoid(z)
        g2 = jnp.log2(jnp.maximum(f, TINY))
        key = 1.0 - f
        aq = q_ref[rows, cols].astype(F32)
        return aq * _sigmoid(aq), key, _gated_factors(g2, forward)

    return _mixer(q_ref.shape[1] // LANES, 1, True, prep,
                  lambda gi, rows: [v_ref[rows, _group_cols(gi)]], scratch)


def _rotate_half(x):
    lane = lax.broadcasted_iota(jnp.int32, x.shape, 1)
    half = BC_DK // 2
    first = (lane % BC_DK) < half
    return jnp.where(first, pltpu.roll(x, LANES - half, axis=1), pltpu.roll(x, half, axis=1))


def _sub_values(v_ref, gi, rows):
    return [v_ref[rows, (2 * gi + i) * DV:(2 * gi + i + 1) * DV] for i in range(2)]


def _retention_mixer(q_ref, k_ref, v_ref, cos_ref, sin_ref, lgf_ref, lgb_ref, scratch):
    n_groups = q_ref.shape[1] // LANES
    packs = [(_fixed_factors(lgb_ref[:, _group_cols(gi)], False), _fixed_factors(lgf_ref[:, _group_cols(gi)], True))
             for gi in range(n_groups)]

    def prep(gi, forward, rows):
        cols = _group_cols(gi)
        cos = cos_ref[rows, :]
        sin = sin_ref[rows, :]
        q = q_ref[rows, cols].astype(F32)
        k = k_ref[rows, cols].astype(F32)
        q = q * cos + _rotate_half(q) * sin
        k = (k * cos + _rotate_half(k) * sin) * (BC_DK ** -0.5)
        return q, k, packs[gi][1 if forward else 0]

    return _mixer(n_groups, 2, False, prep, functools.partial(_sub_values, v_ref), scratch)


def _gla_mixer(q_ref, k_ref, v_ref, lr_ref, waf_ref, wab_ref, baf_ref, bab_ref, scratch):
    def prep(gi, forward, rows):
        cols = _group_cols(gi)
        w_ref, b_ref = (waf_ref, baf_ref) if forward else (wab_ref, bab_ref)
        x = _dot(lr_ref[rows, :].astype(BF16), w_ref[:, cols]) + b_ref[:, cols]
        log_sig = jnp.minimum(x, 0.0) - jnp.log(1.0 + jnp.exp(-jnp.abs(x)))
        g2 = log_sig * (LOG2_E / GLA_TAU)
        q = q_ref[rows, cols].astype(F32) * (BC_DK ** -0.5)
        return q, k_ref[rows, cols].astype(F32), _gated_factors(g2, forward)

    return _mixer(q_ref.shape[1] // LANES, 2, True, prep, functools.partial(_sub_values, v_ref), scratch)


def _cols(ref, start, width):
    return ref.at[:, start:start + width]


def _mixers_kernel(zb_ref, zf_ref, lr_ref, lb_ref, again_ref,
                   cos_ref, sin_ref, lgf_ref, lgb_ref, bgain_ref,
                   waf_ref, wab_ref, baf_ref, bab_ref, cgain_ref,
                   ya_ref, yb_ref, yc_ref, *scratch):
    pair, qk = 2 * DV, 2 * BC_DK
    s1, s2 = N_MIXER_SCRATCH, 2 * N_MIXER_SCRATCH
    hgrn2 = _hgrn2_mixer(_cols(zb_ref, PAIR_AQ, pair), _cols(zb_ref, PAIR_AI, pair),
                         _cols(zf_ref, PAIR_FF, pair), _cols(zf_ref, PAIR_FB, pair), lb_ref, scratch[:s1])
    retention = _retention_mixer(_cols(zb_ref, PAIR_BQ, qk), _cols(zb_ref, PAIR_BK, qk),
                                 _cols(zb_ref, PAIR_BV, pair), cos_ref, sin_ref, lgf_ref, lgb_ref, scratch[s1:s2])
    gla = _gla_mixer(_cols(zb_ref, PAIR_CQ, qk), _cols(zb_ref, PAIR_CK, qk), _cols(zb_ref, PAIR_CV, pair),
                     lr_ref, waf_ref, wab_ref, baf_ref, bab_ref, scratch[s2:])
    _scan_mixers(zb_ref.shape[0] // CHUNK, [hgrn2, retention, gla])
    _finish(hgrn2, again_ref, _cols(zb_ref, PAIR_AG, pair), ya_ref, center=False, swish=False)
    _finish(retention, bgain_ref, _cols(zb_ref, PAIR_BG, pair), yb_ref, center=True, swish=True)
    _finish(gla, cgain_ref, _cols(zb_ref, PAIR_CG, pair), yc_ref, center=False, swish=True)


def _zspec(seqlen, width, col0):
    blk = col0 // width
    return pl.BlockSpec((seqlen, width), lambda b, g: (b, blk + g))


def _vspec(width, blk0=0):
    return pl.BlockSpec((1, width), lambda b, g: (0, blk0 + g))


def _mixers(zb, zf, lb, gain_a, cos, sin, lgf, lgb, gain_b, wal, bal, gain_c, bsz, seqlen):
    t, w = seqlen, 2 * DV
    n_steps = N_HEADS * DV // w
    half = N_HEADS * BC_DK // LANES
    out = pl.BlockSpec((t, w), lambda b, g: (b, g))
    table = pl.BlockSpec((t, LANES), lambda b, g: (0, 0), pipeline_mode=pl.Buffered(1))
    return pl.pallas_call(
        _mixers_kernel,
        grid=(bsz, n_steps),
        in_specs=[_zspec(t, ZB_PAIR_COLS, 0), _zspec(t, ZF_PAIR_COLS, 0),
                  pl.BlockSpec((t, LANES), lambda b, g: (b, ZF_LR // LANES)),
                  _vspec(w), _vspec(w),
                  table, table, _vspec(LANES), _vspec(LANES), _vspec(w),
                  pl.BlockSpec((LANES, LANES), lambda b, g: (0, g)),
                  pl.BlockSpec((LANES, LANES), lambda b, g: (0, half + g)),
                  _vspec(LANES), _vspec(LANES, half), _vspec(w)],
        out_specs=[out, out, out],
        out_shape=[jax.ShapeDtypeStruct((bsz * t, n_steps * w), BF16)] * 3,
        scratch_shapes=(_mixer_scratch(t, 2, 1, True) + _mixer_scratch(t, 1, 2, False)
                        + _mixer_scratch(t, 1, 2, True)),
        compiler_params=pltpu.CompilerParams(dimension_semantics=("parallel", "parallel"),
                                             vmem_limit_bytes=VMEM_LIMIT_MIXERS),
        name="mixers",
    )(zb, zf, zf, lb, gain_a, cos, sin, lgf, lgb, gain_b, wal, wal, bal, bal, gain_c)


def _split_w_in(w):
    o = {}
    off = 0
    for name, size in (("a_q", 512), ("a_ff", 512), ("a_fb", 512), ("a_i", 512), ("a_g", 512),
                       ("b_q", 256), ("b_k", 256), ("b_v", 512), ("b_g", 512),
                       ("c_q", 256), ("c_k", 256), ("c_v", 512), ("c_g", 512), ("c_lr", 32),
                       ("gate_a", 1024), ("gate_b", 1024), ("gate_c", 1024)):
        o[name] = w[:, off:off + size]
        off += size
    n_pairs = N_HEADS // 2
    part = lambda name, g: o[name][:, g * (o[name].shape[1] // n_pairs):(g + 1) * (o[name].shape[1] // n_pairs)]
    pairs = [part(n, g) for g in range(n_pairs)
             for n in ("a_q", "a_i", "a_g", "c_q", "c_k", "c_v", "c_g", "b_q", "b_k", "b_v", "b_g")]
    wb = jnp.concatenate(pairs + [o["gate_a"], o["gate_b"], o["gate_c"]], axis=1)
    pad = jnp.zeros((w.shape[0], ZF_COLS - ZF_LR - 2 * GLA_RANK), w.dtype)
    gates = [part(n, g) for g in range(n_pairs) for n in ("a_ff", "a_fb")]
    wf = jnp.concatenate(gates + [o["c_lr"], pad], axis=1)
    return wb.astype(BF16), wf.astype(BF16)


def _rotary_tables(seqlen):
    pos = jnp.arange(seqlen, dtype=F32)
    inv_freq = 10000.0 ** (-jnp.arange(0, BC_DK, 2, dtype=F32) / BC_DK)
    ang = pos[:, None] * inv_freq[None, :]
    cos, sin = jnp.cos(ang), jnp.sin(ang)
    reps = LANES // BC_DK
    return (jnp.tile(jnp.concatenate([cos, cos], axis=1), (1, reps)),
            jnp.tile(jnp.concatenate([-sin, sin], axis=1), (1, reps)))


def _retention_log_decays():
    h = jnp.arange(N_HEADS, dtype=F32)
    fwd = jnp.log1p(-jnp.exp2(-5.0 - h))
    bwd = jnp.log1p(-jnp.exp2(-5.0 - h[::-1]))
    spread = lambda v: jnp.repeat(v, BC_DK)[None, :]
    return spread(fwd), spread(bwd)


def kernel(x, c, norm1_g, w_ada, b_ada, w_in, lb_logits, norm_a_g, norm_b_g, norm_c_g, w_alpha, b_alpha,
           w_pa, w_pb, w_pc, w_out, norm2_g, w_ffn_in, w_ffn_out, norm_f_g):
    bsz, seqlen, d = x.shape
    depth = w_in.shape[0]
    assert d == D_MODEL and seqlen % (2 * CHUNK) == 0
    tm = min(TM_DENSE, seqlen)
    tm_in = min(TM_INPROJ, seqlen)
    assert seqlen % tm == 0 and seqlen % tm_in == 0

    mod = _modulation(c, w_ada, b_ada)
    lbs = _lower_bounds(lb_logits)
    cos, sin = _rotary_tables(seqlen)
    lgf, lgb = _retention_log_decays()

    x2 = x.reshape(bsz * seqlen, d)
    for l in range(depth):
        sh1, sc1, g1, sh2, sc2, g2 = [mod[l, :, i * d:(i + 1) * d].reshape(bsz, 1, d) for i in range(6)]
        wb, wf = _split_w_in(w_in[l])
        gain1 = norm1_g[l].reshape(1, d)
        zb, zf = _inproj(x2, sh1, sc1, gain1, wb, wf, seqlen, tm_in)

        wal = jnp.zeros((LANES, 2 * N_HEADS * BC_DK), F32)
        wal = wal.at[0:GLA_RANK, 0:N_HEADS * BC_DK].set(w_alpha[l, 0])
        wal = wal.at[GLA_RANK:2 * GLA_RANK, N_HEADS * BC_DK:].set(w_alpha[l, 1])
        bal = b_alpha[l].reshape(1, 2 * N_HEADS * BC_DK)

        ya, yb, yc = _mixers(zb, zf, lbs[l].reshape(1, -1), norm_a_g[l].reshape(1, -1),
                             cos, sin, lgf, lgb, norm_b_g[l].reshape(1, -1),
                             wal.astype(BF16), bal, norm_c_g[l].reshape(1, -1), bsz, seqlen)

        x2 = _merge_ffn(x2, ya, yb, yc, zb, w_pa[l].astype(BF16), w_pb[l].astype(BF16), w_pc[l].astype(BF16),
                        w_out[l].astype(BF16), g1, sh2, sc2, g2, norm2_g[l].reshape(1, d),
                        w_ffn_in[l].astype(BF16), w_ffn_out[l].astype(BF16), norm_f_g.reshape(1, d),
                        seqlen, tm, final_norm=(l == depth - 1))
    return x2.reshape(bsz, seqlen, d)
```

```python
import functools
from typing import Any, Callable, NamedTuple

import jax
import jax.numpy as jnp
from jax import lax
from jax.experimental import pallas as pl
from jax.experimental.pallas import tpu as pltpu

F32 = jnp.float32
BF16 = jnp.bfloat16

D_MODEL = 1024
N_HEADS = 4
BC_DK = 64
DV = 128
GLA_RANK = 16
GLA_TAU = 16.0
D_FF = 2816
EPS = 1e-6
TINY = 1e-30
LOG2_E = 1.4426950408889634

LANES = 128
CHUNK = 128
assert CHUNK <= LANES
VMEM_LIMIT = 48 * 1024 * 1024
VMEM_LIMIT_MIXERS = 58 * 1024 * 1024

TM_INPROJ, TN_INPROJ = 512, 1280
TM_DENSE = 512
TF_FFN = D_FF // 2

ZB_PAIR_COLS = 2304
PAIR_AQ, PAIR_AI, PAIR_AG = 0, 256, 512
PAIR_CQ, PAIR_CK, PAIR_CV, PAIR_CG = 768, 896, 1024, 1280
PAIR_BQ, PAIR_BK, PAIR_BV, PAIR_BG = 1536, 1664, 1792, 2048
ZB_GATE = 4608
ZB_GATE_BLOCK = 1536
ZF_PAIR_COLS = 512
PAIR_FF, PAIR_FB = 0, 256
ZF_LR = 1024
ZF_COLS = 1152


def _dot(a, b):
    return jnp.dot(a, b, preferred_element_type=F32)


def _dot_nt(a, b):
    return lax.dot_general(a, b, (((1,), (1,)), ((), ())), preferred_element_type=F32)


def _dot_tn(a, b):
    return lax.dot_general(a, b, (((0,), (0,)), ((), ())), preferred_element_type=F32)


def _sigmoid(x):
    return jax.nn.sigmoid(x)


def _mod_kernel(c_ref, w_ref, b_ref, o_ref):
    c = c_ref[...]
    a = c * _sigmoid(c)
    w = w_ref[...]
    a_hi = a.astype(BF16)
    a_lo = (a - a_hi.astype(F32)).astype(BF16)
    w_hi = w.astype(BF16)
    w_lo = (w - w_hi.astype(F32)).astype(BF16)
    o_ref[...] = _dot(a_hi, w_hi) + _dot(a_hi, w_lo) + _dot(a_lo, w_hi) + b_ref[...]


def _modulation(c, w_ada, b_ada):
    depth, d, n6 = w_ada.shape
    bsz = c.shape[0]
    return pl.pallas_call(
        _mod_kernel,
        grid=(depth, n6 // d),
        in_specs=[
            pl.BlockSpec((bsz, d), lambda l, j: (0, 0)),
            pl.BlockSpec((None, d, d), lambda l, j: (l, 0, j)),
            pl.BlockSpec((None, 1, d), lambda l, j: (l, 0, j)),
        ],
        out_specs=pl.BlockSpec((None, bsz, d), lambda l, j: (l, 0, j)),
        out_shape=jax.ShapeDtypeStruct((depth, bsz, n6), F32),
        name="adaln_modulation",
    )(c, w_ada, b_ada.reshape(depth, 1, n6))


def _lb_kernel(x_ref, o_ref):
    depth = x_ref.shape[0]
    rows = [x_ref[i:i + 1, :] for i in range(depth)]
    m = rows[0]
    for r in rows[1:]:
        m = jnp.maximum(m, r)
    e = [jnp.exp(r - m) for r in rows]
    s = e[0]
    for t in e[1:]:
        s = s + t
    p = [t / s for t in e]
    acc = p[0]
    o_ref[0:1, :] = jnp.maximum(acc - p[0], 0.0)
    for i in range(1, depth):
        acc = acc + p[i]
        o_ref[i:i + 1, :] = jnp.maximum(acc - p[0], 0.0)


def _lower_bounds(lb_logits):
    return pl.pallas_call(
        _lb_kernel,
        out_shape=jax.ShapeDtypeStruct(lb_logits.shape, F32),
        name="hgrn2_lower_bounds",
    )(lb_logits)


def _norm_mod(x, gain, sc, sh):
    y = x * lax.rsqrt(jnp.mean(x * x, axis=-1, keepdims=True) + EPS)
    return (y * gain) * (1.0 + sc) + sh


def _inproj_kernel(x_ref, sh_ref, sc_ref, g_ref, wb_ref, wf_ref, zb_ref, zf_ref, h_ref):
    h_ref[...] = _norm_mod(x_ref[...], g_ref[...], sc_ref[0], sh_ref[0]).astype(BF16)
    for j in range(zb_ref.shape[1] // TN_INPROJ):
        cols = slice(j * TN_INPROJ, (j + 1) * TN_INPROJ)
        zb_ref[:, cols] = _dot(h_ref[...], wb_ref[:, cols]).astype(zb_ref.dtype)
    zf_ref[...] = _dot(h_ref[...], wf_ref[...])


def _inproj(x2, sh, sc, gain, wb, wf, seqlen, tm):
    m, d = x2.shape
    per = seqlen // tm
    bvec = lambda i: (i // per, 0, 0)
    resident = lambda w: pl.BlockSpec(w.shape, lambda i: (0, 0), pipeline_mode=pl.Buffered(1))
    return pl.pallas_call(
        _inproj_kernel,
        grid=(m // tm,),
        in_specs=[
            pl.BlockSpec((tm, d), lambda i: (i, 0)),
            pl.BlockSpec((1, 1, d), bvec), pl.BlockSpec((1, 1, d), bvec),
            pl.BlockSpec((1, d), lambda i: (0, 0)),
            resident(wb), resident(wf),
        ],
        out_specs=[pl.BlockSpec((tm, wb.shape[1]), lambda i: (i, 0)),
                   pl.BlockSpec((tm, wf.shape[1]), lambda i: (i, 0))],
        out_shape=[jax.ShapeDtypeStruct((m, wb.shape[1]), BF16), jax.ShapeDtypeStruct((m, wf.shape[1]), F32)],
        scratch_shapes=[pltpu.VMEM((tm, d), BF16)],
        compiler_params=pltpu.CompilerParams(dimension_semantics=("parallel",), vmem_limit_bytes=VMEM_LIMIT),
        name="inproj",
    )(x2, sh, sc, gain, wb, wf)


def _merge_ffn_kernel(x_ref, ya_ref, yb_ref, yc_ref, gates_lo_ref, gates_hi_ref,
                      wpa_ref, wpb_ref, wpc_ref, wout_ref, g1_ref,
                      sh_ref, sc_ref, g2_ref, ng_ref, wi_ref, wo_ref, nf_ref, o_ref, h_ref, *, final_norm):
    d = x_ref.shape[1]
    dff = wo_ref.shape[0]
    gates = jnp.concatenate([gates_lo_ref[...], gates_hi_ref[...]], axis=1).astype(F32)
    merged = (_sigmoid(gates[:, 0:d]) * _dot(ya_ref[...], wpa_ref[...])
              + _sigmoid(gates[:, d:2 * d]) * _dot(yb_ref[...], wpb_ref[...])
              + _sigmoid(gates[:, 2 * d:3 * d]) * _dot(yc_ref[...], wpc_ref[...]))
    x = x_ref[...] + g1_ref[0] * _dot(merged.astype(BF16), wout_ref[...])
    h_ref[...] = _norm_mod(x, ng_ref[...], sc_ref[0], sh_ref[0]).astype(BF16)
    acc = None
    for k in range(dff // TF_FFN):
        gate = _dot(h_ref[...], wi_ref[:, k * TF_FFN:(k + 1) * TF_FFN])
        up = _dot(h_ref[...], wi_ref[:, dff + k * TF_FFN:dff + (k + 1) * TF_FFN])
        act = (gate * _sigmoid(gate) * up).astype(BF16)
        part = _dot(act, wo_ref[k * TF_FFN:(k + 1) * TF_FFN, :])
        acc = part if acc is None else acc + part
    xn = x + g2_ref[0] * acc
    if final_norm:
        xn = xn * lax.rsqrt(jnp.mean(xn * xn, axis=-1, keepdims=True) + EPS) * nf_ref[...]
    o_ref[...] = xn


def _merge_ffn(x2, ya, yb, yc, zb, wpa, wpb, wpc, wout, g1, sh, sc, g2, ng, w_in, w_out, nf, seqlen, tm,
               final_norm):
    m, d = x2.shape
    w = ya.shape[1]
    per = seqlen // tm
    row = lambda i: (i, 0)
    bvec = pl.BlockSpec((1, 1, d), lambda i: (i // per, 0, 0))
    vec = pl.BlockSpec((1, d), lambda i: (0, 0))
    resident = lambda a: pl.BlockSpec(a.shape, lambda i: (0, 0), pipeline_mode=pl.Buffered(1))
    return pl.pallas_call(
        functools.partial(_merge_ffn_kernel, final_norm=final_norm),
        grid=(m // tm,),
        in_specs=[
            pl.BlockSpec((tm, d), row),
            pl.BlockSpec((tm, w), row), pl.BlockSpec((tm, w), row), pl.BlockSpec((tm, w), row),
            pl.BlockSpec((tm, ZB_GATE_BLOCK), lambda i: (i, ZB_GATE // ZB_GATE_BLOCK)),
            pl.BlockSpec((tm, ZB_GATE_BLOCK), lambda i: (i, ZB_GATE // ZB_GATE_BLOCK + 1)),
            resident(wpa), resident(wpb), resident(wpc), resident(wout), bvec,
            bvec, bvec, bvec, vec, resident(w_in), resident(w_out), vec,
        ],
        out_specs=pl.BlockSpec((tm, d), row),
        out_shape=jax.ShapeDtypeStruct((m, d), F32),
        scratch_shapes=[pltpu.VMEM((tm, d), BF16)],
        compiler_params=pltpu.CompilerParams(dimension_semantics=("parallel",),
                                             vmem_limit_bytes=VMEM_LIMIT_MIXERS),
        name="merge_ffn",
    )(x2, ya, yb, yc, zb, zb, wpa, wpb, wpc, wout, g1, sh, sc, g2, ng, w_in, w_out, nf)


N_LEVELS = CHUNK.bit_length() - 1


def _gated_factors(g2, forward):
    row = lax.broadcasted_iota(jnp.int32, g2.shape, 0)
    yield jnp.exp2(g2), None, None
    lb, tb = g2, g2
    half = CHUNK // 2
    b = 1
    while b < half:
        right = (row & b) != 0
        from_left = pltpu.roll(tb, b, axis=0)
        from_right = pltpu.roll(tb, CHUNK - b, axis=0)
        lb = lb + jnp.where(right, from_left, 0.0)
        tb = tb + jnp.where(right, from_left, from_right)
        b *= 2
        rest = tb - lb
        fq, fk = (jnp.exp2(lb), jnp.exp2(rest)) if forward else (jnp.exp2(rest + g2), jnp.exp2(lb - g2))
        yield fq, fk, None
    decay_lo, decay_hi = jnp.exp2(tb[0:1, :]), jnp.exp2(tb[half:half + 1, :])
    after_lo = jnp.where(row >= half, decay_lo, 1.0)
    before_hi = jnp.where(row < half, decay_hi, 1.0)
    if forward:
        yield fq * after_lo, fk * before_hi, decay_lo * decay_hi
    else:
        yield fq * before_hi, fk * after_lo, decay_lo * decay_hi


def _fixed_factors(log_decay, forward):
    pos = lax.broadcasted_iota(jnp.int32, (CHUNK, LANES), 0).astype(F32)
    mid = CHUNK // 2
    chunk_decay = jnp.exp(float(CHUNK) * log_decay)
    if forward:
        return [(jnp.exp((pos - (mid - 1.0)) * log_decay), jnp.exp(((mid - 1.0) - pos) * log_decay), None),
                (jnp.exp((pos + 1.0) * log_decay), jnp.exp((CHUNK - 1.0 - pos) * log_decay), chunk_decay)]
    return [(jnp.exp((mid - pos) * log_decay), jnp.exp((pos - mid) * log_decay), None),
            (jnp.exp((CHUNK - pos) * log_decay), jnp.exp(pos * log_decay), chunk_decay)]


def _store_masks(mask_ref, hierarchical):
    row = lax.broadcasted_iota(jnp.int32, (CHUNK, CHUNK), 0)
    col = lax.broadcasted_iota(jnp.int32, (CHUNK, CHUNK), 1)
    one = lambda m: jnp.where(m, 1.0, 0.0).astype(BF16)
    mask_ref[0] = one(row == col)
    n = N_LEVELS if hierarchical else 1
    for level in range(n):
        same_level = ((row ^ col) >> level) == 1 if hierarchical else (row != col)
        mask_ref[1 + level] = one(same_level & (row > col))
        mask_ref[1 + n + level] = one(same_level & (col > row))
    return n


def _lane_masks(n_sub):
    if n_sub == 1:
        return [None]
    lane = lax.broadcasted_iota(jnp.int32, (1, LANES), 1)
    width = LANES // n_sub
    return [jnp.where((lane >= i * width) & (lane < (i + 1) * width), 1.0, 0.0) for i in range(n_sub)]


def _chunk_rows(n):
    if isinstance(n, int):
        return pl.ds(n * CHUNK, CHUNK)
    return pl.ds(pl.multiple_of(n * CHUNK, CHUNK), CHUNK)


class _Mixer(NamedTuple):
    n_groups: int
    n_sub: int
    hierarchical: bool
    prep: Callable
    load_v: Callable
    of_ref: Any
    ob_ref: Any
    st_ref: Any
    mask_ref: Any
    slots: tuple


def _scan_mixers(n_chunks, mixers):
    levels = []
    for m in mixers:
        m.st_ref[...] = jnp.zeros_like(m.st_ref)
        levels.append(_store_masks(m.mask_ref, m.hierarchical))
        if m.hierarchical and m.n_sub > 1:
            for _, kt_ref, _, _ in m.slots:
                kt_ref[...] = jnp.zeros_like(kt_ref)

    def rows_of(n, forward):
        return _chunk_rows(n if forward else n_chunks - 1 - n)

    def chains(m):
        return enumerate((gi, forward) for gi in range(m.n_groups) for forward in (True, False))

    def stage(n, slot):
        for m in mixers:
            qt_ref, kt_ref, p0_ref, dec_ref = m.slots[slot]
            lane_masks = _lane_masks(m.n_sub)
            for c, (gi, forward) in chains(m):
                q, k, factors = m.prep(gi, forward, rows_of(n, forward))
                qb, kb = q.astype(BF16), k.astype(BF16)
                if forward:
                    for i in range(m.n_sub):
                        qm = q if lane_masks[i] is None else q * lane_masks[i]
                        diag = jnp.sum(qm * k, axis=-1, keepdims=True).astype(BF16)
                        p0_ref[gi * m.n_sub + i] = m.mask_ref[0] * diag
                for level, (fq, fk, chunk_decay) in enumerate(factors):
                    qt_ref[c, level] = qb * fq.astype(BF16)
                    if chunk_decay is not None:
                        dec_ref[c] = chunk_decay
                    if m.hierarchical and chunk_decay is None:
                        kt = (kb if fk is None else kb * fk.astype(BF16)).T
                        width = LANES // m.n_sub
                        for i in range(m.n_sub):
                            kt_ref[c, level, i, i * width:(i + 1) * width, 0:CHUNK] = kt[i * width:(i + 1) * width, :]
                    else:
                        kt_ref[c, level, 0, 0:CHUNK, :] = kb * fk.astype(BF16)

    def consume(n, slot):
        for m, n_levels in zip(mixers, levels):
            qt_ref, kt_ref, p0_ref, dec_ref = m.slots[slot]
            lane_masks = _lane_masks(m.n_sub)
            for c, (gi, forward) in chains(m):
                rows = rows_of(n, forward)
                o_ref = m.of_ref if forward else m.ob_ref
                mask0 = 1 if forward else 1 + n_levels
                for i, v in enumerate(m.load_v(gi, rows)):
                    head = gi * m.n_sub + i
                    sub = (lambda t: t) if lane_masks[i] is None else (lambda t, lm=lane_masks[i].astype(BF16): t * lm)
                    p = p0_ref[head] if forward else None
                    for level in range(n_levels):
                        if m.hierarchical:
                            s = _dot(qt_ref[c, level], kt_ref[c, level, i, :, 0:CHUNK])
                        else:
                            s = _dot_nt(sub(qt_ref[c, level]), kt_ref[c, level, 0, 0:CHUNK, :])
                        s = s.astype(BF16) * m.mask_ref[mask0 + level]
                        p = s if p is None else p + s
                    st = m.st_ref[c * m.n_sub + i]
                    inter = _dot_nt(sub(qt_ref[c, n_levels]), st.astype(BF16))
                    o_ref[rows, head * DV:(head + 1) * DV] = (_dot(p, v) + inter).astype(o_ref.dtype)
                    m.st_ref[c * m.n_sub + i] = st * dec_ref[c] + _dot_tn(v, kt_ref[c, n_levels, 0, 0:CHUNK, :])

    def body(step, carry):
        n = 2 * step
        consume(n, 0)
        stage(n + 1, 1)
        consume(n + 1, 1)
        stage(jnp.minimum(n + 2, n_chunks - 1), 0)
        return carry

    stage(0, 0)
    lax.fori_loop(0, n_chunks // 2, body, 0)


def _finish(mixer, gain_ref, gate_ref, y_ref, center, swish):
    rows_per = 256
    n_blocks = mixer.of_ref.shape[0] // rows_per

    def body(b, carry):
        rows = pl.ds(pl.multiple_of(b * rows_per, rows_per), rows_per)
        for i in range(mixer.n_groups * mixer.n_sub):
            cols = slice(i * DV, (i + 1) * DV)
            o = mixer.of_ref[rows, cols].astype(F32) + mixer.ob_ref[rows, cols].astype(F32)
            if center:
                o = o - jnp.mean(o, axis=-1, keepdims=True)
            o = o * lax.rsqrt(jnp.mean(o * o, axis=-1, keepdims=True) + EPS)
            gz = gate_ref[rows, cols].astype(F32)
            act = gz * _sigmoid(gz) if swish else _sigmoid(gz)
            y_ref[rows, cols] = ((o * gain_ref[:, cols]) * act).astype(y_ref.dtype)
        return carry

    lax.fori_loop(0, n_blocks, body, 0)


def _group_cols(gi):
    return slice(gi * LANES, (gi + 1) * LANES)


N_SLOTS = 2
N_MIXER_SCRATCH = 4 + 4 * N_SLOTS


def _mixer(n_groups, n_sub, hierarchical, prep, load_v, scratch):
    of_ref, ob_ref, st_ref, mask_ref = scratch[:4]
    return _Mixer(n_groups, n_sub, hierarchical, prep, load_v, of_ref, ob_ref, st_ref, mask_ref,
                  tuple(scratch[4 + 4 * s:8 + 4 * s] for s in range(N_SLOTS)))


def _mixer_scratch(seqlen, n_groups, n_sub, hierarchical):
    n_heads, n_chains = n_groups * n_sub, 2 * n_groups
    n_levels = N_LEVELS if hierarchical else 1
    key_copies = n_sub if hierarchical else 1
    slot = [pltpu.VMEM((n_chains, n_levels + 1, CHUNK, LANES), BF16),
            pltpu.VMEM((n_chains, n_levels + 1, key_copies, LANES, LANES), BF16),
            pltpu.VMEM((n_heads, CHUNK, CHUNK), BF16),
            pltpu.VMEM((n_chains, 1, LANES), F32)]
    return [pltpu.VMEM((seqlen, n_heads * DV), BF16),
            pltpu.VMEM((seqlen, n_heads * DV), BF16),
            pltpu.VMEM((2 * n_heads, DV, LANES), F32),
            pltpu.VMEM((1 + 2 * n_levels, CHUNK, CHUNK), BF16)] + N_SLOTS * slot


def _hgrn2_mixer(q_ref, v_ref, ff_ref, fb_ref, lb_ref, scratch):
    def prep(gi, forward, rows):
        cols = _group_cols(gi)
        lb = lb_ref[:, cols]
        z = (ff_ref if forward else fb_ref)[rows, cols]
        f = lb + (1.0 - lb) * _sigmoid(z)
        g2 = jnp.log2(jnp.maximum(f, TINY))
        key = 1.0 - f
        aq = q_ref[rows, cols].astype(F32)
        return aq * _sigmoid(aq), key, _gated_factors(g2, forward)

    return _mixer(q_ref.shape[1] // LANES, 1, True, prep,
                  lambda gi, rows: [v_ref[rows, _group_cols(gi)]], scratch)


def _rotate_half(x):
    lane = lax.broadcasted_iota(jnp.int32, x.shape, 1)
    half = BC_DK // 2
    first = (lane % BC_DK) < half
    return jnp.where(first, pltpu.roll(x, LANES - half, axis=1), pltpu.roll(x, half, axis=1))


def _sub_values(v_ref, gi, rows):
    return [v_ref[rows, (2 * gi + i) * DV:(2 * gi + i + 1) * DV] for i in range(2)]


def _retention_mixer(q_ref, k_ref, v_ref, cos_ref, sin_ref, lgf_ref, lgb_ref, scratch):
    n_groups = q_ref.shape[1] // LANES
    packs = [(_fixed_factors(lgb_ref[:, _group_cols(gi)], False), _fixed_factors(lgf_ref[:, _group_cols(gi)], True))
             for gi in range(n_groups)]

    def prep(gi, forward, rows):
        cols = _group_cols(gi)
        cos = cos_ref[rows, :]
        sin = sin_ref[rows, :]
        q = q_ref[rows, cols].astype(F32)
        k = k_ref[rows, cols].astype(F32)
        q = q * cos + _rotate_half(q) * sin
        k = (k * cos + _rotate_half(k) * sin) * (BC_DK ** -0.5)
        return q, k, packs[gi][1 if forward else 0]

    return _mixer(n_groups, 2, False, prep, functools.partial(_sub_values, v_ref), scratch)


def _gla_mixer(q_ref, k_ref, v_ref, lr_ref, waf_ref, wab_ref, baf_ref, bab_ref, scratch):
    def prep(gi, forward, rows):
        cols = _group_cols(gi)
        w_ref, b_ref = (waf_ref, baf_ref) if forward else (wab_ref, bab_ref)
        x = _dot(lr_ref[rows, :].astype(BF16), w_ref[:, cols]) + b_ref[:, cols]
        log_sig = jnp.minimum(x, 0.0) - jnp.log(1.0 + jnp.exp(-jnp.abs(x)))
        g2 = log_sig * (LOG2_E / GLA_TAU)
        q = q_ref[rows, cols].astype(F32) * (BC_DK ** -0.5)
        return q, k_ref[rows, cols].astype(F32), _gated_factors(g2, forward)

    return _mixer(q_ref.shape[1] // LANES, 2, True, prep, functools.partial(_sub_values, v_ref), scratch)


def _cols(ref, start, width):
    return ref.at[:, start:start + width]


def _mixers_kernel(zb_ref, zf_ref, lr_ref, lb_ref, again_ref,
                   cos_ref, sin_ref, lgf_ref, lgb_ref, bgain_ref,
                   waf_ref, wab_ref, baf_ref, bab_ref, cgain_ref,
                   ya_ref, yb_ref, yc_ref, *scratch):
    pair, qk = 2 * DV, 2 * BC_DK
    s1, s2 = N_MIXER_SCRATCH, 2 * N_MIXER_SCRATCH
    hgrn2 = _hgrn2_mixer(_cols(zb_ref, PAIR_AQ, pair), _cols(zb_ref, PAIR_AI, pair),
                         _cols(zf_ref, PAIR_FF, pair), _cols(zf_ref, PAIR_FB, pair), lb_ref, scratch[:s1])
    retention = _retention_mixer(_cols(zb_ref, PAIR_BQ, qk), _cols(zb_ref, PAIR_BK, qk),
                                 _cols(zb_ref, PAIR_BV, pair), cos_ref, sin_ref, lgf_ref, lgb_ref, scratch[s1:s2])
    gla = _gla_mixer(_cols(zb_ref, PAIR_CQ, qk), _cols(zb_ref, PAIR_CK, qk), _cols(zb_ref, PAIR_CV, pair),
                     lr_ref, waf_ref, wab_ref, baf_ref, bab_ref, scratch[s2:])
    _scan_mixers(zb_ref.shape[0] // CHUNK, [hgrn2, retention, gla])
    _finish(hgrn2, again_ref, _cols(zb_ref, PAIR_AG, pair), ya_ref, center=False, swish=False)
    _finish(retention, bgain_ref, _cols(zb_ref, PAIR_BG, pair), yb_ref, center=True, swish=True)
    _finish(gla, cgain_ref, _cols(zb_ref, PAIR_CG, pair), yc_ref, center=False, swish=True)


def _zspec(seqlen, width, col0):
    blk = col0 // width
    return pl.BlockSpec((seqlen, width), lambda b, g: (b, blk + g))


def _vspec(width, blk0=0):
    return pl.BlockSpec((1, width), lambda b, g: (0, blk0 + g))


def _mixers(zb, zf, lb, gain_a, cos, sin, lgf, lgb, gain_b, wal, bal, gain_c, bsz, seqlen):
    t, w = seqlen, 2 * DV
    n_steps = N_HEADS * DV // w
    half = N_HEADS * BC_DK // LANES
    out = pl.BlockSpec((t, w), lambda b, g: (b, g))
    table = pl.BlockSpec((t, LANES), lambda b, g: (0, 0), pipeline_mode=pl.Buffered(1))
    return pl.pallas_call(
        _mixers_kernel,
        grid=(bsz, n_steps),
        in_specs=[_zspec(t, ZB_PAIR_COLS, 0), _zspec(t, ZF_PAIR_COLS, 0),
                  pl.BlockSpec((t, LANES), lambda b, g: (b, ZF_LR // LANES)),
                  _vspec(w), _vspec(w),
                  table, table, _vspec(LANES), _vspec(LANES), _vspec(w),
                  pl.BlockSpec((LANES, LANES), lambda b, g: (0, g)),
                  pl.BlockSpec((LANES, LANES), lambda b, g: (0, half + g)),
                  _vspec(LANES), _vspec(LANES, half), _vspec(w)],
        out_specs=[out, out, out],
        out_shape=[jax.ShapeDtypeStruct((bsz * t, n_steps * w), BF16)] * 3,
        scratch_shapes=(_mixer_scratch(t, 2, 1, True) + _mixer_scratch(t, 1, 2, False)
                        + _mixer_scratch(t, 1, 2, True)),
        compiler_params=pltpu.CompilerParams(dimension_semantics=("parallel", "parallel"),
                                             vmem_limit_bytes=VMEM_LIMIT_MIXERS),
        name="mixers",
    )(zb, zf, zf, lb, gain_a, cos, sin, lgf, lgb, gain_b, wal, wal, bal, bal, gain_c)


def _split_w_in(w):
    o = {}
    off = 0
    for name, size in (("a_q", 512), ("a_ff", 512), ("a_fb", 512), ("a_i", 512), ("a_g", 512),
                       ("b_q", 256), ("b_k", 256), ("b_v", 512), ("b_g", 512),
                       ("c_q", 256), ("c_k", 256), ("c_v", 512), ("c_g", 512), ("c_lr", 32),
                       ("gate_a", 1024), ("gate_b", 1024), ("gate_c", 1024)):
        o[name] = w[..., off:off + size]
        off += size
    n_pairs = N_HEADS // 2
    part = lambda name, g: o[name][..., g * (o[name].shape[-1] // n_pairs):(g + 1) * (o[name].shape[-1] // n_pairs)]
    pairs = [part(n, g) for g in range(n_pairs)
             for n in ("a_q", "a_i", "a_g", "c_q", "c_k", "c_v", "c_g", "b_q", "b_k", "b_v", "b_g")]
    wb = jnp.concatenate(pairs + [o["gate_a"], o["gate_b"], o["gate_c"]], axis=-1)
    pad = jnp.zeros(w.shape[:-1] + (ZF_COLS - ZF_LR - 2 * GLA_RANK,), w.dtype)
    gates = [part(n, g) for g in range(n_pairs) for n in ("a_ff", "a_fb")]
    wf = jnp.concatenate(gates + [o["c_lr"], pad], axis=-1)
    return wb.astype(BF16), wf.astype(BF16)


def _rotary_tables(seqlen):
    pos = jnp.arange(seqlen, dtype=F32)
    inv_freq = 10000.0 ** (-jnp.arange(0, BC_DK, 2, dtype=F32) / BC_DK)
    ang = pos[:, None] * inv_freq[None, :]
    cos, sin = jnp.cos(ang), jnp.sin(ang)
    reps = LANES // BC_DK
    return (jnp.tile(jnp.concatenate([cos, cos], axis=1), (1, reps)),
            jnp.tile(jnp.concatenate([-sin, sin], axis=1), (1, reps)))


def _retention_log_decays():
    h = jnp.arange(N_HEADS, dtype=F32)
    fwd = jnp.log1p(-jnp.exp2(-5.0 - h))
    bwd = jnp.log1p(-jnp.exp2(-5.0 - h[::-1]))
    spread = lambda v: jnp.repeat(v, BC_DK)[None, :]
    return spread(fwd), spread(bwd)


def kernel(x, c, norm1_g, w_ada, b_ada, w_in, lb_logits, norm_a_g, norm_b_g, norm_c_g, w_alpha, b_alpha,
           w_pa, w_pb, w_pc, w_out, norm2_g, w_ffn_in, w_ffn_out, norm_f_g):
    bsz, seqlen, d = x.shape
    depth = w_in.shape[0]
    assert d == D_MODEL and seqlen % (2 * CHUNK) == 0
    tm = min(TM_DENSE, seqlen)
    tm_in = min(TM_INPROJ, seqlen)
    assert seqlen % tm == 0 and seqlen % tm_in == 0

    mod = _modulation(c, w_ada, b_ada)
    lbs = _lower_bounds(lb_logits)
    cos, sin = _rotary_tables(seqlen)
    lgf, lgb = _retention_log_decays()
    wb_all, wf_all = _split_w_in(w_in)

    x2 = x.reshape(bsz * seqlen, d)
    for l in range(depth):
        sh1, sc1, g1, sh2, sc2, g2 = [mod[l, :, i * d:(i + 1) * d].reshape(bsz, 1, d) for i in range(6)]
        gain1 = norm1_g[l].reshape(1, d)
        zb, zf = _inproj(x2, sh1, sc1, gain1, wb_all[l], wf_all[l], seqlen, tm_in)

        wal = jnp.zeros((LANES, 2 * N_HEADS * BC_DK), F32)
        wal = wal.at[0:GLA_RANK, 0:N_HEADS * BC_DK].set(w_alpha[l, 0])
        wal = wal.at[GLA_RANK:2 * GLA_RANK, N_HEADS * BC_DK:].set(w_alpha[l, 1])
        bal = b_alpha[l].reshape(1, 2 * N_HEADS * BC_DK)

        ya, yb, yc = _mixers(zb, zf, lbs[l].reshape(1, -1), norm_a_g[l].reshape(1, -1),
                             cos, sin, lgf, lgb, norm_b_g[l].reshape(1, -1),
                             wal.astype(BF16), bal, norm_c_g[l].reshape(1, -1), bsz, seqlen)

        x2 = _merge_ffn(x2, ya, yb, yc, zb, w_pa[l].astype(BF16), w_pb[l].astype(BF16), w_pc[l].astype(BF16),
                        w_out[l].astype(BF16), g1, sh2, sc2, g2, norm2_g[l].reshape(1, d),
                        w_ffn_in[l].astype(BF16), w_ffn_out[l].astype(BF16), norm_f_g.reshape(1, d),
                        seqlen, tm, final_norm=(l == depth - 1))
    return x2.reshape(bsz, seqlen, d)
```

```python
import functools
from typing import Any, Callable, NamedTuple

import jax
import jax.numpy as jnp
from jax import lax
from jax.experimental import pallas as pl
from jax.experimental.pallas import tpu as pltpu

F32 = jnp.float32
BF16 = jnp.bfloat16

D_MODEL = 1024
N_HEADS = 4
BC_DK = 64
DV = 128
GLA_RANK = 16
GLA_TAU = 16.0
D_FF = 2816
EPS = 1e-6
TINY = 1e-30
LOG2_E = 1.4426950408889634

LANES = 128
CHUNK = 128
assert CHUNK <= LANES
VMEM_LIMIT = 48 * 1024 * 1024
VMEM_LIMIT_MIXERS = 58 * 1024 * 1024

TM_INPROJ, TN_INPROJ = 512, 1280
TM_DENSE = 512
TF_FFN = D_FF // 2

ZB_PAIR_COLS = 2304
PAIR_AQ, PAIR_AI, PAIR_AG = 0, 256, 512
PAIR_CQ, PAIR_CK, PAIR_CV, PAIR_CG = 768, 896, 1024, 1280
PAIR_BQ, PAIR_BK, PAIR_BV, PAIR_BG = 1536, 1664, 1792, 2048
ZB_GATE = 4608
ZB_GATE_BLOCK = 1536
ZF_PAIR_COLS = 512
PAIR_FF, PAIR_FB = 0, 256
ZF_LR = 1024
ZF_COLS = 1152


def _dot(a, b):
    return jnp.dot(a, b, preferred_element_type=F32)


def _dot_nt(a, b):
    return lax.dot_general(a, b, (((1,), (1,)), ((), ())), preferred_element_type=F32)


def _dot_tn(a, b):
    return lax.dot_general(a, b, (((0,), (0,)), ((), ())), preferred_element_type=F32)


def _sigmoid(x):
    return jax.nn.sigmoid(x)


def _mod_kernel(c_ref, w_ref, b_ref, o_ref):
    c = c_ref[...]
    a = c * _sigmoid(c)
    w = w_ref[...]
    a_hi = a.astype(BF16)
    a_lo = (a - a_hi.astype(F32)).astype(BF16)
    w_hi = w.astype(BF16)
    w_lo = (w - w_hi.astype(F32)).astype(BF16)
    o_ref[...] = _dot(a_hi, w_hi) + _dot(a_hi, w_lo) + _dot(a_lo, w_hi) + b_ref[...]


def _modulation(c, w_ada, b_ada):
    depth, d, n6 = w_ada.shape
    bsz = c.shape[0]
    return pl.pallas_call(
        _mod_kernel,
        grid=(depth, n6 // d),
        in_specs=[
            pl.BlockSpec((bsz, d), lambda l, j: (0, 0)),
            pl.BlockSpec((None, d, d), lambda l, j: (l, 0, j)),
            pl.BlockSpec((None, 1, d), lambda l, j: (l, 0, j)),
        ],
        out_specs=pl.BlockSpec((None, bsz, d), lambda l, j: (l, 0, j)),
        out_shape=jax.ShapeDtypeStruct((depth, bsz, n6), F32),
        name="adaln_modulation",
    )(c, w_ada, b_ada.reshape(depth, 1, n6))


def _lb_kernel(x_ref, o_ref):
    depth = x_ref.shape[0]
    rows = [x_ref[i:i + 1, :] for i in range(depth)]
    m = rows[0]
    for r in rows[1:]:
        m = jnp.maximum(m, r)
    e = [jnp.exp(r - m) for r in rows]
    s = e[0]
    for t in e[1:]:
        s = s + t
    p = [t / s for t in e]
    acc = p[0]
    o_ref[0:1, :] = jnp.maximum(acc - p[0], 0.0)
    for i in range(1, depth):
        acc = acc + p[i]
        o_ref[i:i + 1, :] = jnp.maximum(acc - p[0], 0.0)


def _lower_bounds(lb_logits):
    return pl.pallas_call(
        _lb_kernel,
        out_shape=jax.ShapeDtypeStruct(lb_logits.shape, F32),
        name="hgrn2_lower_bounds",
    )(lb_logits)


def _norm_mod(x, gain, sc, sh):
    y = x * lax.rsqrt(jnp.mean(x * x, axis=-1, keepdims=True) + EPS)
    return (y * gain) * (1.0 + sc) + sh


def _inproj_kernel(x_ref, sh_ref, sc_ref, g_ref, wb_ref, wf_ref, zb_ref, zf_ref, h_ref):
    h_ref[...] = _norm_mod(x_ref[...], g_ref[...], sc_ref[0], sh_ref[0]).astype(BF16)
    for j in range(zb_ref.shape[1] // TN_INPROJ):
        cols = slice(j * TN_INPROJ, (j + 1) * TN_INPROJ)
        zb_ref[:, cols] = _dot(h_ref[...], wb_ref[:, cols]).astype(zb_ref.dtype)
    zf_ref[...] = _dot(h_ref[...], wf_ref[...])


def _resident(stacked, layer):
    return pl.BlockSpec((None,) + stacked.shape[1:], lambda i: (layer, 0, 0), pipeline_mode=pl.Buffered(1))


def _inproj(x2, sh, sc, gain, wb, wf, layer, seqlen, tm):
    m, d = x2.shape
    per = seqlen // tm
    bvec = lambda i: (i // per, 0, 0)
    resident = lambda w: _resident(w, layer)
    return pl.pallas_call(
        _inproj_kernel,
        grid=(m // tm,),
        in_specs=[
            pl.BlockSpec((tm, d), lambda i: (i, 0)),
            pl.BlockSpec((1, 1, d), bvec), pl.BlockSpec((1, 1, d), bvec),
            pl.BlockSpec((1, d), lambda i: (0, 0)),
            resident(wb), resident(wf),
        ],
        out_specs=[pl.BlockSpec((tm, wb.shape[2]), lambda i: (i, 0)),
                   pl.BlockSpec((tm, wf.shape[2]), lambda i: (i, 0))],
        out_shape=[jax.ShapeDtypeStruct((m, wb.shape[2]), BF16), jax.ShapeDtypeStruct((m, wf.shape[2]), F32)],
        scratch_shapes=[pltpu.VMEM((tm, d), BF16)],
        compiler_params=pltpu.CompilerParams(dimension_semantics=("parallel",), vmem_limit_bytes=VMEM_LIMIT),
        name="inproj",
    )(x2, sh, sc, gain, wb, wf)


def _merge_ffn_kernel(x_ref, ya_ref, yb_ref, yc_ref, gates_lo_ref, gates_hi_ref,
                      wpa_ref, wpb_ref, wpc_ref, wout_ref, g1_ref,
                      sh_ref, sc_ref, g2_ref, ng_ref, wi_ref, wo_ref, nf_ref, o_ref, h_ref, *, final_norm):
    d = x_ref.shape[1]
    dff = wo_ref.shape[0]
    gates = jnp.concatenate([gates_lo_ref[...], gates_hi_ref[...]], axis=1).astype(F32)
    merged = (_sigmoid(gates[:, 0:d]) * _dot(ya_ref[...], wpa_ref[...])
              + _sigmoid(gates[:, d:2 * d]) * _dot(yb_ref[...], wpb_ref[...])
              + _sigmoid(gates[:, 2 * d:3 * d]) * _dot(yc_ref[...], wpc_ref[...]))
    x = x_ref[...] + g1_ref[0] * _dot(merged.astype(BF16), wout_ref[...])
    h_ref[...] = _norm_mod(x, ng_ref[...], sc_ref[0], sh_ref[0]).astype(BF16)
    acc = None
    for k in range(dff // TF_FFN):
        gate = _dot(h_ref[...], wi_ref[:, k * TF_FFN:(k + 1) * TF_FFN])
        up = _dot(h_ref[...], wi_ref[:, dff + k * TF_FFN:dff + (k + 1) * TF_FFN])
        act = (gate * _sigmoid(gate) * up).astype(BF16)
        part = _dot(act, wo_ref[k * TF_FFN:(k + 1) * TF_FFN, :])
        acc = part if acc is None else acc + part
    xn = x + g2_ref[0] * acc
    if final_norm:
        xn = xn * lax.rsqrt(jnp.mean(xn * xn, axis=-1, keepdims=True) + EPS) * nf_ref[...]
    o_ref[...] = xn


def _merge_ffn(x2, ya, yb, yc, zb, wpa, wpb, wpc, wout, g1, sh, sc, g2, ng, w_in, w_out, nf, layer, seqlen, tm,
               final_norm):
    m, d = x2.shape
    w = ya.shape[1]
    per = seqlen // tm
    row = lambda i: (i, 0)
    bvec = pl.BlockSpec((1, 1, d), lambda i: (i // per, 0, 0))
    vec = pl.BlockSpec((1, d), lambda i: (0, 0))
    resident = lambda a: _resident(a, layer)
    return pl.pallas_call(
        functools.partial(_merge_ffn_kernel, final_norm=final_norm),
        grid=(m // tm,),
        in_specs=[
            pl.BlockSpec((tm, d), row),
            pl.BlockSpec((tm, w), row), pl.BlockSpec((tm, w), row), pl.BlockSpec((tm, w), row),
            pl.BlockSpec((tm, ZB_GATE_BLOCK), lambda i: (i, ZB_GATE // ZB_GATE_BLOCK)),
            pl.BlockSpec((tm, ZB_GATE_BLOCK), lambda i: (i, ZB_GATE // ZB_GATE_BLOCK + 1)),
            resident(wpa), resident(wpb), resident(wpc), resident(wout), bvec,
            bvec, bvec, bvec, vec, resident(w_in), resident(w_out), vec,
        ],
        out_specs=pl.BlockSpec((tm, d), row),
        out_shape=jax.ShapeDtypeStruct((m, d), F32),
        scratch_shapes=[pltpu.VMEM((tm, d), BF16)],
        compiler_params=pltpu.CompilerParams(dimension_semantics=("parallel",),
                                             vmem_limit_bytes=VMEM_LIMIT_MIXERS),
        name="merge_ffn",
    )(x2, ya, yb, yc, zb, zb, wpa, wpb, wpc, wout, g1, sh, sc, g2, ng, w_in, w_out, nf)


N_LEVELS = CHUNK.bit_length() - 1


def _gated_factors(g2, forward):
    row = lax.broadcasted_iota(jnp.int32, g2.shape, 0)
    yield jnp.exp2(g2), None, None
    lb, tb = g2, g2
    half = CHUNK // 2
    b = 1
    while b < half:
        right = (row & b) != 0
        from_left = pltpu.roll(tb, b, axis=0)
        from_right = pltpu.roll(tb, CHUNK - b, axis=0)
        lb = lb + jnp.where(right, from_left, 0.0)
        tb = tb + jnp.where(right, from_left, from_right)
        b *= 2
        rest = tb - lb
        fq, fk = (jnp.exp2(lb), jnp.exp2(rest)) if forward else (jnp.exp2(rest + g2), jnp.exp2(lb - g2))
        yield fq, fk, None
    decay_lo, decay_hi = jnp.exp2(tb[0:1, :]), jnp.exp2(tb[half:half + 1, :])
    after_lo = jnp.where(row >= half, decay_lo, 1.0)
    before_hi = jnp.where(row < half, decay_hi, 1.0)
    if forward:
        yield fq * after_lo, fk * before_hi, decay_lo * decay_hi
    else:
        yield fq * before_hi, fk * after_lo, decay_lo * decay_hi


def _fixed_factors(log_decay, forward):
    pos = lax.broadcasted_iota(jnp.int32, (CHUNK, LANES), 0).astype(F32)
    mid = CHUNK // 2
    chunk_decay = jnp.exp(float(CHUNK) * log_decay)
    if forward:
        return [(jnp.exp((pos - (mid - 1.0)) * log_decay), jnp.exp(((mid - 1.0) - pos) * log_decay), None),
                (jnp.exp((pos + 1.0) * log_decay), jnp.exp((CHUNK - 1.0 - pos) * log_decay), chunk_decay)]
    return [(jnp.exp((mid - pos) * log_decay), jnp.exp((pos - mid) * log_decay), None),
            (jnp.exp((CHUNK - pos) * log_decay), jnp.exp(pos * log_decay), chunk_decay)]


def _store_masks(mask_ref, hierarchical):
    row = lax.broadcasted_iota(jnp.int32, (CHUNK, CHUNK), 0)
    col = lax.broadcasted_iota(jnp.int32, (CHUNK, CHUNK), 1)
    one = lambda m: jnp.where(m, 1.0, 0.0).astype(BF16)
    mask_ref[0] = one(row == col)
    n = N_LEVELS if hierarchical else 1
    for level in range(n):
        same_level = ((row ^ col) >> level) == 1 if hierarchical else (row != col)
        mask_ref[1 + level] = one(same_level & (row > col))
        mask_ref[1 + n + level] = one(same_level & (col > row))
    return n


def _lane_masks(n_sub):
    if n_sub == 1:
        return [None]
    lane = lax.broadcasted_iota(jnp.int32, (1, LANES), 1)
    width = LANES // n_sub
    return [jnp.where((lane >= i * width) & (lane < (i + 1) * width), 1.0, 0.0) for i in range(n_sub)]


def _chunk_rows(n):
    if isinstance(n, int):
        return pl.ds(n * CHUNK, CHUNK)
    return pl.ds(pl.multiple_of(n * CHUNK, CHUNK), CHUNK)


class _Mixer(NamedTuple):
    n_groups: int
    n_sub: int
    hierarchical: bool
    prep: Callable
    load_v: Callable
    of_ref: Any
    ob_ref: Any
    st_ref: Any
    mask_ref: Any
    slots: tuple


def _scan_mixers(n_chunks, mixers):
    levels = []
    for m in mixers:
        m.st_ref[...] = jnp.zeros_like(m.st_ref)
        levels.append(_store_masks(m.mask_ref, m.hierarchical))
        if m.hierarchical and m.n_sub > 1:
            for _, kt_ref, _, _ in m.slots:
                kt_ref[...] = jnp.zeros_like(kt_ref)

    def rows_of(n, forward):
        return _chunk_rows(n if forward else n_chunks - 1 - n)

    def chains(m):
        return enumerate((gi, forward) for gi in range(m.n_groups) for forward in (True, False))

    def stage(n, slot):
        for m in mixers:
            qt_ref, kt_ref, p0_ref, dec_ref = m.slots[slot]
            lane_masks = _lane_masks(m.n_sub)
            for c, (gi, forward) in chains(m):
                q, k, factors = m.prep(gi, forward, rows_of(n, forward))
                qb, kb = q.astype(BF16), k.astype(BF16)
                if forward:
                    for i in range(m.n_sub):
                        qm = q if lane_masks[i] is None else q * lane_masks[i]
                        diag = jnp.sum(qm * k, axis=-1, keepdims=True).astype(BF16)
                        p0_ref[gi * m.n_sub + i] = m.mask_ref[0] * diag
                for level, (fq, fk, chunk_decay) in enumerate(factors):
                    qt_ref[c, level] = qb * fq.astype(BF16)
                    if chunk_decay is not None:
                        dec_ref[c] = chunk_decay
                    if m.hierarchical and chunk_decay is None:
                        kt = (kb if fk is None else kb * fk.astype(BF16)).T
                        width = LANES // m.n_sub
                        for i in range(m.n_sub):
                            kt_ref[c, level, i, i * width:(i + 1) * width, 0:CHUNK] = kt[i * width:(i + 1) * width, :]
                    else:
                        kt_ref[c, level, 0, 0:CHUNK, :] = kb * fk.astype(BF16)

    def consume(n, slot):
        for m, n_levels in zip(mixers, levels):
            qt_ref, kt_ref, p0_ref, dec_ref = m.slots[slot]
            lane_masks = _lane_masks(m.n_sub)
            for c, (gi, forward) in chains(m):
                rows = rows_of(n, forward)
                o_ref = m.of_ref if forward else m.ob_ref
                mask0 = 1 if forward else 1 + n_levels
                for i, v in enumerate(m.load_v(gi, rows)):
                    head = gi * m.n_sub + i
                    sub = (lambda t: t) if lane_masks[i] is None else (lambda t, lm=lane_masks[i].astype(BF16): t * lm)
                    p = p0_ref[head] if forward else None
                    for level in range(n_levels):
                        if m.hierarchical:
                            s = _dot(qt_ref[c, level], kt_ref[c, level, i, :, 0:CHUNK])
                        else:
                            s = _dot_nt(sub(qt_ref[c, level]), kt_ref[c, level, 0, 0:CHUNK, :])
                        s = s.astype(BF16) * m.mask_ref[mask0 + level]
                        p = s if p is None else p + s
                    st = m.st_ref[c * m.n_sub + i]
                    inter = _dot_nt(sub(qt_ref[c, n_levels]), st.astype(BF16))
                    o_ref[rows, head * DV:(head + 1) * DV] = (_dot(p, v) + inter).astype(o_ref.dtype)
                    m.st_ref[c * m.n_sub + i] = st * dec_ref[c] + _dot_tn(v, kt_ref[c, n_levels, 0, 0:CHUNK, :])

    def body(step, carry):
        n = 2 * step
        consume(n, 0)
        stage(n + 1, 1)
        consume(n + 1, 1)
        stage(jnp.minimum(n + 2, n_chunks - 1), 0)
        return carry

    stage(0, 0)
    lax.fori_loop(0, n_chunks // 2, body, 0)


def _finish(mixer, gain_ref, gate_ref, y_ref, center, swish):
    rows_per = 256
    n_blocks = mixer.of_ref.shape[0] // rows_per

    def body(b, carry):
        rows = pl.ds(pl.multiple_of(b * rows_per, rows_per), rows_per)
        for i in range(mixer.n_groups * mixer.n_sub):
            cols = slice(i * DV, (i + 1) * DV)
            o = mixer.of_ref[rows, cols].astype(F32) + mixer.ob_ref[rows, cols].astype(F32)
            if center:
                o = o - jnp.mean(o, axis=-1, keepdims=True)
            o = o * lax.rsqrt(jnp.mean(o * o, axis=-1, keepdims=True) + EPS)
            gz = gate_ref[rows, cols].astype(F32)
            act = gz * _sigmoid(gz) if swish else _sigmoid(gz)
            y_ref[rows, cols] = ((o * gain_ref[:, cols]) * act).astype(y_ref.dtype)
        return carry

    lax.fori_loop(0, n_blocks, body, 0)


def _group_cols(gi):
    return slice(gi * LANES, (gi + 1) * LANES)


N_SLOTS = 2
N_MIXER_SCRATCH = 4 + 4 * N_SLOTS


def _mixer(n_groups, n_sub, hierarchical, prep, load_v, scratch):
    of_ref, ob_ref, st_ref, mask_ref = scratch[:4]
    return _Mixer(n_groups, n_sub, hierarchical, prep, load_v, of_ref, ob_ref, st_ref, mask_ref,
                  tuple(scratch[4 + 4 * s:8 + 4 * s] for s in range(N_SLOTS)))


def _mixer_scratch(seqlen, n_groups, n_sub, hierarchical):
    n_heads, n_chains = n_groups * n_sub, 2 * n_groups
    n_levels = N_LEVELS if hierarchical else 1
    key_copies = n_sub if hierarchical else 1
    slot = [pltpu.VMEM((n_chains, n_levels + 1, CHUNK, LANES), BF16),
            pltpu.VMEM((n_chains, n_levels + 1, key_copies, LANES, LANES), BF16),
            pltpu.VMEM((n_heads, CHUNK, CHUNK), BF16),
            pltpu.VMEM((n_chains, 1, LANES), F32)]
    return [pltpu.VMEM((seqlen, n_heads * DV), BF16),
            pltpu.VMEM((seqlen, n_heads * DV), BF16),
            pltpu.VMEM((2 * n_heads, DV, LANES), F32),
            pltpu.VMEM((1 + 2 * n_levels, CHUNK, CHUNK), BF16)] + N_SLOTS * slot


def _hgrn2_mixer(q_ref, v_ref, ff_ref, fb_ref, lb_ref, scratch):
    def prep(gi, forward, rows):
        cols = _group_cols(gi)
        lb = lb_ref[:, cols]
        z = (ff_ref if forward else fb_ref)[rows, cols]
        f = lb + (1.0 - lb) * _sigmoid(z)
        g2 = jnp.log2(jnp.maximum(f, TINY))
        key = 1.0 - f
        aq = q_ref[rows, cols].astype(F32)
        return aq * _sigmoid(aq), key, _gated_factors(g2, forward)

    return _mixer(q_ref.shape[1] // LANES, 1, True, prep,
                  lambda gi, rows: [v_ref[rows, _group_cols(gi)]], scratch)


def _rotate_half(x):
    lane = lax.broadcasted_iota(jnp.int32, x.shape, 1)
    half = BC_DK // 2
    first = (lane % BC_DK) < half
    return jnp.where(first, pltpu.roll(x, LANES - half, axis=1), pltpu.roll(x, half, axis=1))


def _sub_values(v_ref, gi, rows):
    return [v_ref[rows, (2 * gi + i) * DV:(2 * gi + i + 1) * DV] for i in range(2)]


def _retention_mixer(q_ref, k_ref, v_ref, cos_ref, sin_ref, lgf_ref, lgb_ref, scratch):
    n_groups = q_ref.shape[1] // LANES
    packs = [(_fixed_factors(lgb_ref[:, _group_cols(gi)], False), _fixed_factors(lgf_ref[:, _group_cols(gi)], True))
             for gi in range(n_groups)]

    def prep(gi, forward, rows):
        cols = _group_cols(gi)
        cos = cos_ref[rows, :]
        sin = sin_ref[rows, :]
        q = q_ref[rows, cols].astype(F32)
        k = k_ref[rows, cols].astype(F32)
        q = q * cos + _rotate_half(q) * sin
        k = (k * cos + _rotate_half(k) * sin) * (BC_DK ** -0.5)
        return q, k, packs[gi][1 if forward else 0]

    return _mixer(n_groups, 2, False, prep, functools.partial(_sub_values, v_ref), scratch)


def _gla_mixer(q_ref, k_ref, v_ref, lr_ref, waf_ref, wab_ref, baf_ref, bab_ref, scratch):
    def prep(gi, forward, rows):
        cols = _group_cols(gi)
        w_ref, b_ref = (waf_ref, baf_ref) if forward else (wab_ref, bab_ref)
        x = _dot(lr_ref[rows, :].astype(BF16), w_ref[:, cols]) + b_ref[:, cols]
        log_sig = jnp.minimum(x, 0.0) - jnp.log(1.0 + jnp.exp(-jnp.abs(x)))
        g2 = log_sig * (LOG2_E / GLA_TAU)
        q = q_ref[rows, cols].astype(F32) * (BC_DK ** -0.5)
        return q, k_ref[rows, cols].astype(F32), _gated_factors(g2, forward)

    return _mixer(q_ref.shape[1] // LANES, 2, True, prep, functools.partial(_sub_values, v_ref), scratch)


def _cols(ref, start, width):
    return ref.at[:, start:start + width]


def _mixers_kernel(zb_ref, zf_ref, lr_ref, lb_ref, again_ref,
                   cos_ref, sin_ref, lgf_ref, lgb_ref, bgain_ref,
                   waf_ref, wab_ref, baf_ref, bab_ref, cgain_ref,
                   ya_ref, yb_ref, yc_ref, *scratch):
    pair, qk = 2 * DV, 2 * BC_DK
    s1, s2 = N_MIXER_SCRATCH, 2 * N_MIXER_SCRATCH
    hgrn2 = _hgrn2_mixer(_cols(zb_ref, PAIR_AQ, pair), _cols(zb_ref, PAIR_AI, pair),
                         _cols(zf_ref, PAIR_FF, pair), _cols(zf_ref, PAIR_FB, pair), lb_ref, scratch[:s1])
    retention = _retention_mixer(_cols(zb_ref, PAIR_BQ, qk), _cols(zb_ref, PAIR_BK, qk),
                                 _cols(zb_ref, PAIR_BV, pair), cos_ref, sin_ref, lgf_ref, lgb_ref, scratch[s1:s2])
    gla = _gla_mixer(_cols(zb_ref, PAIR_CQ, qk), _cols(zb_ref, PAIR_CK, qk), _cols(zb_ref, PAIR_CV, pair),
                     lr_ref, waf_ref, wab_ref, baf_ref, bab_ref, scratch[s2:])
    _scan_mixers(zb_ref.shape[0] // CHUNK, [hgrn2, retention, gla])
    _finish(hgrn2, again_ref, _cols(zb_ref, PAIR_AG, pair), ya_ref, center=False, swish=False)
    _finish(retention, bgain_ref, _cols(zb_ref, PAIR_BG, pair), yb_ref, center=True, swish=True)
    _finish(gla, cgain_ref, _cols(zb_ref, PAIR_CG, pair), yc_ref, center=False, swish=True)


def _zspec(seqlen, width, col0):
    blk = col0 // width
    return pl.BlockSpec((seqlen, width), lambda b, g: (b, blk + g))


def _vspec(width, blk0=0):
    return pl.BlockSpec((1, width), lambda b, g: (0, blk0 + g))


def _mixers(zb, zf, lb, gain_a, cos, sin, lgf, lgb, gain_b, wal, bal, gain_c, bsz, seqlen):
    t, w = seqlen, 2 * DV
    n_steps = N_HEADS * DV // w
    half = N_HEADS * BC_DK // LANES
    out = pl.BlockSpec((t, w), lambda b, g: (b, g))
    table = pl.BlockSpec((t, LANES), lambda b, g: (0, 0), pipeline_mode=pl.Buffered(1))
    return pl.pallas_call(
        _mixers_kernel,
        grid=(bsz, n_steps),
        in_specs=[_zspec(t, ZB_PAIR_COLS, 0), _zspec(t, ZF_PAIR_COLS, 0),
                  pl.BlockSpec((t, LANES), lambda b, g: (b, ZF_LR // LANES)),
                  _vspec(w), _vspec(w),
                  table, table, _vspec(LANES), _vspec(LANES), _vspec(w),
                  pl.BlockSpec((LANES, LANES), lambda b, g: (0, g)),
                  pl.BlockSpec((LANES, LANES), lambda b, g: (0, half + g)),
                  _vspec(LANES), _vspec(LANES, half), _vspec(w)],
        out_specs=[out, out, out],
        out_shape=[jax.ShapeDtypeStruct((bsz * t, n_steps * w), BF16)] * 3,
        scratch_shapes=(_mixer_scratch(t, 2, 1, True) + _mixer_scratch(t, 1, 2, False)
                        + _mixer_scratch(t, 1, 2, True)),
        compiler_params=pltpu.CompilerParams(dimension_semantics=("parallel", "parallel"),
                                             vmem_limit_bytes=VMEM_LIMIT_MIXERS),
        name="mixers",
    )(zb, zf, zf, lb, gain_a, cos, sin, lgf, lgb, gain_b, wal, wal, bal, bal, gain_c)


def _split_w_in(w):
    o = {}
    off = 0
    for name, size in (("a_q", 512), ("a_ff", 512), ("a_fb", 512), ("a_i", 512), ("a_g", 512),
                       ("b_q", 256), ("b_k", 256), ("b_v", 512), ("b_g", 512),
                       ("c_q", 256), ("c_k", 256), ("c_v", 512), ("c_g", 512), ("c_lr", 32),
                       ("gate_a", 1024), ("gate_b", 1024), ("gate_c", 1024)):
        o[name] = w[..., off:off + size]
        off += size
    n_pairs = N_HEADS // 2
    part = lambda name, g: o[name][..., g * (o[name].shape[-1] // n_pairs):(g + 1) * (o[name].shape[-1] // n_pairs)]
    pairs = [part(n, g) for g in range(n_pairs)
             for n in ("a_q", "a_i", "a_g", "c_q", "c_k", "c_v", "c_g", "b_q", "b_k", "b_v", "b_g")]
    wb = jnp.concatenate(pairs + [o["gate_a"], o["gate_b"], o["gate_c"]], axis=-1)
    pad = jnp.zeros(w.shape[:-1] + (ZF_COLS - ZF_LR - 2 * GLA_RANK,), w.dtype)
    gates = [part(n, g) for g in range(n_pairs) for n in ("a_ff", "a_fb")]
    wf = jnp.concatenate(gates + [o["c_lr"], pad], axis=-1)
    return wb.astype(BF16), wf.astype(BF16)


def _rotary_tables(seqlen):
    pos = jnp.arange(seqlen, dtype=F32)
    inv_freq = 10000.0 ** (-jnp.arange(0, BC_DK, 2, dtype=F32) / BC_DK)
    ang = pos[:, None] * inv_freq[None, :]
    cos, sin = jnp.cos(ang), jnp.sin(ang)
    reps = LANES // BC_DK
    return (jnp.tile(jnp.concatenate([cos, cos], axis=1), (1, reps)),
            jnp.tile(jnp.concatenate([-sin, sin], axis=1), (1, reps)))


def _retention_log_decays():
    h = jnp.arange(N_HEADS, dtype=F32)
    fwd = jnp.log1p(-jnp.exp2(-5.0 - h))
    bwd = jnp.log1p(-jnp.exp2(-5.0 - h[::-1]))
    spread = lambda v: jnp.repeat(v, BC_DK)[None, :]
    return spread(fwd), spread(bwd)


def kernel(x, c, norm1_g, w_ada, b_ada, w_in, lb_logits, norm_a_g, norm_b_g, norm_c_g, w_alpha, b_alpha,
           w_pa, w_pb, w_pc, w_out, norm2_g, w_ffn_in, w_ffn_out, norm_f_g):
    bsz, seqlen, d = x.shape
    depth = w_in.shape[0]
    assert d == D_MODEL and seqlen % (2 * CHUNK) == 0
    tm = min(TM_DENSE, seqlen)
    tm_in = min(TM_INPROJ, seqlen)
    assert seqlen % tm == 0 and seqlen % tm_in == 0

    mod = _modulation(c, w_ada, b_ada)
    lbs = _lower_bounds(lb_logits)
    cos, sin = _rotary_tables(seqlen)
    lgf, lgb = _retention_log_decays()
    wb_all, wf_all = _split_w_in(w_in)
    dense_weights = [w.astype(BF16) for w in (w_pa, w_pb, w_pc, w_out)]
    ffn_weights = (w_ffn_in.astype(BF16), w_ffn_out.astype(BF16))

    x2 = x.reshape(bsz * seqlen, d)
    for l in range(depth):
        sh1, sc1, g1, sh2, sc2, g2 = [mod[l, :, i * d:(i + 1) * d].reshape(bsz, 1, d) for i in range(6)]
        gain1 = norm1_g[l].reshape(1, d)
        zb, zf = _inproj(x2, sh1, sc1, gain1, wb_all, wf_all, l, seqlen, tm_in)

        wal = jnp.zeros((LANES, 2 * N_HEADS * BC_DK), F32)
        wal = wal.at[0:GLA_RANK, 0:N_HEADS * BC_DK].set(w_alpha[l, 0])
        wal = wal.at[GLA_RANK:2 * GLA_RANK, N_HEADS * BC_DK:].set(w_alpha[l, 1])
        bal = b_alpha[l].reshape(1, 2 * N_HEADS * BC_DK)

        ya, yb, yc = _mixers(zb, zf, lbs[l].reshape(1, -1), norm_a_g[l].reshape(1, -1),
                             cos, sin, lgf, lgb, norm_b_g[l].reshape(1, -1),
                             wal.astype(BF16), bal, norm_c_g[l].reshape(1, -1), bsz, seqlen)

        x2 = _merge_ffn(x2, ya, yb, yc, zb, *dense_weights, g1, sh2, sc2, g2, norm2_g[l].reshape(1, d),
                        *ffn_weights, norm_f_g.reshape(1, d), l, seqlen, tm, final_norm=(l == depth - 1))
    return x2.reshape(bsz, seqlen, d)
```

```python
import functools
from typing import Any, Callable, NamedTuple

import jax
import jax.numpy as jnp
from jax import lax
from jax.experimental import pallas as pl
from jax.experimental.pallas import tpu as pltpu

F32 = jnp.float32
BF16 = jnp.bfloat16

D_MODEL = 1024
N_HEADS = 4
BC_DK = 64
DV = 128
GLA_RANK = 16
GLA_TAU = 16.0
D_FF = 2816
EPS = 1e-6
TINY = 1e-30
LOG2_E = 1.4426950408889634

LANES = 128
CHUNK = 128
assert CHUNK <= LANES
VMEM_LIMIT = 48 * 1024 * 1024
VMEM_LIMIT_MIXERS = 58 * 1024 * 1024

TM_INPROJ, TN_INPROJ = 512, 1280
TM_DENSE = 512
TF_FFN = D_FF // 2

ZB_PAIR_COLS = 2304
PAIR_AQ, PAIR_AI, PAIR_AG = 0, 256, 512
PAIR_CQ, PAIR_CK, PAIR_CV, PAIR_CG = 768, 896, 1024, 1280
PAIR_BQ, PAIR_BK, PAIR_BV, PAIR_BG = 1536, 1664, 1792, 2048
ZB_GATE = 4608
ZB_GATE_BLOCK = 1536
ZF_PAIR_COLS = 512
PAIR_FF, PAIR_FB = 0, 256
ZF_LR = 1024
ZF_COLS = 1152


def _dot(a, b):
    return jnp.dot(a, b, preferred_element_type=F32)


def _dot_nt(a, b):
    return lax.dot_general(a, b, (((1,), (1,)), ((), ())), preferred_element_type=F32)


def _dot_tn(a, b):
    return lax.dot_general(a, b, (((0,), (0,)), ((), ())), preferred_element_type=F32)


def _sigmoid(x):
    return jax.nn.sigmoid(x)


def _mod_kernel(c_ref, w_ref, b_ref, o_ref):
    c = c_ref[...]
    a = c * _sigmoid(c)
    w = w_ref[...]
    a_hi = a.astype(BF16)
    a_lo = (a - a_hi.astype(F32)).astype(BF16)
    w_hi = w.astype(BF16)
    w_lo = (w - w_hi.astype(F32)).astype(BF16)
    o_ref[...] = _dot(a_hi, w_hi) + _dot(a_hi, w_lo) + _dot(a_lo, w_hi) + b_ref[...]


def _modulation(c, w_ada, b_ada):
    depth, d, n6 = w_ada.shape
    bsz = c.shape[0]
    return pl.pallas_call(
        _mod_kernel,
        grid=(depth, n6 // d),
        in_specs=[
            pl.BlockSpec((bsz, d), lambda l, j: (0, 0)),
            pl.BlockSpec((None, d, d), lambda l, j: (l, 0, j)),
            pl.BlockSpec((None, 1, d), lambda l, j: (l, 0, j)),
        ],
        out_specs=pl.BlockSpec((None, bsz, d), lambda l, j: (l, 0, j)),
        out_shape=jax.ShapeDtypeStruct((depth, bsz, n6), F32),
        name="adaln_modulation",
    )(c, w_ada, b_ada.reshape(depth, 1, n6))


def _lb_kernel(x_ref, o_ref):
    depth = x_ref.shape[0]
    rows = [x_ref[i:i + 1, :] for i in range(depth)]
    m = rows[0]
    for r in rows[1:]:
        m = jnp.maximum(m, r)
    e = [jnp.exp(r - m) for r in rows]
    s = e[0]
    for t in e[1:]:
        s = s + t
    p = [t / s for t in e]
    acc = p[0]
    o_ref[0:1, :] = jnp.maximum(acc - p[0], 0.0)
    for i in range(1, depth):
        acc = acc + p[i]
        o_ref[i:i + 1, :] = jnp.maximum(acc - p[0], 0.0)


def _lower_bounds(lb_logits):
    return pl.pallas_call(
        _lb_kernel,
        out_shape=jax.ShapeDtypeStruct(lb_logits.shape, F32),
        name="hgrn2_lower_bounds",
    )(lb_logits)


def _norm_mod(x, gain, sc, sh):
    y = x * lax.rsqrt(jnp.mean(x * x, axis=-1, keepdims=True) + EPS)
    return (y * gain) * (1.0 + sc) + sh


def _inproj_kernel(x_ref, sh_ref, sc_ref, g_ref, wb_ref, wf_ref, zb_ref, zf_ref, h_ref):
    h_ref[...] = _norm_mod(x_ref[...], g_ref[...], sc_ref[0], sh_ref[0]).astype(BF16)
    for j in range(zb_ref.shape[1] // TN_INPROJ):
        cols = slice(j * TN_INPROJ, (j + 1) * TN_INPROJ)
        zb_ref[:, cols] = _dot(h_ref[...], wb_ref[:, cols]).astype(zb_ref.dtype)
    zf_ref[...] = _dot(h_ref[...], wf_ref[...])


def _resident(stacked, layer):
    return pl.BlockSpec((None,) + stacked.shape[1:], lambda i: (layer, 0, 0), pipeline_mode=pl.Buffered(1))


def _inproj(x2, sh, sc, gain, wb, wf, layer, seqlen, tm):
    m, d = x2.shape
    per = seqlen // tm
    bvec = lambda i: (i // per, 0, 0)
    resident = lambda w: _resident(w, layer)
    return pl.pallas_call(
        _inproj_kernel,
        grid=(m // tm,),
        in_specs=[
            pl.BlockSpec((tm, d), lambda i: (i, 0)),
            pl.BlockSpec((1, 1, d), bvec), pl.BlockSpec((1, 1, d), bvec),
            pl.BlockSpec((1, d), lambda i: (0, 0)),
            resident(wb), resident(wf),
        ],
        out_specs=[pl.BlockSpec((tm, wb.shape[2]), lambda i: (i, 0)),
                   pl.BlockSpec((tm, wf.shape[2]), lambda i: (i, 0))],
        out_shape=[jax.ShapeDtypeStruct((m, wb.shape[2]), BF16), jax.ShapeDtypeStruct((m, wf.shape[2]), F32)],
        scratch_shapes=[pltpu.VMEM((tm, d), BF16)],
        compiler_params=pltpu.CompilerParams(dimension_semantics=("parallel",), vmem_limit_bytes=VMEM_LIMIT),
        name="inproj",
    )(x2, sh, sc, gain, wb, wf)


def _merge_ffn_kernel(x_ref, ya_ref, yb_ref, yc_ref, gates_lo_ref, gates_hi_ref,
                      wpa_ref, wpb_ref, wpc_ref, wout_ref, g1_ref,
                      sh_ref, sc_ref, g2_ref, ng_ref, wi_ref, wo_ref, nf_ref, o_ref, h_ref, *, final_norm):
    d = x_ref.shape[1]
    dff = wo_ref.shape[0]
    gates = jnp.concatenate([gates_lo_ref[...], gates_hi_ref[...]], axis=1).astype(F32)
    merged = (_sigmoid(gates[:, 0:d]) * _dot(ya_ref[...], wpa_ref[...])
              + _sigmoid(gates[:, d:2 * d]) * _dot(yb_ref[...], wpb_ref[...])
              + _sigmoid(gates[:, 2 * d:3 * d]) * _dot(yc_ref[...], wpc_ref[...]))
    x = x_ref[...] + g1_ref[0] * _dot(merged.astype(BF16), wout_ref[...])
    h_ref[...] = _norm_mod(x, ng_ref[...], sc_ref[0], sh_ref[0]).astype(BF16)
    acc = None
    for k in range(dff // TF_FFN):
        gate = _dot(h_ref[...], wi_ref[:, k * TF_FFN:(k + 1) * TF_FFN])
        up = _dot(h_ref[...], wi_ref[:, dff + k * TF_FFN:dff + (k + 1) * TF_FFN])
        act = (gate * _sigmoid(gate) * up).astype(BF16)
        part = _dot(act, wo_ref[k * TF_FFN:(k + 1) * TF_FFN, :])
        acc = part if acc is None else acc + part
    xn = x + g2_ref[0] * acc
    if final_norm:
        xn = xn * lax.rsqrt(jnp.mean(xn * xn, axis=-1, keepdims=True) + EPS) * nf_ref[...]
    o_ref[...] = xn


def _merge_ffn(x2, ya, yb, yc, zb, wpa, wpb, wpc, wout, g1, sh, sc, g2, ng, w_in, w_out, nf, layer, seqlen, tm,
               final_norm):
    m, d = x2.shape
    w = ya.shape[1]
    per = seqlen // tm
    row = lambda i: (i, 0)
    bvec = pl.BlockSpec((1, 1, d), lambda i: (i // per, 0, 0))
    vec = pl.BlockSpec((1, d), lambda i: (0, 0))
    resident = lambda a: _resident(a, layer)
    return pl.pallas_call(
        functools.partial(_merge_ffn_kernel, final_norm=final_norm),
        grid=(m // tm,),
        in_specs=[
            pl.BlockSpec((tm, d), row),
            pl.BlockSpec((tm, w), row), pl.BlockSpec((tm, w), row), pl.BlockSpec((tm, w), row),
            pl.BlockSpec((tm, ZB_GATE_BLOCK), lambda i: (i, ZB_GATE // ZB_GATE_BLOCK)),
            pl.BlockSpec((tm, ZB_GATE_BLOCK), lambda i: (i, ZB_GATE // ZB_GATE_BLOCK + 1)),
            resident(wpa), resident(wpb), resident(wpc), resident(wout), bvec,
            bvec, bvec, bvec, vec, resident(w_in), resident(w_out), vec,
        ],
        out_specs=pl.BlockSpec((tm, d), row),
        out_shape=jax.ShapeDtypeStruct((m, d), F32),
        scratch_shapes=[pltpu.VMEM((tm, d), BF16)],
        compiler_params=pltpu.CompilerParams(dimension_semantics=("parallel",),
                                             vmem_limit_bytes=VMEM_LIMIT_MIXERS),
        name="merge_ffn",
    )(x2, ya, yb, yc, zb, zb, wpa, wpb, wpc, wout, g1, sh, sc, g2, ng, w_in, w_out, nf)


N_LEVELS = CHUNK.bit_length() - 1


def _gated_factors(g2, forward):
    row = lax.broadcasted_iota(jnp.int32, g2.shape, 0)
    yield jnp.exp2(g2), None, None
    lb, tb = g2, g2
    half = CHUNK // 2
    b = 1
    while b < half:
        right = (row & b) != 0
        from_left = pltpu.roll(tb, b, axis=0)
        from_right = pltpu.roll(tb, CHUNK - b, axis=0)
        lb = lb + jnp.where(right, from_left, 0.0)
        tb = tb + jnp.where(right, from_left, from_right)
        b *= 2
        rest = tb - lb
        fq, fk = (jnp.exp2(lb), jnp.exp2(rest)) if forward else (jnp.exp2(rest + g2), jnp.exp2(lb - g2))
        yield fq, fk, None
    decay_lo, decay_hi = jnp.exp2(tb[0:1, :]), jnp.exp2(tb[half:half + 1, :])
    after_lo = jnp.where(row >= half, decay_lo, 1.0)
    before_hi = jnp.where(row < half, decay_hi, 1.0)
    if forward:
        yield fq * after_lo, fk * before_hi, decay_lo * decay_hi
    else:
        yield fq * before_hi, fk * after_lo, decay_lo * decay_hi


def _fixed_factors(log_decay, forward):
    pos = lax.broadcasted_iota(jnp.int32, (CHUNK, LANES), 0).astype(F32)
    mid = CHUNK // 2
    chunk_decay = jnp.exp(float(CHUNK) * log_decay)
    if forward:
        return [(jnp.exp((pos - (mid - 1.0)) * log_decay), jnp.exp(((mid - 1.0) - pos) * log_decay), None),
                (jnp.exp((pos + 1.0) * log_decay), jnp.exp((CHUNK - 1.0 - pos) * log_decay), chunk_decay)]
    return [(jnp.exp((mid - pos) * log_decay), jnp.exp((pos - mid) * log_decay), None),
            (jnp.exp((CHUNK - pos) * log_decay), jnp.exp(pos * log_decay), chunk_decay)]


def _store_masks(mask_ref, hierarchical):
    row = lax.broadcasted_iota(jnp.int32, (CHUNK, CHUNK), 0)
    col = lax.broadcasted_iota(jnp.int32, (CHUNK, CHUNK), 1)
    one = lambda m: jnp.where(m, 1.0, 0.0).astype(BF16)
    mask_ref[0] = one(row == col)
    n = N_LEVELS if hierarchical else 1
    for level in range(n):
        same_level = ((row ^ col) >> level) == 1 if hierarchical else (row != col)
        mask_ref[1 + level] = one(same_level & (row > col))
        mask_ref[1 + n + level] = one(same_level & (col > row))
    return n


def _lane_masks(n_sub):
    if n_sub == 1:
        return [None]
    lane = lax.broadcasted_iota(jnp.int32, (1, LANES), 1)
    width = LANES // n_sub
    return [jnp.where((lane >= i * width) & (lane < (i + 1) * width), 1.0, 0.0) for i in range(n_sub)]


def _chunk_rows(n):
    if isinstance(n, int):
        return pl.ds(n * CHUNK, CHUNK)
    return pl.ds(pl.multiple_of(n * CHUNK, CHUNK), CHUNK)


class _Mixer(NamedTuple):
    n_groups: int
    n_sub: int
    hierarchical: bool
    prep: Callable
    load_v: Callable
    of_ref: Any
    ob_ref: Any
    st_ref: Any
    mask_ref: Any
    slots: tuple


def _scan_mixers(n_chunks, mixers):
    levels = []
    for m in mixers:
        m.st_ref[...] = jnp.zeros_like(m.st_ref)
        levels.append(_store_masks(m.mask_ref, m.hierarchical))
        if m.hierarchical and m.n_sub > 1:
            for _, kt_ref, _, _ in m.slots:
                kt_ref[...] = jnp.zeros_like(kt_ref)

    def rows_of(n, forward):
        return _chunk_rows(n if forward else n_chunks - 1 - n)

    def chains(m):
        return enumerate((gi, forward) for gi in range(m.n_groups) for forward in (True, False))

    def stage(n, slot):
        for m in mixers:
            qt_ref, kt_ref, p0_ref, dec_ref = m.slots[slot]
            lane_masks = _lane_masks(m.n_sub)
            for c, (gi, forward) in chains(m):
                q, k, factors = m.prep(gi, forward, rows_of(n, forward))
                qb, kb = q.astype(BF16), k.astype(BF16)
                if forward:
                    for i in range(m.n_sub):
                        qm = q if lane_masks[i] is None else q * lane_masks[i]
                        diag = jnp.sum(qm * k, axis=-1, keepdims=True).astype(BF16)
                        p0_ref[gi * m.n_sub + i] = m.mask_ref[0] * diag
                for level, (fq, fk, chunk_decay) in enumerate(factors):
                    qt_ref[c, level] = qb * fq.astype(BF16)
                    if chunk_decay is not None:
                        dec_ref[c] = chunk_decay
                    if m.hierarchical and chunk_decay is None:
                        kt = (kb if fk is None else kb * fk.astype(BF16)).T
                        width = LANES // m.n_sub
                        for i in range(m.n_sub):
                            kt_ref[c, level, i, i * width:(i + 1) * width, 0:CHUNK] = kt[i * width:(i + 1) * width, :]
                    else:
                        kt_ref[c, level, 0, 0:CHUNK, :] = kb * fk.astype(BF16)

    def consume(n, slot):
        for m, n_levels in zip(mixers, levels):
            qt_ref, kt_ref, p0_ref, dec_ref = m.slots[slot]
            lane_masks = _lane_masks(m.n_sub)
            for c, (gi, forward) in chains(m):
                rows = rows_of(n, forward)
                o_ref = m.of_ref if forward else m.ob_ref
                mask0 = 1 if forward else 1 + n_levels
                for i, v in enumerate(m.load_v(gi, rows)):
                    head = gi * m.n_sub + i
                    sub = (lambda t: t) if lane_masks[i] is None else (lambda t, lm=lane_masks[i].astype(BF16): t * lm)
                    p = p0_ref[head] if forward else None
                    for level in range(n_levels):
                        if m.hierarchical:
                            s = _dot(qt_ref[c, level], kt_ref[c, level, i, :, 0:CHUNK])
                        else:
                            s = _dot_nt(sub(qt_ref[c, level]), kt_ref[c, level, 0, 0:CHUNK, :])
                        s = s.astype(BF16) * m.mask_ref[mask0 + level]
                        p = s if p is None else p + s
                    st = m.st_ref[c * m.n_sub + i]
                    inter = _dot_nt(sub(qt_ref[c, n_levels]), st.astype(BF16))
                    o_ref[rows, head * DV:(head + 1) * DV] = (_dot(p, v) + inter).astype(o_ref.dtype)
                    m.st_ref[c * m.n_sub + i] = st * dec_ref[c] + _dot_tn(v, kt_ref[c, n_levels, 0, 0:CHUNK, :])

    def body(step, carry):
        n = 2 * step
        consume(n, 0)
        stage(n + 1, 1)
        consume(n + 1, 1)
        stage(jnp.minimum(n + 2, n_chunks - 1), 0)
        return carry

    stage(0, 0)
    lax.fori_loop(0, n_chunks // 2, body, 0)


def _finish(mixer, gain_ref, gate_ref, y_ref, center, swish):
    rows_per = 256
    n_blocks = mixer.of_ref.shape[0] // rows_per

    def body(b, carry):
        rows = pl.ds(pl.multiple_of(b * rows_per, rows_per), rows_per)
        for i in range(mixer.n_groups * mixer.n_sub):
            cols = slice(i * DV, (i + 1) * DV)
            o = mixer.of_ref[rows, cols].astype(F32) + mixer.ob_ref[rows, cols].astype(F32)
            if center:
                o = o - jnp.mean(o, axis=-1, keepdims=True)
            o = o * lax.rsqrt(jnp.mean(o * o, axis=-1, keepdims=True) + EPS)
            gz = gate_ref[rows, cols].astype(F32)
            act = gz * _sigmoid(gz) if swish else _sigmoid(gz)
            y_ref[rows, cols] = ((o * gain_ref[:, cols]) * act).astype(y_ref.dtype)
        return carry

    lax.fori_loop(0, n_blocks, body, 0)


def _group_cols(gi):
    return slice(gi * LANES, (gi + 1) * LANES)


N_SLOTS = 2
N_MIXER_SCRATCH = 4 + 4 * N_SLOTS


def _mixer(n_groups, n_sub, hierarchical, prep, load_v, scratch):
    of_ref, ob_ref, st_ref, mask_ref = scratch[:4]
    return _Mixer(n_groups, n_sub, hierarchical, prep, load_v, of_ref, ob_ref, st_ref, mask_ref,
                  tuple(scratch[4 + 4 * s:8 + 4 * s] for s in range(N_SLOTS)))


def _mixer_scratch(seqlen, n_groups, n_sub, hierarchical):
    n_heads, n_chains = n_groups * n_sub, 2 * n_groups
    n_levels = N_LEVELS if hierarchical else 1
    key_copies = n_sub if hierarchical else 1
    slot = [pltpu.VMEM((n_chains, n_levels + 1, CHUNK, LANES), BF16),
            pltpu.VMEM((n_chains, n_levels + 1, key_copies, LANES, LANES), BF16),
            pltpu.VMEM((n_heads, CHUNK, CHUNK), BF16),
            pltpu.VMEM((n_chains, 1, LANES), F32)]
    return [pltpu.VMEM((seqlen, n_heads * DV), BF16),
            pltpu.VMEM((seqlen, n_heads * DV), BF16),
            pltpu.VMEM((2 * n_heads, DV, LANES), F32),
            pltpu.VMEM((1 + 2 * n_levels, CHUNK, CHUNK), BF16)] + N_SLOTS * slot


def _hgrn2_mixer(q_ref, v_ref, ff_ref, fb_ref, lb_ref, scratch):
    def prep(gi, forward, rows):
        cols = _group_cols(gi)
        lb = lb_ref[:, cols]
        z = (ff_ref if forward else fb_ref)[rows, cols]
        f = lb + (1.0 - lb) * _sigmoid(z)
        g2 = jnp.log2(jnp.maximum(f, TINY))
        key = 1.0 - f
        aq = q_ref[rows, cols].astype(F32)
        return aq * _sigmoid(aq), key, _gated_factors(g2, forward)

    return _mixer(q_ref.shape[1] // LANES, 1, True, prep,
                  lambda gi, rows: [v_ref[rows, _group_cols(gi)]], scratch)


def _rotate_half(x):
    lane = lax.broadcasted_iota(jnp.int32, x.shape, 1)
    half = BC_DK // 2
    first = (lane % BC_DK) < half
    return jnp.where(first, pltpu.roll(x, LANES - half, axis=1), pltpu.roll(x, half, axis=1))


def _sub_values(v_ref, gi, rows):
    return [v_ref[rows, (2 * gi + i) * DV:(2 * gi + i + 1) * DV] for i in range(2)]


def _retention_mixer(q_ref, k_ref, v_ref, cos_ref, sin_ref, lgf_ref, lgb_ref, scratch):
    n_groups = q_ref.shape[1] // LANES
    packs = [(_fixed_factors(lgb_ref[:, _group_cols(gi)], False), _fixed_factors(lgf_ref[:, _group_cols(gi)], True))
             for gi in range(n_groups)]

    def prep(gi, forward, rows):
        cols = _group_cols(gi)
        cos = cos_ref[rows, :]
        sin = sin_ref[rows, :]
        q = q_ref[rows, cols].astype(F32)
        k = k_ref[rows, cols].astype(F32)
        q = q * cos + _rotate_half(q) * sin
        k = (k * cos + _rotate_half(k) * sin) * (BC_DK ** -0.5)
        return q, k, packs[gi][1 if forward else 0]

    return _mixer(n_groups, 2, False, prep, functools.partial(_sub_values, v_ref), scratch)


def _gla_mixer(q_ref, k_ref, v_ref, lr_ref, waf_ref, wab_ref, baf_ref, bab_ref, scratch):
    def prep(gi, forward, rows):
        cols = _group_cols(gi)
        w_ref, b_ref = (waf_ref, baf_ref) if forward else (wab_ref, bab_ref)
        x = _dot(lr_ref[rows, :].astype(BF16), w_ref[:, cols]) + b_ref[:, cols]
        log_sig = jnp.minimum(x, 0.0) - jnp.log(1.0 + jnp.exp(-jnp.abs(x)))
        g2 = log_sig * (LOG2_E / GLA_TAU)
        q = q_ref[rows, cols].astype(F32) * (BC_DK ** -0.5)
        return q, k_ref[rows, cols].astype(F32), _gated_factors(g2, forward)

    return _mixer(q_ref.shape[1] // LANES, 2, True, prep, functools.partial(_sub_values, v_ref), scratch)


def _cols(ref, start, width):
    return ref.at[:, start:start + width]


def _mixers_kernel(zb_ref, zf_ref, lr_ref, lb_ref, again_ref,
                   cos_ref, sin_ref, lgf_ref, lgb_ref, bgain_ref,
                   waf_ref, wab_ref, baf_ref, bab_ref, cgain_ref,
                   ya_ref, yb_ref, yc_ref, *scratch):
    pair, qk = 2 * DV, 2 * BC_DK
    s1, s2 = N_MIXER_SCRATCH, 2 * N_MIXER_SCRATCH
    hgrn2 = _hgrn2_mixer(_cols(zb_ref, PAIR_AQ, pair), _cols(zb_ref, PAIR_AI, pair),
                         _cols(zf_ref, PAIR_FF, pair), _cols(zf_ref, PAIR_FB, pair), lb_ref, scratch[:s1])
    retention = _retention_mixer(_cols(zb_ref, PAIR_BQ, qk), _cols(zb_ref, PAIR_BK, qk),
                                 _cols(zb_ref, PAIR_BV, pair), cos_ref, sin_ref, lgf_ref, lgb_ref, scratch[s1:s2])
    gla = _gla_mixer(_cols(zb_ref, PAIR_CQ, qk), _cols(zb_ref, PAIR_CK, qk), _cols(zb_ref, PAIR_CV, pair),
                     lr_ref, waf_ref, wab_ref, baf_ref, bab_ref, scratch[s2:])
    _scan_mixers(zb_ref.shape[0] // CHUNK, [hgrn2, retention, gla])
    _finish(hgrn2, again_ref, _cols(zb_ref, PAIR_AG, pair), ya_ref, center=False, swish=False)
    _finish(retention, bgain_ref, _cols(zb_ref, PAIR_BG, pair), yb_ref, center=True, swish=True)
    _finish(gla, cgain_ref, _cols(zb_ref, PAIR_CG, pair), yc_ref, center=False, swish=True)


def _zspec(seqlen, width, col0):
    blk = col0 // width
    return pl.BlockSpec((seqlen, width), lambda b, g: (b, blk + g))


def _vspec(width, blk0=0):
    return pl.BlockSpec((1, width), lambda b, g: (0, blk0 + g))


def _mixers(zb, zf, lb, gain_a, cos, sin, lgf, lgb, gain_b, wal, bal, gain_c, bsz, seqlen):
    t, w = seqlen, 2 * DV
    n_steps = N_HEADS * DV // w
    half = N_HEADS * BC_DK // LANES
    out = pl.BlockSpec((t, w), lambda b, g: (b, g))
    table = pl.BlockSpec((t, LANES), lambda b, g: (0, 0), pipeline_mode=pl.Buffered(1))
    return pl.pallas_call(
        _mixers_kernel,
        grid=(bsz, n_steps),
        in_specs=[_zspec(t, ZB_PAIR_COLS, 0), _zspec(t, ZF_PAIR_COLS, 0),
                  pl.BlockSpec((t, LANES), lambda b, g: (b, ZF_LR // LANES)),
                  _vspec(w), _vspec(w),
                  table, table, _vspec(LANES), _vspec(LANES), _vspec(w),
                  pl.BlockSpec((LANES, LANES), lambda b, g: (0, g)),
                  pl.BlockSpec((LANES, LANES), lambda b, g: (0, half + g)),
                  _vspec(LANES), _vspec(LANES, half), _vspec(w)],
        out_specs=[out, out, out],
        out_shape=[jax.ShapeDtypeStruct((bsz * t, n_steps * w), BF16)] * 3,
        scratch_shapes=(_mixer_scratch(t, 2, 1, True) + _mixer_scratch(t, 1, 2, False)
                        + _mixer_scratch(t, 1, 2, True)),
        compiler_params=pltpu.CompilerParams(dimension_semantics=("parallel", "parallel"),
                                             vmem_limit_bytes=VMEM_LIMIT_MIXERS),
        name="mixers",
    )(zb, zf, zf, lb, gain_a, cos, sin, lgf, lgb, gain_b, wal, wal, bal, bal, gain_c)


def _split_w_in(w):
    w = w.astype(BF16)
    o = {}
    off = 0
    for name, size in (("a_q", 512), ("a_ff", 512), ("a_fb", 512), ("a_i", 512), ("a_g", 512),
                       ("b_q", 256), ("b_k", 256), ("b_v", 512), ("b_g", 512),
                       ("c_q", 256), ("c_k", 256), ("c_v", 512), ("c_g", 512), ("c_lr", 32),
                       ("gate_a", 1024), ("gate_b", 1024), ("gate_c", 1024)):
        o[name] = w[..., off:off + size]
        off += size
    n_pairs = N_HEADS // 2
    part = lambda name, g: o[name][..., g * (o[name].shape[-1] // n_pairs):(g + 1) * (o[name].shape[-1] // n_pairs)]
    pairs = [part(n, g) for g in range(n_pairs)
             for n in ("a_q", "a_i", "a_g", "c_q", "c_k", "c_v", "c_g", "b_q", "b_k", "b_v", "b_g")]
    wb = jnp.concatenate(pairs + [o["gate_a"], o["gate_b"], o["gate_c"]], axis=-1)
    pad = jnp.zeros(w.shape[:-1] + (ZF_COLS - ZF_LR - 2 * GLA_RANK,), w.dtype)
    gates = [part(n, g) for g in range(n_pairs) for n in ("a_ff", "a_fb")]
    wf = jnp.concatenate(gates + [o["c_lr"], pad], axis=-1)
    return wb, wf


def _rotary_tables(seqlen):
    pos = jnp.arange(seqlen, dtype=F32)
    inv_freq = 10000.0 ** (-jnp.arange(0, BC_DK, 2, dtype=F32) / BC_DK)
    ang = pos[:, None] * inv_freq[None, :]
    cos, sin = jnp.cos(ang), jnp.sin(ang)
    reps = LANES // BC_DK
    return (jnp.tile(jnp.concatenate([cos, cos], axis=1), (1, reps)),
            jnp.tile(jnp.concatenate([-sin, sin], axis=1), (1, reps)))


def _retention_log_decays():
    h = jnp.arange(N_HEADS, dtype=F32)
    fwd = jnp.log1p(-jnp.exp2(-5.0 - h))
    bwd = jnp.log1p(-jnp.exp2(-5.0 - h[::-1]))
    spread = lambda v: jnp.repeat(v, BC_DK)[None, :]
    return spread(fwd), spread(bwd)


def kernel(x, c, norm1_g, w_ada, b_ada, w_in, lb_logits, norm_a_g, norm_b_g, norm_c_g, w_alpha, b_alpha,
           w_pa, w_pb, w_pc, w_out, norm2_g, w_ffn_in, w_ffn_out, norm_f_g):
    bsz, seqlen, d = x.shape
    depth = w_in.shape[0]
    assert d == D_MODEL and seqlen % (2 * CHUNK) == 0
    tm = min(TM_DENSE, seqlen)
    tm_in = min(TM_INPROJ, seqlen)
    assert seqlen % tm == 0 and seqlen % tm_in == 0

    mod = _modulation(c, w_ada, b_ada)
    lbs = _lower_bounds(lb_logits)
    cos, sin = _rotary_tables(seqlen)
    lgf, lgb = _retention_log_decays()
    wb_all, wf_all = _split_w_in(w_in)
    dense_weights = [w.astype(BF16) for w in (w_pa, w_pb, w_pc, w_out)]
    ffn_weights = (w_ffn_in.astype(BF16), w_ffn_out.astype(BF16))

    x2 = x.reshape(bsz * seqlen, d)
    for l in range(depth):
        sh1, sc1, g1, sh2, sc2, g2 = [mod[l, :, i * d:(i + 1) * d].reshape(bsz, 1, d) for i in range(6)]
        gain1 = norm1_g[l].reshape(1, d)
        zb, zf = _inproj(x2, sh1, sc1, gain1, wb_all, wf_all, l, seqlen, tm_in)

        wal = jnp.zeros((LANES, 2 * N_HEADS * BC_DK), F32)
        wal = wal.at[0:GLA_RANK, 0:N_HEADS * BC_DK].set(w_alpha[l, 0])
        wal = wal.at[GLA_RANK:2 * GLA_RANK, N_HEADS * BC_DK:].set(w_alpha[l, 1])
        bal = b_alpha[l].reshape(1, 2 * N_HEADS * BC_DK)

        ya, yb, yc = _mixers(zb, zf, lbs[l].reshape(1, -1), norm_a_g[l].reshape(1, -1),
                             cos, sin, lgf, lgb, norm_b_g[l].reshape(1, -1),
                             wal.astype(BF16), bal, norm_c_g[l].reshape(1, -1), bsz, seqlen)

        x2 = _merge_ffn(x2, ya, yb, yc, zb, *dense_weights, g1, sh2, sc2, g2, norm2_g[l].reshape(1, d),
                        *ffn_weights, norm_f_g.reshape(1, d), l, seqlen, tm, final_norm=(l == depth - 1))
    return x2.reshape(bsz, seqlen, d)
```

```python
import functools
from typing import Any, Callable, NamedTuple

import jax
import jax.numpy as jnp
from jax import lax
from jax.experimental import pallas as pl
from jax.experimental.pallas import tpu as pltpu

F32 = jnp.float32
BF16 = jnp.bfloat16

D_MODEL = 1024
N_HEADS = 4
BC_DK = 64
DV = 128
GLA_RANK = 16
GLA_TAU = 16.0
D_FF = 2816
EPS = 1e-6
TINY = 1e-30
LOG2_E = 1.4426950408889634

LANES = 128
CHUNK = 128
assert CHUNK <= LANES
VMEM_LIMIT = 48 * 1024 * 1024
VMEM_LIMIT_MIXERS = 58 * 1024 * 1024

TM_INPROJ, TN_INPROJ = 512, 1280
TM_DENSE = 512
TF_FFN = D_FF // 2

ZB_PAIR_COLS = 2304
PAIR_AQ, PAIR_AI, PAIR_AG = 0, 256, 512
PAIR_CQ, PAIR_CK, PAIR_CV, PAIR_CG = 768, 896, 1024, 1280
PAIR_BQ, PAIR_BK, PAIR_BV, PAIR_BG = 1536, 1664, 1792, 2048
ZB_GATE = 4608
ZB_GATE_BLOCK = 1536
ZF_PAIR_COLS = 512
PAIR_FF, PAIR_FB = 0, 256
ZF_LR = 1024
ZF_COLS = 1152


def _dot(a, b):
    return jnp.dot(a, b, preferred_element_type=F32)


def _dot_nt(a, b):
    return lax.dot_general(a, b, (((1,), (1,)), ((), ())), preferred_element_type=F32)


def _dot_tn(a, b):
    return lax.dot_general(a, b, (((0,), (0,)), ((), ())), preferred_element_type=F32)


def _sigmoid(x):
    return jax.nn.sigmoid(x)


def _mod_kernel(c_ref, w_ref, b_ref, o_ref):
    c = c_ref[...]
    a = c * _sigmoid(c)
    w = w_ref[...]
    a_hi = a.astype(BF16)
    a_lo = (a - a_hi.astype(F32)).astype(BF16)
    w_hi = w.astype(BF16)
    w_lo = (w - w_hi.astype(F32)).astype(BF16)
    o_ref[...] = _dot(a_hi, w_hi) + _dot(a_hi, w_lo) + _dot(a_lo, w_hi) + b_ref[...]


def _modulation(c, w_ada, b_ada):
    depth, d, n6 = w_ada.shape
    bsz = c.shape[0]
    return pl.pallas_call(
        _mod_kernel,
        grid=(depth, n6 // d),
        in_specs=[
            pl.BlockSpec((bsz, d), lambda l, j: (0, 0)),
            pl.BlockSpec((None, d, d), lambda l, j: (l, 0, j)),
            pl.BlockSpec((None, 1, d), lambda l, j: (l, 0, j)),
        ],
        out_specs=pl.BlockSpec((None, bsz, d), lambda l, j: (l, 0, j)),
        out_shape=jax.ShapeDtypeStruct((depth, bsz, n6), F32),
        name="adaln_modulation",
    )(c, w_ada, b_ada.reshape(depth, 1, n6))


def _lb_kernel(x_ref, o_ref):
    depth = x_ref.shape[0]
    rows = [x_ref[i:i + 1, :] for i in range(depth)]
    m = rows[0]
    for r in rows[1:]:
        m = jnp.maximum(m, r)
    e = [jnp.exp(r - m) for r in rows]
    s = e[0]
    for t in e[1:]:
        s = s + t
    p = [t / s for t in e]
    acc = p[0]
    o_ref[0:1, :] = jnp.maximum(acc - p[0], 0.0)
    for i in range(1, depth):
        acc = acc + p[i]
        o_ref[i:i + 1, :] = jnp.maximum(acc - p[0], 0.0)


def _lower_bounds(lb_logits):
    return pl.pallas_call(
        _lb_kernel,
        out_shape=jax.ShapeDtypeStruct(lb_logits.shape, F32),
        name="hgrn2_lower_bounds",
    )(lb_logits)


def _norm_mod(x, gain, sc, sh):
    y = x * lax.rsqrt(jnp.mean(x * x, axis=-1, keepdims=True) + EPS)
    return (y * gain) * (1.0 + sc) + sh


def _inproj_kernel(x_ref, sh_ref, sc_ref, g_ref, wb_ref, wf_ref, zb_ref, zf_ref, h_ref):
    h_ref[...] = _norm_mod(x_ref[...], g_ref[...], sc_ref[0], sh_ref[0]).astype(BF16)
    for j in range(zb_ref.shape[1] // TN_INPROJ):
        cols = slice(j * TN_INPROJ, (j + 1) * TN_INPROJ)
        zb_ref[:, cols] = _dot(h_ref[...], wb_ref[:, cols]).astype(zb_ref.dtype)
    zf_ref[...] = _dot(h_ref[...], wf_ref[...])


def _resident(stacked, layer):
    return pl.BlockSpec((None,) + stacked.shape[1:], lambda i: (layer, 0, 0), pipeline_mode=pl.Buffered(1))


def _inproj(x2, sh, sc, gain, wb, wf, layer, seqlen, tm):
    m, d = x2.shape
    per = seqlen // tm
    bvec = lambda i: (i // per, 0, 0)
    resident = lambda w: _resident(w, layer)
    return pl.pallas_call(
        _inproj_kernel,
        grid=(m // tm,),
        in_specs=[
            pl.BlockSpec((tm, d), lambda i: (i, 0)),
            pl.BlockSpec((1, 1, d), bvec), pl.BlockSpec((1, 1, d), bvec),
            pl.BlockSpec((1, d), lambda i: (0, 0)),
            resident(wb), resident(wf),
        ],
        out_specs=[pl.BlockSpec((tm, wb.shape[2]), lambda i: (i, 0)),
                   pl.BlockSpec((tm, wf.shape[2]), lambda i: (i, 0))],
        out_shape=[jax.ShapeDtypeStruct((m, wb.shape[2]), BF16), jax.ShapeDtypeStruct((m, wf.shape[2]), F32)],
        scratch_shapes=[pltpu.VMEM((tm, d), BF16)],
        compiler_params=pltpu.CompilerParams(dimension_semantics=("parallel",), vmem_limit_bytes=VMEM_LIMIT),
        name="inproj",
    )(x2, sh, sc, gain, wb, wf)


def _merge_ffn_kernel(x_ref, ya_ref, yb_ref, yc_ref, gates_lo_ref, gates_hi_ref,
                      wpa_ref, wpb_ref, wpc_ref, wout_ref, g1_ref,
                      sh_ref, sc_ref, g2_ref, ng_ref, wi_ref, wo_ref, nf_ref, o_ref, h_ref, *, final_norm):
    d = x_ref.shape[1]
    dff = wo_ref.shape[0]
    gates = jnp.concatenate([gates_lo_ref[...], gates_hi_ref[...]], axis=1).astype(F32)
    merged = (_sigmoid(gates[:, 0:d]) * _dot(ya_ref[...], wpa_ref[...])
              + _sigmoid(gates[:, d:2 * d]) * _dot(yb_ref[...], wpb_ref[...])
              + _sigmoid(gates[:, 2 * d:3 * d]) * _dot(yc_ref[...], wpc_ref[...]))
    x = x_ref[...] + g1_ref[0] * _dot(merged.astype(BF16), wout_ref[...])
    h_ref[...] = _norm_mod(x, ng_ref[...], sc_ref[0], sh_ref[0]).astype(BF16)
    acc = None
    for k in range(dff // TF_FFN):
        gate = _dot(h_ref[...], wi_ref[:, k * TF_FFN:(k + 1) * TF_FFN])
        up = _dot(h_ref[...], wi_ref[:, dff + k * TF_FFN:dff + (k + 1) * TF_FFN])
        act = (gate * _sigmoid(gate) * up).astype(BF16)
        part = _dot(act, wo_ref[k * TF_FFN:(k + 1) * TF_FFN, :])
        acc = part if acc is None else acc + part
    xn = x + g2_ref[0] * acc
    if final_norm:
        xn = xn * lax.rsqrt(jnp.mean(xn * xn, axis=-1, keepdims=True) + EPS) * nf_ref[...]
    o_ref[...] = xn


def _merge_ffn(x2, ya, yb, yc, zb, wpa, wpb, wpc, wout, g1, sh, sc, g2, ng, w_in, w_out, nf, layer, seqlen, tm,
               final_norm):
    m, d = x2.shape
    w = ya.shape[1]
    per = seqlen // tm
    row = lambda i: (i, 0)
    bvec = pl.BlockSpec((1, 1, d), lambda i: (i // per, 0, 0))
    vec = pl.BlockSpec((1, d), lambda i: (0, 0))
    resident = lambda a: _resident(a, layer)
    return pl.pallas_call(
        functools.partial(_merge_ffn_kernel, final_norm=final_norm),
        grid=(m // tm,),
        in_specs=[
            pl.BlockSpec((tm, d), row),
            pl.BlockSpec((tm, w), row), pl.BlockSpec((tm, w), row), pl.BlockSpec((tm, w), row),
            pl.BlockSpec((tm, ZB_GATE_BLOCK), lambda i: (i, ZB_GATE // ZB_GATE_BLOCK)),
            pl.BlockSpec((tm, ZB_GATE_BLOCK), lambda i: (i, ZB_GATE // ZB_GATE_BLOCK + 1)),
            resident(wpa), resident(wpb), resident(wpc), resident(wout), bvec,
            bvec, bvec, bvec, vec, resident(w_in), resident(w_out), vec,
        ],
        out_specs=pl.BlockSpec((tm, d), row),
        out_shape=jax.ShapeDtypeStruct((m, d), F32),
        scratch_shapes=[pltpu.VMEM((tm, d), BF16)],
        compiler_params=pltpu.CompilerParams(dimension_semantics=("parallel",),
                                             vmem_limit_bytes=VMEM_LIMIT_MIXERS),
        name="merge_ffn",
    )(x2, ya, yb, yc, zb, zb, wpa, wpb, wpc, wout, g1, sh, sc, g2, ng, w_in, w_out, nf)


N_LEVELS = CHUNK.bit_length() - 1


def _gated_factors(g2, forward):
    row = lax.broadcasted_iota(jnp.int32, g2.shape, 0)
    yield jnp.exp2(g2), None, None
    lb, tb = g2, g2
    half = CHUNK // 2
    b = 1
    while b < half:
        right = (row & b) != 0
        from_left = pltpu.roll(tb, b, axis=0)
        from_right = pltpu.roll(tb, CHUNK - b, axis=0)
        lb = lb + jnp.where(right, from_left, 0.0)
        tb = tb + jnp.where(right, from_left, from_right)
        b *= 2
        rest = tb - lb
        fq, fk = (jnp.exp2(lb), jnp.exp2(rest)) if forward else (jnp.exp2(rest + g2), jnp.exp2(lb - g2))
        yield fq, fk, None
    decay_lo, decay_hi = jnp.exp2(tb[0:1, :]), jnp.exp2(tb[half:half + 1, :])
    after_lo = jnp.where(row >= half, decay_lo, 1.0)
    before_hi = jnp.where(row < half, decay_hi, 1.0)
    if forward:
        yield fq * after_lo, fk * before_hi, decay_lo * decay_hi
    else:
        yield fq * before_hi, fk * after_lo, decay_lo * decay_hi


def _fixed_factors(log_decay, forward):
    pos = lax.broadcasted_iota(jnp.int32, (CHUNK, LANES), 0).astype(F32)
    mid = CHUNK // 2
    chunk_decay = jnp.exp(float(CHUNK) * log_decay)
    if forward:
        return [(jnp.exp((pos - (mid - 1.0)) * log_decay), jnp.exp(((mid - 1.0) - pos) * log_decay), None),
                (jnp.exp((pos + 1.0) * log_decay), jnp.exp((CHUNK - 1.0 - pos) * log_decay), chunk_decay)]
    return [(jnp.exp((mid - pos) * log_decay), jnp.exp((pos - mid) * log_decay), None),
            (jnp.exp((CHUNK - pos) * log_decay), jnp.exp(pos * log_decay), chunk_decay)]


def _store_masks(mask_ref, hierarchical):
    row = lax.broadcasted_iota(jnp.int32, (CHUNK, CHUNK), 0)
    col = lax.broadcasted_iota(jnp.int32, (CHUNK, CHUNK), 1)
    one = lambda m: jnp.where(m, 1.0, 0.0).astype(BF16)
    mask_ref[0] = one(row == col)
    n = N_LEVELS if hierarchical else 1
    for level in range(n):
        same_level = ((row ^ col) >> level) == 1 if hierarchical else (row != col)
        mask_ref[1 + level] = one(same_level & (row > col))
        mask_ref[1 + n + level] = one(same_level & (col > row))
    return n


def _lane_masks(n_sub):
    if n_sub == 1:
        return [None]
    lane = lax.broadcasted_iota(jnp.int32, (1, LANES), 1)
    width = LANES // n_sub
    return [jnp.where((lane >= i * width) & (lane < (i + 1) * width), 1.0, 0.0) for i in range(n_sub)]


def _chunk_rows(n):
    if isinstance(n, int):
        return pl.ds(n * CHUNK, CHUNK)
    return pl.ds(pl.multiple_of(n * CHUNK, CHUNK), CHUNK)


class _Mixer(NamedTuple):
    n_groups: int
    n_sub: int
    hierarchical: bool
    prep: Callable
    load_v: Callable
    of_ref: Any
    ob_ref: Any
    st_ref: Any
    mask_ref: Any
    slots: tuple


def _scan_mixers(n_chunks, mixers, finish_block):
    levels = []
    for m in mixers:
        m.st_ref[...] = jnp.zeros_like(m.st_ref)
        levels.append(_store_masks(m.mask_ref, m.hierarchical))
        if m.hierarchical and m.n_sub > 1:
            for _, kt_ref, _, _ in m.slots:
                kt_ref[...] = jnp.zeros_like(kt_ref)

    def rows_of(n, forward):
        return _chunk_rows(n if forward else n_chunks - 1 - n)

    def chains(m):
        return enumerate((gi, forward) for gi in range(m.n_groups) for forward in (True, False))

    def stage(n, slot):
        for m in mixers:
            qt_ref, kt_ref, p0_ref, dec_ref = m.slots[slot]
            lane_masks = _lane_masks(m.n_sub)
            for c, (gi, forward) in chains(m):
                q, k, factors = m.prep(gi, forward, rows_of(n, forward))
                qb, kb = q.astype(BF16), k.astype(BF16)
                if forward:
                    for i in range(m.n_sub):
                        qm = q if lane_masks[i] is None else q * lane_masks[i]
                        diag = jnp.sum(qm * k, axis=-1, keepdims=True).astype(BF16)
                        p0_ref[gi * m.n_sub + i] = m.mask_ref[0] * diag
                for level, (fq, fk, chunk_decay) in enumerate(factors):
                    qt_ref[c, level] = qb * fq.astype(BF16)
                    if chunk_decay is not None:
                        dec_ref[c] = chunk_decay
                    if m.hierarchical and chunk_decay is None:
                        kt = (kb if fk is None else kb * fk.astype(BF16)).T
                        width = LANES // m.n_sub
                        for i in range(m.n_sub):
                            kt_ref[c, level, i, i * width:(i + 1) * width, 0:CHUNK] = kt[i * width:(i + 1) * width, :]
                    else:
                        kt_ref[c, level, 0, 0:CHUNK, :] = kb * fk.astype(BF16)

    def consume(n, slot):
        for m, n_levels in zip(mixers, levels):
            qt_ref, kt_ref, p0_ref, dec_ref = m.slots[slot]
            lane_masks = _lane_masks(m.n_sub)
            for c, (gi, forward) in chains(m):
                rows = rows_of(n, forward)
                o_ref = m.of_ref if forward else m.ob_ref
                mask0 = 1 if forward else 1 + n_levels
                for i, v in enumerate(m.load_v(gi, rows)):
                    head = gi * m.n_sub + i
                    sub = (lambda t: t) if lane_masks[i] is None else (lambda t, lm=lane_masks[i].astype(BF16): t * lm)
                    p = p0_ref[head] if forward else None
                    for level in range(n_levels):
                        if m.hierarchical:
                            s = _dot(qt_ref[c, level], kt_ref[c, level, i, :, 0:CHUNK])
                        else:
                            s = _dot_nt(sub(qt_ref[c, level]), kt_ref[c, level, 0, 0:CHUNK, :])
                        s = s.astype(BF16) * m.mask_ref[mask0 + level]
                        p = s if p is None else p + s
                    st = m.st_ref[c * m.n_sub + i]
                    inter = _dot_nt(sub(qt_ref[c, n_levels]), st.astype(BF16))
                    o_ref[rows, head * DV:(head + 1) * DV] = (_dot(p, v) + inter).astype(o_ref.dtype)
                    m.st_ref[c * m.n_sub + i] = st * dec_ref[c] + _dot_tn(v, kt_ref[c, n_levels, 0, 0:CHUNK, :])

    def scan_step(step):
        n = 2 * step
        consume(n, 0)
        stage(n + 1, 1)
        consume(n + 1, 1)
        stage(jnp.minimum(n + 2, n_chunks - 1), 0)

    def body(step, carry):
        scan_step(step)
        return carry

    def body_finishing(step, carry):
        finish_block(step - 1)
        finish_block(n_steps - step)
        scan_step(step)
        return carry

    n_steps = n_chunks // 2
    first_finishing = n_steps // 2 + 1
    stage(0, 0)
    lax.fori_loop(0, min(first_finishing, n_steps), body, 0)
    lax.fori_loop(first_finishing, n_steps, body_finishing, 0)
    finish_block(n_steps - 1)
    if n_steps > 1:
        finish_block(0)


def _finish_block(mixer, gain_ref, gate_ref, y_ref, center, swish, block):
    rows_per = 2 * CHUNK
    start = block * rows_per
    rows = pl.ds(start if isinstance(block, int) else pl.multiple_of(start, rows_per), rows_per)
    for i in range(mixer.n_groups * mixer.n_sub):
        cols = slice(i * DV, (i + 1) * DV)
        o = mixer.of_ref[rows, cols].astype(F32) + mixer.ob_ref[rows, cols].astype(F32)
        if center:
            o = o - jnp.mean(o, axis=-1, keepdims=True)
        o = o * lax.rsqrt(jnp.mean(o * o, axis=-1, keepdims=True) + EPS)
        gz = gate_ref[rows, cols].astype(F32)
        act = gz * _sigmoid(gz) if swish else _sigmoid(gz)
        y_ref[rows, cols] = ((o * gain_ref[:, cols]) * act).astype(y_ref.dtype)


def _group_cols(gi):
    return slice(gi * LANES, (gi + 1) * LANES)


N_SLOTS = 2
N_MIXER_SCRATCH = 4 + 4 * N_SLOTS


def _mixer(n_groups, n_sub, hierarchical, prep, load_v, scratch):
    of_ref, ob_ref, st_ref, mask_ref = scratch[:4]
    return _Mixer(n_groups, n_sub, hierarchical, prep, load_v, of_ref, ob_ref, st_ref, mask_ref,
                  tuple(scratch[4 + 4 * s:8 + 4 * s] for s in range(N_SLOTS)))


def _mixer_scratch(seqlen, n_groups, n_sub, hierarchical):
    n_heads, n_chains = n_groups * n_sub, 2 * n_groups
    n_levels = N_LEVELS if hierarchical else 1
    key_copies = n_sub if hierarchical else 1
    slot = [pltpu.VMEM((n_chains, n_levels + 1, CHUNK, LANES), BF16),
            pltpu.VMEM((n_chains, n_levels + 1, key_copies, LANES, LANES), BF16),
            pltpu.VMEM((n_heads, CHUNK, CHUNK), BF16),
            pltpu.VMEM((n_chains, 1, LANES), F32)]
    return [pltpu.VMEM((seqlen, n_heads * DV), BF16),
            pltpu.VMEM((seqlen, n_heads * DV), BF16),
            pltpu.VMEM((2 * n_heads, DV, LANES), F32),
            pltpu.VMEM((1 + 2 * n_levels, CHUNK, CHUNK), BF16)] + N_SLOTS * slot


def _hgrn2_mixer(q_ref, v_ref, ff_ref, fb_ref, lb_ref, scratch):
    def prep(gi, forward, rows):
        cols = _group_cols(gi)
        lb = lb_ref[:, cols]
        z = (ff_ref if forward else fb_ref)[rows, cols]
        f = lb + (1.0 - lb) * _sigmoid(z)
        g2 = jnp.log2(jnp.maximum(f, TINY))
        key = 1.0 - f
        aq = q_ref[rows, cols].astype(F32)
        return aq * _sigmoid(aq), key, _gated_factors(g2, forward)

    return _mixer(q_ref.shape[1] // LANES, 1, True, prep,
                  lambda gi, rows: [v_ref[rows, _group_cols(gi)]], scratch)


def _rotate_half(x):
    lane = lax.broadcasted_iota(jnp.int32, x.shape, 1)
    half = BC_DK // 2
    first = (lane % BC_DK) < half
    return jnp.where(first, pltpu.roll(x, LANES - half, axis=1), pltpu.roll(x, half, axis=1))


def _sub_values(v_ref, gi, rows):
    return [v_ref[rows, (2 * gi + i) * DV:(2 * gi + i + 1) * DV] for i in range(2)]


def _retention_mixer(q_ref, k_ref, v_ref, cos_ref, sin_ref, lgf_ref, lgb_ref, scratch):
    n_groups = q_ref.shape[1] // LANES
    packs = [(_fixed_factors(lgb_ref[:, _group_cols(gi)], False), _fixed_factors(lgf_ref[:, _group_cols(gi)], True))
             for gi in range(n_groups)]

    def prep(gi, forward, rows):
        cols = _group_cols(gi)
        cos = cos_ref[rows, :]
        sin = sin_ref[rows, :]
        q = q_ref[rows, cols].astype(F32)
        k = k_ref[rows, cols].astype(F32)
        q = q * cos + _rotate_half(q) * sin
        k = (k * cos + _rotate_half(k) * sin) * (BC_DK ** -0.5)
        return q, k, packs[gi][1 if forward else 0]

    return _mixer(n_groups, 2, False, prep, functools.partial(_sub_values, v_ref), scratch)


def _gla_mixer(q_ref, k_ref, v_ref, lr_ref, waf_ref, wab_ref, baf_ref, bab_ref, scratch):
    def prep(gi, forward, rows):
        cols = _group_cols(gi)
        w_ref, b_ref = (waf_ref, baf_ref) if forward else (wab_ref, bab_ref)
        x = _dot(lr_ref[rows, :].astype(BF16), w_ref[:, cols]) + b_ref[:, cols]
        log_sig = jnp.minimum(x, 0.0) - jnp.log(1.0 + jnp.exp(-jnp.abs(x)))
        g2 = log_sig * (LOG2_E / GLA_TAU)
        q = q_ref[rows, cols].astype(F32) * (BC_DK ** -0.5)
        return q, k_ref[rows, cols].astype(F32), _gated_factors(g2, forward)

    return _mixer(q_ref.shape[1] // LANES, 2, True, prep, functools.partial(_sub_values, v_ref), scratch)


def _cols(ref, start, width):
    return ref.at[:, start:start + width]


def _mixers_kernel(zb_ref, zf_ref, lr_ref, lb_ref, again_ref,
                   cos_ref, sin_ref, lgf_ref, lgb_ref, bgain_ref,
                   waf_ref, wab_ref, baf_ref, bab_ref, cgain_ref,
                   ya_ref, yb_ref, yc_ref, *scratch):
    pair, qk = 2 * DV, 2 * BC_DK
    s1, s2 = N_MIXER_SCRATCH, 2 * N_MIXER_SCRATCH
    hgrn2 = _hgrn2_mixer(_cols(zb_ref, PAIR_AQ, pair), _cols(zb_ref, PAIR_AI, pair),
                         _cols(zf_ref, PAIR_FF, pair), _cols(zf_ref, PAIR_FB, pair), lb_ref, scratch[:s1])
    retention = _retention_mixer(_cols(zb_ref, PAIR_BQ, qk), _cols(zb_ref, PAIR_BK, qk),
                                 _cols(zb_ref, PAIR_BV, pair), cos_ref, sin_ref, lgf_ref, lgb_ref, scratch[s1:s2])
    gla = _gla_mixer(_cols(zb_ref, PAIR_CQ, qk), _cols(zb_ref, PAIR_CK, qk), _cols(zb_ref, PAIR_CV, pair),
                     lr_ref, waf_ref, wab_ref, baf_ref, bab_ref, scratch[s2:])
    def finish_block(block):
        _finish_block(hgrn2, again_ref, _cols(zb_ref, PAIR_AG, pair), ya_ref, False, False, block)
        _finish_block(retention, bgain_ref, _cols(zb_ref, PAIR_BG, pair), yb_ref, True, True, block)
        _finish_block(gla, cgain_ref, _cols(zb_ref, PAIR_CG, pair), yc_ref, False, True, block)

    _scan_mixers(zb_ref.shape[0] // CHUNK, [hgrn2, retention, gla], finish_block)


def _zspec(seqlen, width, col0):
    blk = col0 // width
    return pl.BlockSpec((seqlen, width), lambda b, g: (b, blk + g))


def _vspec(width, blk0=0):
    return pl.BlockSpec((1, width), lambda b, g: (0, blk0 + g))


def _mixers(zb, zf, lb, gain_a, cos, sin, lgf, lgb, gain_b, wal, bal, gain_c, bsz, seqlen):
    t, w = seqlen, 2 * DV
    n_steps = N_HEADS * DV // w
    half = N_HEADS * BC_DK // LANES
    out = pl.BlockSpec((t, w), lambda b, g: (b, g))
    table = pl.BlockSpec((t, LANES), lambda b, g: (0, 0), pipeline_mode=pl.Buffered(1))
    return pl.pallas_call(
        _mixers_kernel,
        grid=(bsz, n_steps),
        in_specs=[_zspec(t, ZB_PAIR_COLS, 0), _zspec(t, ZF_PAIR_COLS, 0),
                  pl.BlockSpec((t, LANES), lambda b, g: (b, ZF_LR // LANES)),
                  _vspec(w), _vspec(w),
                  table, table, _vspec(LANES), _vspec(LANES), _vspec(w),
                  pl.BlockSpec((LANES, LANES), lambda b, g: (0, g)),
                  pl.BlockSpec((LANES, LANES), lambda b, g: (0, half + g)),
                  _vspec(LANES), _vspec(LANES, half), _vspec(w)],
        out_specs=[out, out, out],
        out_shape=[jax.ShapeDtypeStruct((bsz * t, n_steps * w), BF16)] * 3,
        scratch_shapes=(_mixer_scratch(t, 2, 1, True) + _mixer_scratch(t, 1, 2, False)
                        + _mixer_scratch(t, 1, 2, True)),
        compiler_params=pltpu.CompilerParams(dimension_semantics=("parallel", "parallel"),
                                             vmem_limit_bytes=VMEM_LIMIT_MIXERS),
        name="mixers",
    )(zb, zf, zf, lb, gain_a, cos, sin, lgf, lgb, gain_b, wal, wal, bal, bal, gain_c)


def _split_w_in(w):
    w = w.astype(BF16)
    o = {}
    off = 0
    for name, size in (("a_q", 512), ("a_ff", 512), ("a_fb", 512), ("a_i", 512), ("a_g", 512),
                       ("b_q", 256), ("b_k", 256), ("b_v", 512), ("b_g", 512),
                       ("c_q", 256), ("c_k", 256), ("c_v", 512), ("c_g", 512), ("c_lr", 32),
                       ("gate_a", 1024), ("gate_b", 1024), ("gate_c", 1024)):
        o[name] = w[..., off:off + size]
        off += size
    n_pairs = N_HEADS // 2
    part = lambda name, g: o[name][..., g * (o[name].shape[-1] // n_pairs):(g + 1) * (o[name].shape[-1] // n_pairs)]
    pairs = [part(n, g) for g in range(n_pairs)
             for n in ("a_q", "a_i", "a_g", "c_q", "c_k", "c_v", "c_g", "b_q", "b_k", "b_v", "b_g")]
    wb = jnp.concatenate(pairs + [o["gate_a"], o["gate_b"], o["gate_c"]], axis=-1)
    pad = jnp.zeros(w.shape[:-1] + (ZF_COLS - ZF_LR - 2 * GLA_RANK,), w.dtype)
    gates = [part(n, g) for g in range(n_pairs) for n in ("a_ff", "a_fb")]
    wf = jnp.concatenate(gates + [o["c_lr"], pad], axis=-1)
    return wb, wf


def _rotary_tables(seqlen):
    pos = jnp.arange(seqlen, dtype=F32)
    inv_freq = 10000.0 ** (-jnp.arange(0, BC_DK, 2, dtype=F32) / BC_DK)
    ang = pos[:, None] * inv_freq[None, :]
    cos, sin = jnp.cos(ang), jnp.sin(ang)
    reps = LANES // BC_DK
    return (jnp.tile(jnp.concatenate([cos, cos], axis=1), (1, reps)),
            jnp.tile(jnp.concatenate([-sin, sin], axis=1), (1, reps)))


def _retention_log_decays():
    h = jnp.arange(N_HEADS, dtype=F32)
    fwd = jnp.log1p(-jnp.exp2(-5.0 - h))
    bwd = jnp.log1p(-jnp.exp2(-5.0 - h[::-1]))
    spread = lambda v: jnp.repeat(v, BC_DK)[None, :]
    return spread(fwd), spread(bwd)


def kernel(x, c, norm1_g, w_ada, b_ada, w_in, lb_logits, norm_a_g, norm_b_g, norm_c_g, w_alpha, b_alpha,
           w_pa, w_pb, w_pc, w_out, norm2_g, w_ffn_in, w_ffn_out, norm_f_g):
    bsz, seqlen, d = x.shape
    depth = w_in.shape[0]
    assert d == D_MODEL and seqlen % (2 * CHUNK) == 0
    tm = min(TM_DENSE, seqlen)
    tm_in = min(TM_INPROJ, seqlen)
    assert seqlen % tm == 0 and seqlen % tm_in == 0

    mod = _modulation(c, w_ada, b_ada)
    lbs = _lower_bounds(lb_logits)
    cos, sin = _rotary_tables(seqlen)
    lgf, lgb = _retention_log_decays()
    wb_all, wf_all = _split_w_in(w_in)
    dense_weights = [w.astype(BF16) for w in (w_pa, w_pb, w_pc, w_out)]
    ffn_weights = (w_ffn_in.astype(BF16), w_ffn_out.astype(BF16))

    x2 = x.reshape(bsz * seqlen, d)
    for l in range(depth):
        sh1, sc1, g1, sh2, sc2, g2 = [mod[l, :, i * d:(i + 1) * d].reshape(bsz, 1, d) for i in range(6)]
        gain1 = norm1_g[l].reshape(1, d)
        zb, zf = _inproj(x2, sh1, sc1, gain1, wb_all, wf_all, l, seqlen, tm_in)

        wal = jnp.zeros((LANES, 2 * N_HEADS * BC_DK), F32)
        wal = wal.at[0:GLA_RANK, 0:N_HEADS * BC_DK].set(w_alpha[l, 0])
        wal = wal.at[GLA_RANK:2 * GLA_RANK, N_HEADS * BC_DK:].set(w_alpha[l, 1])
        bal = b_alpha[l].reshape(1, 2 * N_HEADS * BC_DK)

        ya, yb, yc = _mixers(zb, zf, lbs[l].reshape(1, -1), norm_a_g[l].reshape(1, -1),
                             cos, sin, lgf, lgb, norm_b_g[l].reshape(1, -1),
                             wal.astype(BF16), bal, norm_c_g[l].reshape(1, -1), bsz, seqlen)

        x2 = _merge_ffn(x2, ya, yb, yc, zb, *dense_weights, g1, sh2, sc2, g2, norm2_g[l].reshape(1, d),
                        *ffn_weights, norm_f_g.reshape(1, d), l, seqlen, tm, final_norm=(l == depth - 1))
    return x2.reshape(bsz, seqlen, d)
```

```python
import functools
from typing import Any, Callable, NamedTuple

import jax
import jax.numpy as jnp
from jax import lax
from jax.experimental import pallas as pl
from jax.experimental.pallas import tpu as pltpu

F32 = jnp.float32
BF16 = jnp.bfloat16

D_MODEL = 1024
N_HEADS = 4
BC_DK = 64
DV = 128
GLA_RANK = 16
GLA_TAU = 16.0
D_FF = 2816
EPS = 1e-6
TINY = 1e-30
LOG2_E = 1.4426950408889634

LANES = 128
CHUNK = 128
assert CHUNK <= LANES
VMEM_LIMIT = 48 * 1024 * 1024
VMEM_LIMIT_MIXERS = 61 * 1024 * 1024

TM_INPROJ, TN_INPROJ = 512, 1280
TM_DENSE = 512
TF_FFN = D_FF // 2

ZB_PAIR_COLS = 2304
PAIR_AQ, PAIR_AI, PAIR_AG = 0, 256, 512
PAIR_CQ, PAIR_CK, PAIR_CV, PAIR_CG = 768, 896, 1024, 1280
PAIR_BQ, PAIR_BK, PAIR_BV, PAIR_BG = 1536, 1664, 1792, 2048
ZB_GATE = 4608
ZB_GATE_BLOCK = 1536
ZF_PAIR_COLS = 512
PAIR_FF, PAIR_FB = 0, 256
ZF_LR = 1024
ZF_COLS = 1152


def _dot(a, b):
    return jnp.dot(a, b, preferred_element_type=F32)


def _dot_nt(a, b):
    return lax.dot_general(a, b, (((1,), (1,)), ((), ())), preferred_element_type=F32)


def _dot_tn(a, b):
    return lax.dot_general(a, b, (((0,), (0,)), ((), ())), preferred_element_type=F32)


def _sigmoid(x):
    return jax.nn.sigmoid(x)


def _mod_kernel(c_ref, w_ref, b_ref, o_ref):
    c = c_ref[...]
    a = c * _sigmoid(c)
    w = w_ref[...]
    a_hi = a.astype(BF16)
    a_lo = (a - a_hi.astype(F32)).astype(BF16)
    w_hi = w.astype(BF16)
    w_lo = (w - w_hi.astype(F32)).astype(BF16)
    o_ref[...] = _dot(a_hi, w_hi) + _dot(a_hi, w_lo) + _dot(a_lo, w_hi) + b_ref[...]


def _modulation(c, w_ada, b_ada):
    depth, d, n6 = w_ada.shape
    bsz = c.shape[0]
    return pl.pallas_call(
        _mod_kernel,
        grid=(depth, n6 // d),
        in_specs=[
            pl.BlockSpec((bsz, d), lambda l, j: (0, 0)),
            pl.BlockSpec((None, d, d), lambda l, j: (l, 0, j)),
            pl.BlockSpec((None, 1, d), lambda l, j: (l, 0, j)),
        ],
        out_specs=pl.BlockSpec((None, bsz, d), lambda l, j: (l, 0, j)),
        out_shape=jax.ShapeDtypeStruct((depth, bsz, n6), F32),
        name="adaln_modulation",
    )(c, w_ada, b_ada.reshape(depth, 1, n6))


def _lb_kernel(x_ref, o_ref):
    depth = x_ref.shape[0]
    rows = [x_ref[i:i + 1, :] for i in range(depth)]
    m = rows[0]
    for r in rows[1:]:
        m = jnp.maximum(m, r)
    e = [jnp.exp(r - m) for r in rows]
    s = e[0]
    for t in e[1:]:
        s = s + t
    p = [t / s for t in e]
    acc = p[0]
    o_ref[0:1, :] = jnp.maximum(acc - p[0], 0.0)
    for i in range(1, depth):
        acc = acc + p[i]
        o_ref[i:i + 1, :] = jnp.maximum(acc - p[0], 0.0)


def _lower_bounds(lb_logits):
    return pl.pallas_call(
        _lb_kernel,
        out_shape=jax.ShapeDtypeStruct(lb_logits.shape, F32),
        name="hgrn2_lower_bounds",
    )(lb_logits)


def _norm_mod(x, gain, sc, sh):
    y = x * lax.rsqrt(jnp.mean(x * x, axis=-1, keepdims=True) + EPS)
    return (y * gain) * (1.0 + sc) + sh


def _inproj_kernel(x_ref, sh_ref, sc_ref, g_ref, wb_ref, wf_ref, zb_ref, zf_ref, h_ref):
    h_ref[...] = _norm_mod(x_ref[...], g_ref[...], sc_ref[0], sh_ref[0]).astype(BF16)
    for j in range(zb_ref.shape[1] // TN_INPROJ):
        cols = slice(j * TN_INPROJ, (j + 1) * TN_INPROJ)
        zb_ref[:, cols] = _dot(h_ref[...], wb_ref[:, cols]).astype(zb_ref.dtype)
    zf_ref[...] = _dot(h_ref[...], wf_ref[...])


def _resident(stacked, layer):
    return pl.BlockSpec((None,) + stacked.shape[1:], lambda i: (layer, 0, 0), pipeline_mode=pl.Buffered(1))


def _inproj(x2, sh, sc, gain, wb, wf, layer, seqlen, tm):
    m, d = x2.shape
    per = seqlen // tm
    bvec = lambda i: (i // per, 0, 0)
    resident = lambda w: _resident(w, layer)
    return pl.pallas_call(
        _inproj_kernel,
        grid=(m // tm,),
        in_specs=[
            pl.BlockSpec((tm, d), lambda i: (i, 0)),
            pl.BlockSpec((1, 1, d), bvec), pl.BlockSpec((1, 1, d), bvec),
            pl.BlockSpec((1, d), lambda i: (0, 0)),
            resident(wb), resident(wf),
        ],
        out_specs=[pl.BlockSpec((tm, wb.shape[2]), lambda i: (i, 0)),
                   pl.BlockSpec((tm, wf.shape[2]), lambda i: (i, 0))],
        out_shape=[jax.ShapeDtypeStruct((m, wb.shape[2]), BF16), jax.ShapeDtypeStruct((m, wf.shape[2]), F32)],
        scratch_shapes=[pltpu.VMEM((tm, d), BF16)],
        compiler_params=pltpu.CompilerParams(dimension_semantics=("parallel",), vmem_limit_bytes=VMEM_LIMIT),
        name="inproj",
    )(x2, sh, sc, gain, wb, wf)


def _merge_ffn_kernel(x_ref, ya_ref, yb_ref, yc_ref, gates_lo_ref, gates_hi_ref,
                      wpa_ref, wpb_ref, wpc_ref, wout_ref, g1_ref,
                      sh_ref, sc_ref, g2_ref, ng_ref, wi_ref, wo_ref, nf_ref, o_ref, h_ref, *, final_norm):
    d = x_ref.shape[1]
    dff = wo_ref.shape[0]
    gates = jnp.concatenate([gates_lo_ref[...], gates_hi_ref[...]], axis=1).astype(F32)
    merged = (_sigmoid(gates[:, 0:d]) * _dot(ya_ref[...], wpa_ref[...])
              + _sigmoid(gates[:, d:2 * d]) * _dot(yb_ref[...], wpb_ref[...])
              + _sigmoid(gates[:, 2 * d:3 * d]) * _dot(yc_ref[...], wpc_ref[...]))
    x = x_ref[...] + g1_ref[0] * _dot(merged.astype(BF16), wout_ref[...])
    h_ref[...] = _norm_mod(x, ng_ref[...], sc_ref[0], sh_ref[0]).astype(BF16)
    acc = None
    for k in range(dff // TF_FFN):
        gate = _dot(h_ref[...], wi_ref[:, k * TF_FFN:(k + 1) * TF_FFN])
        up = _dot(h_ref[...], wi_ref[:, dff + k * TF_FFN:dff + (k + 1) * TF_FFN])
        act = (gate * _sigmoid(gate) * up).astype(BF16)
        part = _dot(act, wo_ref[k * TF_FFN:(k + 1) * TF_FFN, :])
        acc = part if acc is None else acc + part
    xn = x + g2_ref[0] * acc
    if final_norm:
        xn = xn * lax.rsqrt(jnp.mean(xn * xn, axis=-1, keepdims=True) + EPS) * nf_ref[...]
    o_ref[...] = xn


def _merge_ffn(x2, ya, yb, yc, zb, wpa, wpb, wpc, wout, g1, sh, sc, g2, ng, w_in, w_out, nf, layer, seqlen, tm,
               final_norm):
    m, d = x2.shape
    w = ya.shape[1]
    per = seqlen // tm
    row = lambda i: (i, 0)
    bvec = pl.BlockSpec((1, 1, d), lambda i: (i // per, 0, 0))
    vec = pl.BlockSpec((1, d), lambda i: (0, 0))
    resident = lambda a: _resident(a, layer)
    return pl.pallas_call(
        functools.partial(_merge_ffn_kernel, final_norm=final_norm),
        grid=(m // tm,),
        in_specs=[
            pl.BlockSpec((tm, d), row),
            pl.BlockSpec((tm, w), row), pl.BlockSpec((tm, w), row), pl.BlockSpec((tm, w), row),
            pl.BlockSpec((tm, ZB_GATE_BLOCK), lambda i: (i, ZB_GATE // ZB_GATE_BLOCK)),
            pl.BlockSpec((tm, ZB_GATE_BLOCK), lambda i: (i, ZB_GATE // ZB_GATE_BLOCK + 1)),
            resident(wpa), resident(wpb), resident(wpc), resident(wout), bvec,
            bvec, bvec, bvec, vec, resident(w_in), resident(w_out), vec,
        ],
        out_specs=pl.BlockSpec((tm, d), row),
        out_shape=jax.ShapeDtypeStruct((m, d), F32),
        scratch_shapes=[pltpu.VMEM((tm, d), BF16)],
        compiler_params=pltpu.CompilerParams(dimension_semantics=("parallel",),
                                             vmem_limit_bytes=VMEM_LIMIT_MIXERS),
        name="merge_ffn",
    )(x2, ya, yb, yc, zb, zb, wpa, wpb, wpc, wout, g1, sh, sc, g2, ng, w_in, w_out, nf)


N_LEVELS = CHUNK.bit_length() - 1


def _gated_factors(g2, forward):
    row = lax.broadcasted_iota(jnp.int32, g2.shape, 0)
    yield jnp.exp2(g2), None, None
    lb, tb = g2, g2
    half = CHUNK // 2
    b = 1
    while b < half:
        right = (row & b) != 0
        from_left = pltpu.roll(tb, b, axis=0)
        from_right = pltpu.roll(tb, CHUNK - b, axis=0)
        lb = lb + jnp.where(right, from_left, 0.0)
        tb = tb + jnp.where(right, from_left, from_right)
        b *= 2
        rest = tb - lb
        fq, fk = (jnp.exp2(lb), jnp.exp2(rest)) if forward else (jnp.exp2(rest + g2), jnp.exp2(lb - g2))
        yield fq, fk, None
    decay_lo, decay_hi = jnp.exp2(tb[0:1, :]), jnp.exp2(tb[half:half + 1, :])
    after_lo = jnp.where(row >= half, decay_lo, 1.0)
    before_hi = jnp.where(row < half, decay_hi, 1.0)
    if forward:
        yield fq * after_lo, fk * before_hi, decay_lo * decay_hi
    else:
        yield fq * before_hi, fk * after_lo, decay_lo * decay_hi


def _fixed_factors(log_decay, forward):
    pos = lax.broadcasted_iota(jnp.int32, (CHUNK, LANES), 0).astype(F32)
    mid = CHUNK // 2
    chunk_decay = jnp.exp(float(CHUNK) * log_decay)
    if forward:
        return [(jnp.exp((pos - (mid - 1.0)) * log_decay), jnp.exp(((mid - 1.0) - pos) * log_decay), None),
                (jnp.exp((pos + 1.0) * log_decay), jnp.exp((CHUNK - 1.0 - pos) * log_decay), chunk_decay)]
    return [(jnp.exp((mid - pos) * log_decay), jnp.exp((pos - mid) * log_decay), None),
            (jnp.exp((CHUNK - pos) * log_decay), jnp.exp(pos * log_decay), chunk_decay)]


def _store_masks(mask_ref, hierarchical):
    row = lax.broadcasted_iota(jnp.int32, (CHUNK, CHUNK), 0)
    col = lax.broadcasted_iota(jnp.int32, (CHUNK, CHUNK), 1)
    one = lambda m: jnp.where(m, 1.0, 0.0).astype(BF16)
    mask_ref[0] = one(row == col)
    n = N_LEVELS if hierarchical else 1
    for level in range(n):
        same_level = ((row ^ col) >> level) == 1 if hierarchical else (row != col)
        mask_ref[1 + level] = one(same_level & (row > col))
        mask_ref[1 + n + level] = one(same_level & (col > row))
    return n


def _lane_masks(n_sub):
    if n_sub == 1:
        return [None]
    lane = lax.broadcasted_iota(jnp.int32, (1, LANES), 1)
    width = LANES // n_sub
    return [jnp.where((lane >= i * width) & (lane < (i + 1) * width), 1.0, 0.0) for i in range(n_sub)]


def _chunk_rows(n):
    if isinstance(n, int):
        return pl.ds(n * CHUNK, CHUNK)
    return pl.ds(pl.multiple_of(n * CHUNK, CHUNK), CHUNK)


class _Mixer(NamedTuple):
    n_groups: int
    n_sub: int
    hierarchical: bool
    prep: Callable
    load_v: Callable
    of_ref: Any
    ob_ref: Any
    st_ref: Any
    mask_ref: Any
    slots: tuple


def _scan_mixers(n_chunks, mixers, finish_block):
    levels = []
    for m in mixers:
        m.st_ref[...] = jnp.zeros_like(m.st_ref)
        levels.append(_store_masks(m.mask_ref, m.hierarchical))
        if m.hierarchical and m.n_sub > 1:
            for _, kt_ref, _, _ in m.slots:
                kt_ref[...] = jnp.zeros_like(kt_ref)

    def rows_of(n, forward):
        return _chunk_rows(n if forward else n_chunks - 1 - n)

    def chains(m):
        return enumerate((gi, forward) for gi in range(m.n_groups) for forward in (True, False))

    def stage(n, slot):
        for m in mixers:
            qt_ref, kt_ref, p0_ref, dec_ref = m.slots[slot]
            lane_masks = _lane_masks(m.n_sub)
            for c, (gi, forward) in chains(m):
                q, k, factors = m.prep(gi, forward, rows_of(n, forward))
                qb, kb = q.astype(BF16), k.astype(BF16)
                if forward:
                    for i in range(m.n_sub):
                        qm = q if lane_masks[i] is None else q * lane_masks[i]
                        diag = jnp.sum(qm * k, axis=-1, keepdims=True).astype(BF16)
                        p0_ref[gi * m.n_sub + i] = m.mask_ref[0] * diag
                for level, (fq, fk, chunk_decay) in enumerate(factors):
                    qt_ref[c, level] = qb * fq.astype(BF16)
                    if chunk_decay is not None:
                        dec_ref[c] = chunk_decay
                    if m.hierarchical and chunk_decay is None:
                        kt = (kb if fk is None else kb * fk.astype(BF16)).T
                        width = LANES // m.n_sub
                        for i in range(m.n_sub):
                            kt_ref[c, level, i, i * width:(i + 1) * width, 0:CHUNK] = kt[i * width:(i + 1) * width, :]
                    else:
                        kt_ref[c, level, 0, 0:CHUNK, :] = kb * fk.astype(BF16)

    def consume(n, slot):
        for m, n_levels in zip(mixers, levels):
            qt_ref, kt_ref, p0_ref, dec_ref = m.slots[slot]
            lane_masks = _lane_masks(m.n_sub)
            for c, (gi, forward) in chains(m):
                rows = rows_of(n, forward)
                o_ref = m.of_ref if forward else m.ob_ref
                mask0 = 1 if forward else 1 + n_levels
                for i, v in enumerate(m.load_v(gi, rows)):
                    head = gi * m.n_sub + i
                    sub = (lambda t: t) if lane_masks[i] is None else (lambda t, lm=lane_masks[i].astype(BF16): t * lm)
                    p = p0_ref[head] if forward else None
                    for level in range(n_levels):
                        if m.hierarchical:
                            s = _dot(qt_ref[c, level], kt_ref[c, level, i, :, 0:CHUNK])
                        else:
                            s = _dot_nt(sub(qt_ref[c, level]), kt_ref[c, level, 0, 0:CHUNK, :])
                        s = s.astype(BF16) * m.mask_ref[mask0 + level]
                        p = s if p is None else p + s
                    st = m.st_ref[c * m.n_sub + i]
                    inter = _dot_nt(sub(qt_ref[c, n_levels]), st.astype(BF16))
                    o_ref[rows, head * DV:(head + 1) * DV] = (_dot(p, v) + inter).astype(o_ref.dtype)
                    m.st_ref[c * m.n_sub + i] = st * dec_ref[c] + _dot_tn(v, kt_ref[c, n_levels, 0, 0:CHUNK, :])

    def scan_step(step, stage_next=True):
        n = 2 * step
        consume(n, 0)
        stage(n + 1, 1)
        consume(n + 1, 1)
        if stage_next:
            stage(n + 2, 0)

    def body(step, carry):
        scan_step(step)
        return carry

    def body_finishing(step, carry):
        finish_block(step - 1)
        finish_block(n_steps - step)
        scan_step(step)
        return carry

    n_steps = n_chunks // 2
    first_finishing = n_steps // 2 + 1
    last = n_steps - 1
    stage(0, 0)
    lax.fori_loop(0, min(first_finishing, last), body, 0)
    lax.fori_loop(first_finishing, last, body_finishing, 0)
    if last >= first_finishing:
        finish_block(last - 1)
        finish_block(n_steps - last)
    scan_step(last, stage_next=False)
    finish_block(n_steps - 1)
    if n_steps > 1:
        finish_block(0)


def _finish_block(mixer, gain_ref, gate_ref, y_ref, center, swish, block):
    rows_per = 2 * CHUNK
    start = block * rows_per
    rows = pl.ds(start if isinstance(block, int) else pl.multiple_of(start, rows_per), rows_per)
    for i in range(mixer.n_groups * mixer.n_sub):
        cols = slice(i * DV, (i + 1) * DV)
        o = mixer.of_ref[rows, cols].astype(F32) + mixer.ob_ref[rows, cols].astype(F32)
        if center:
            o = o - jnp.mean(o, axis=-1, keepdims=True)
        o = o * lax.rsqrt(jnp.mean(o * o, axis=-1, keepdims=True) + EPS)
        gz = gate_ref[rows, cols].astype(F32)
        act = gz * _sigmoid(gz) if swish else _sigmoid(gz)
        y_ref[rows, cols] = ((o * gain_ref[:, cols]) * act).astype(y_ref.dtype)


def _group_cols(gi):
    return slice(gi * LANES, (gi + 1) * LANES)


N_SLOTS = 2
N_MIXER_SCRATCH = 4 + 4 * N_SLOTS


def _mixer(n_groups, n_sub, hierarchical, prep, load_v, scratch):
    of_ref, ob_ref, st_ref, mask_ref = scratch[:4]
    return _Mixer(n_groups, n_sub, hierarchical, prep, load_v, of_ref, ob_ref, st_ref, mask_ref,
                  tuple(scratch[4 + 4 * s:8 + 4 * s] for s in range(N_SLOTS)))


def _mixer_scratch(seqlen, n_groups, n_sub, hierarchical):
    n_heads, n_chains = n_groups * n_sub, 2 * n_groups
    n_levels = N_LEVELS if hierarchical else 1
    key_copies = n_sub if hierarchical else 1
    slot = [pltpu.VMEM((n_chains, n_levels + 1, CHUNK, LANES), BF16),
            pltpu.VMEM((n_chains, n_levels + 1, key_copies, LANES, LANES), BF16),
            pltpu.VMEM((n_heads, CHUNK, CHUNK), BF16),
            pltpu.VMEM((n_chains, 1, LANES), F32)]
    return [pltpu.VMEM((seqlen, n_heads * DV), BF16),
            pltpu.VMEM((seqlen, n_heads * DV), BF16),
            pltpu.VMEM((2 * n_heads, DV, LANES), F32),
            pltpu.VMEM((1 + 2 * n_levels, CHUNK, CHUNK), BF16)] + N_SLOTS * slot


def _hgrn2_mixer(q_ref, v_ref, ff_ref, fb_ref, lb_ref, scratch):
    def prep(gi, forward, rows):
        cols = _group_cols(gi)
        lb = lb_ref[:, cols]
        z = (ff_ref if forward else fb_ref)[rows, cols]
        f = lb + (1.0 - lb) * _sigmoid(z)
        g2 = jnp.log2(jnp.maximum(f, TINY))
        key = 1.0 - f
        aq = q_ref[rows, cols].astype(F32)
        return aq * _sigmoid(aq), key, _gated_factors(g2, forward)

    return _mixer(q_ref.shape[1] // LANES, 1, True, prep,
                  lambda gi, rows: [v_ref[rows, _group_cols(gi)]], scratch)


def _rotate_half(x):
    lane = lax.broadcasted_iota(jnp.int32, x.shape, 1)
    half = BC_DK // 2
    first = (lane % BC_DK) < half
    return jnp.where(first, pltpu.roll(x, LANES - half, axis=1), pltpu.roll(x, half, axis=1))


def _sub_values(v_ref, gi, rows):
    return [v_ref[rows, (2 * gi + i) * DV:(2 * gi + i + 1) * DV] for i in range(2)]


def _retention_mixer(q_ref, k_ref, v_ref, cos_ref, sin_ref, lgf_ref, lgb_ref, scratch):
    n_groups = q_ref.shape[1] // LANES
    packs = [(_fixed_factors(lgb_ref[:, _group_cols(gi)], False), _fixed_factors(lgf_ref[:, _group_cols(gi)], True))
             for gi in range(n_groups)]

    def prep(gi, forward, rows):
        cols = _group_cols(gi)
        cos = cos_ref[rows, :]
        sin = sin_ref[rows, :]
        q = q_ref[rows, cols].astype(F32)
        k = k_ref[rows, cols].astype(F32)
        q = q * cos + _rotate_half(q) * sin
        k = (k * cos + _rotate_half(k) * sin) * (BC_DK ** -0.5)
        return q, k, packs[gi][1 if forward else 0]

    return _mixer(n_groups, 2, False, prep, functools.partial(_sub_values, v_ref), scratch)


def _gla_mixer(q_ref, k_ref, v_ref, lr_ref, waf_ref, wab_ref, baf_ref, bab_ref, scratch):
    def prep(gi, forward, rows):
        cols = _group_cols(gi)
        w_ref, b_ref = (waf_ref, baf_ref) if forward else (wab_ref, bab_ref)
        x = _dot(lr_ref[rows, :].astype(BF16), w_ref[:, cols]) + b_ref[:, cols]
        log_sig = jnp.minimum(x, 0.0) - jnp.log(1.0 + jnp.exp(-jnp.abs(x)))
        g2 = log_sig * (LOG2_E / GLA_TAU)
        q = q_ref[rows, cols].astype(F32) * (BC_DK ** -0.5)
        return q, k_ref[rows, cols].astype(F32), _gated_factors(g2, forward)

    return _mixer(q_ref.shape[1] // LANES, 2, True, prep, functools.partial(_sub_values, v_ref), scratch)


def _cols(ref, start, width):
    return ref.at[:, start:start + width]


def _mixers_kernel(zb_ref, zf_ref, lr_ref, lb_ref, again_ref,
                   cos_ref, sin_ref, lgf_ref, lgb_ref, bgain_ref,
                   waf_ref, wab_ref, baf_ref, bab_ref, cgain_ref,
                   ya_ref, yb_ref, yc_ref, *scratch):
    pair, qk = 2 * DV, 2 * BC_DK
    s1, s2 = N_MIXER_SCRATCH, 2 * N_MIXER_SCRATCH
    hgrn2 = _hgrn2_mixer(_cols(zb_ref, PAIR_AQ, pair), _cols(zb_ref, PAIR_AI, pair),
                         _cols(zf_ref, PAIR_FF, pair), _cols(zf_ref, PAIR_FB, pair), lb_ref, scratch[:s1])
    retention = _retention_mixer(_cols(zb_ref, PAIR_BQ, qk), _cols(zb_ref, PAIR_BK, qk),
                                 _cols(zb_ref, PAIR_BV, pair), cos_ref, sin_ref, lgf_ref, lgb_ref, scratch[s1:s2])
    gla = _gla_mixer(_cols(zb_ref, PAIR_CQ, qk), _cols(zb_ref, PAIR_CK, qk), _cols(zb_ref, PAIR_CV, pair),
                     lr_ref, waf_ref, wab_ref, baf_ref, bab_ref, scratch[s2:])
    def finish_block(block):
        _finish_block(hgrn2, again_ref, _cols(zb_ref, PAIR_AG, pair), ya_ref, False, False, block)
        _finish_block(retention, bgain_ref, _cols(zb_ref, PAIR_BG, pair), yb_ref, True, True, block)
        _finish_block(gla, cgain_ref, _cols(zb_ref, PAIR_CG, pair), yc_ref, False, True, block)

    _scan_mixers(zb_ref.shape[0] // CHUNK, [hgrn2, retention, gla], finish_block)


def _zspec(seqlen, width, col0):
    blk = col0 // width
    return pl.BlockSpec((seqlen, width), lambda b, g: (b, blk + g))


def _vspec(width, blk0=0):
    return pl.BlockSpec((1, width), lambda b, g: (0, blk0 + g))


def _mixers(zb, zf, lb, gain_a, cos, sin, lgf, lgb, gain_b, wal, bal, gain_c, bsz, seqlen):
    t, w = seqlen, 2 * DV
    n_steps = N_HEADS * DV // w
    half = N_HEADS * BC_DK // LANES
    out = pl.BlockSpec((t, w), lambda b, g: (b, g))
    table = pl.BlockSpec((t, LANES), lambda b, g: (0, 0), pipeline_mode=pl.Buffered(1))
    return pl.pallas_call(
        _mixers_kernel,
        grid=(bsz, n_steps),
        in_specs=[_zspec(t, ZB_PAIR_COLS, 0), _zspec(t, ZF_PAIR_COLS, 0),
                  pl.BlockSpec((t, LANES), lambda b, g: (b, ZF_LR // LANES)),
                  _vspec(w), _vspec(w),
                  table, table, _vspec(LANES), _vspec(LANES), _vspec(w),
                  pl.BlockSpec((LANES, LANES), lambda b, g: (0, g)),
                  pl.BlockSpec((LANES, LANES), lambda b, g: (0, half + g)),
                  _vspec(LANES), _vspec(LANES, half), _vspec(w)],
        out_specs=[out, out, out],
        out_shape=[jax.ShapeDtypeStruct((bsz * t, n_steps * w), BF16)] * 3,
        scratch_shapes=(_mixer_scratch(t, 2, 1, True) + _mixer_scratch(t, 1, 2, False)
                        + _mixer_scratch(t, 1, 2, True)),
        compiler_params=pltpu.CompilerParams(dimension_semantics=("parallel", "parallel"),
                                             vmem_limit_bytes=VMEM_LIMIT_MIXERS),
        name="mixers",
    )(zb, zf, zf, lb, gain_a, cos, sin, lgf, lgb, gain_b, wal, wal, bal, bal, gain_c)


def _split_w_in(w):
    w = w.astype(BF16)
    o = {}
    off = 0
    for name, size in (("a_q", 512), ("a_ff", 512), ("a_fb", 512), ("a_i", 512), ("a_g", 512),
                       ("b_q", 256), ("b_k", 256), ("b_v", 512), ("b_g", 512),
                       ("c_q", 256), ("c_k", 256), ("c_v", 512), ("c_g", 512), ("c_lr", 32),
                       ("gate_a", 1024), ("gate_b", 1024), ("gate_c", 1024)):
        o[name] = w[..., off:off + size]
        off += size
    n_pairs = N_HEADS // 2
    part = lambda name, g: o[name][..., g * (o[name].shape[-1] // n_pairs):(g + 1) * (o[name].shape[-1] // n_pairs)]
    pairs = [part(n, g) for g in range(n_pairs)
             for n in ("a_q", "a_i", "a_g", "c_q", "c_k", "c_v", "c_g", "b_q", "b_k", "b_v", "b_g")]
    wb = jnp.concatenate(pairs + [o["gate_a"], o["gate_b"], o["gate_c"]], axis=-1)
    pad = jnp.zeros(w.shape[:-1] + (ZF_COLS - ZF_LR - 2 * GLA_RANK,), w.dtype)
    gates = [part(n, g) for g in range(n_pairs) for n in ("a_ff", "a_fb")]
    wf = jnp.concatenate(gates + [o["c_lr"], pad], axis=-1)
    return wb, wf


def _rotary_tables(seqlen):
    pos = jnp.arange(seqlen, dtype=F32)
    inv_freq = 10000.0 ** (-jnp.arange(0, BC_DK, 2, dtype=F32) / BC_DK)
    ang = pos[:, None] * inv_freq[None, :]
    cos, sin = jnp.cos(ang), jnp.sin(ang)
    reps = LANES // BC_DK
    return (jnp.tile(jnp.concatenate([cos, cos], axis=1), (1, reps)),
            jnp.tile(jnp.concatenate([-sin, sin], axis=1), (1, reps)))


def _retention_log_decays():
    h = jnp.arange(N_HEADS, dtype=F32)
    fwd = jnp.log1p(-jnp.exp2(-5.0 - h))
    bwd = jnp.log1p(-jnp.exp2(-5.0 - h[::-1]))
    spread = lambda v: jnp.repeat(v, BC_DK)[None, :]
    return spread(fwd), spread(bwd)


def kernel(x, c, norm1_g, w_ada, b_ada, w_in, lb_logits, norm_a_g, norm_b_g, norm_c_g, w_alpha, b_alpha,
           w_pa, w_pb, w_pc, w_out, norm2_g, w_ffn_in, w_ffn_out, norm_f_g):
    bsz, seqlen, d = x.shape
    depth = w_in.shape[0]
    assert d == D_MODEL and seqlen % (2 * CHUNK) == 0
    tm = min(TM_DENSE, seqlen)
    tm_in = min(TM_INPROJ, seqlen)
    assert seqlen % tm == 0 and seqlen % tm_in == 0

    mod = _modulation(c, w_ada, b_ada)
    lbs = _lower_bounds(lb_logits)
    cos, sin = _rotary_tables(seqlen)
    lgf, lgb = _retention_log_decays()
    wb_all, wf_all = _split_w_in(w_in)
    dense_weights = [w.astype(BF16) for w in (w_pa, w_pb, w_pc, w_out)]
    ffn_weights = (w_ffn_in.astype(BF16), w_ffn_out.astype(BF16))

    x2 = x.reshape(bsz * seqlen, d)
    for l in range(depth):
        sh1, sc1, g1, sh2, sc2, g2 = [mod[l, :, i * d:(i + 1) * d].reshape(bsz, 1, d) for i in range(6)]
        gain1 = norm1_g[l].reshape(1, d)
        zb, zf = _inproj(x2, sh1, sc1, gain1, wb_all, wf_all, l, seqlen, tm_in)

        wal = jnp.zeros((LANES, 2 * N_HEADS * BC_DK), F32)
        wal = wal.at[0:GLA_RANK, 0:N_HEADS * BC_DK].set(w_alpha[l, 0])
        wal = wal.at[GLA_RANK:2 * GLA_RANK, N_HEADS * BC_DK:].set(w_alpha[l, 1])
        bal = b_alpha[l].reshape(1, 2 * N_HEADS * BC_DK)

        ya, yb, yc = _mixers(zb, zf, lbs[l].reshape(1, -1), norm_a_g[l].reshape(1, -1),
                             cos, sin, lgf, lgb, norm_b_g[l].reshape(1, -1),
                             wal.astype(BF16), bal, norm_c_g[l].reshape(1, -1), bsz, seqlen)

        x2 = _merge_ffn(x2, ya, yb, yc, zb, *dense_weights, g1, sh2, sc2, g2, norm2_g[l].reshape(1, d),
                        *ffn_weights, norm_f_g.reshape(1, d), l, seqlen, tm, final_norm=(l == depth - 1))
    return x2.reshape(bsz, seqlen, d)
```
